```python
import math
import jax
import jax.numpy as jnp
from jax import lax
import numpy as np

D_MODEL = 1024
BATCH = 4
SEQ = 8192
DEPTH = 4
DEC_BATCH = 8
DEC_SEQ = 64
PAST_LEN = 1024

CHUNK = 64
N_BRANCH = 4
BRANCH_W = D_MODEL // 2
EPS = 1e-6

LRU_W = BRANCH_W
LRU_BLOCKS = 8
LRU_BW = LRU_W // LRU_BLOCKS
LRU_CONV = 4
LRU_C = 8.0

SSD_INNER = BRANCH_W
SSD_P = 64
SSD_H = SSD_INNER // SSD_P
SSD_G = 4
SSD_R = SSD_H // SSD_G
SSD_N = 64
SSD_CONV = 4
SSD_CONV_DIM = SSD_INNER + 2 * SSD_G * SSD_N
SSD_CHUNK = CHUNK

CF_W = BRANCH_W
CF_CONV = 31

SB_H = 8
SB_DH = BRANCH_W // SB_H
SB_W = SB_H * SB_DH
SB_QBLOCK = 128

IN_SPLITS = (LRU_W, LRU_W, SSD_INNER, SSD_CONV_DIM, SSD_H, 2 * CF_W, CF_W, 3 * SB_W, SB_W, N_BRANCH * D_MODEL)
D_IN = sum(IN_SPLITS)

kernel_name = 'hybrid_streaming_encoder_step'


def split_cols(x, sizes):
    idx = [int(i) for i in np.cumsum(sizes)[:-1]]
    return jnp.split(x, idx, axis=-1)


def rms_norm(x, g):
    xf = x.astype(jnp.float32)
    y = xf * lax.rsqrt(jnp.mean(xf * xf, axis=-1, keepdims=True) + EPS)
    return (y * g.astype(jnp.float32)).astype(x.dtype)


def layer_norm(x, g, b):
    xf = x.astype(jnp.float32)
    xc = xf - jnp.mean(xf, axis=-1, keepdims=True)
    var = jnp.mean(xc * xc, axis=-1, keepdims=True)
    return (xc * lax.rsqrt(var + EPS) * g.astype(jnp.float32) + b.astype(jnp.float32)).astype(x.dtype)


def causal_dwconv(x, buf, w, b):
    k = w.shape[0]
    xf = jnp.concatenate([buf.astype(x.dtype), x], axis=1)
    y = lax.conv_general_dilated(xf, w[:, None, :].astype(x.dtype), window_strides=(1,), padding='VALID',
                                 dimension_numbers=('NWC', 'WIO', 'NWC'), feature_group_count=x.shape[-1])
    return y + b.astype(x.dtype), xf[:, xf.shape[1] - (k - 1):]


def rg_lru(xc, h0, wa, ba, wx, bx, lam):
    bsz, t, _ = xc.shape
    xb = xc.reshape(bsz, t, LRU_BLOCKS, LRU_BW)
    gate_a = jnp.einsum('btnc,ncd->btnd', xb, wa).reshape(bsz, t, LRU_W) + ba
    gate_x = jnp.einsum('btnc,ncd->btnd', xb, wx).reshape(bsz, t, LRU_W) + bx
    r = jax.nn.sigmoid(gate_a.astype(jnp.float32))
    i = jax.nn.sigmoid(gate_x.astype(jnp.float32))
    log_a = -LRU_C * r * jax.nn.softplus(-lam.astype(jnp.float32))
    a = jnp.exp(log_a)
    b = jnp.sqrt(-jnp.expm1(2.0 * log_a)) * (i * xc.astype(jnp.float32))
    b = b.at[:, 0].add(a[:, 0] * h0.astype(jnp.float32))

    def combine(lhs, rhs):
        return lhs[0] * rhs[0], rhs[0] * lhs[1] + rhs[1]

    _, h = lax.associative_scan(combine, (a, b), axis=1)
    return h.astype(xc.dtype), h[:, -1].astype(xc.dtype)


def ssd_scan(x, dt, a, bm, cm, h0):
    bsz, t = x.shape[:2]
    L = SSD_CHUNK if t % SSD_CHUNK == 0 else t
    nc = t // L
    f32 = jnp.float32
    xdt = (x.astype(f32) * dt[..., None]).reshape(bsz, nc, L, SSD_G, SSD_R, SSD_P)
    adt = (dt * a).reshape(bsz, nc, L, SSD_G, SSD_R)
    bm = bm.astype(f32).reshape(bsz, nc, L, SSD_G, SSD_N)
    cm = cm.astype(f32).reshape(bsz, nc, L, SSD_G, SSD_N)
    acs = jnp.cumsum(adt, axis=2)
    seg = acs[:, :, :, None] - acs[:, :, None, :]
    causal = jnp.tril(jnp.ones((L, L), dtype=bool))[:, :, None, None]
    decay = jnp.exp(jnp.where(causal, seg, -jnp.inf))
    cb = jnp.einsum('bclgn,bcsgn->bclsg', cm, bm)
    y_diag = jnp.einsum('bclsgr,bcsgrp->bclgrp', cb[..., None] * decay, xdt)
    decay_states = jnp.exp(acs[:, :, -1:] - acs)
    states = jnp.einsum('bclgn,bclgr,bclgrp->bcgrpn', bm, decay_states, xdt)
    chunk_decay = jnp.exp(acs[:, :, -1])

    def step(h, inp):
        dec, st = inp
        return dec[..., None, None] * h + st, h

    h0g = h0.astype(f32).reshape(bsz, SSD_G, SSD_R, SSD_P, SSD_N)
    h_last, h_in = lax.scan(step, h0g, (jnp.moveaxis(chunk_decay, 1, 0), jnp.moveaxis(states, 1, 0)))
    h_in = jnp.moveaxis(h_in, 0, 1)
    y_off = jnp.einsum('bclgn,bcgrpn,bclgr->bclgrp', cm, h_in, jnp.exp(acs))
    y = (y_diag + y_off).reshape(bsz, t, SSD_H, SSD_P)
    return y, h_last.reshape(bsz, SSD_H, SSD_P, SSD_N)


def stick_breaking(q, k, v, q_pos):
    z = jnp.einsum('bqhd,bkhd->bhqk', q, k).astype(jnp.float32) * (SB_DH ** -0.5)
    k_pos = jnp.arange(k.shape[1])
    mask = k_pos[None, :] < q_pos[:, None]
    log_1m = jnp.where(mask, jax.nn.log_sigmoid(-z), 0.0)
    after = lax.cumsum(log_1m, axis=3, reverse=True) - log_1m
    w = jnp.where(mask, jnp.exp(jax.nn.log_sigmoid(z) + after), 0.0)
    return jnp.einsum('bhqk,bkhd->bqhd', w.astype(v.dtype), v)


def stick_breaking_prompt(q, k, v):
    bsz, t = q.shape[:2]
    nb = t // SB_QBLOCK
    kb = k.reshape(bsz, nb, SB_QBLOCK, SB_H, SB_DH).swapaxes(0, 1)
    vb = v.reshape(bsz, nb, SB_QBLOCK, SB_H, SB_DH).swapaxes(0, 1)
    idx = jnp.arange(SB_QBLOCK)
    later = (idx[:, None] > idx[None, :]).astype(jnp.float32)
    diag_mask = idx[None, :] < idx[:, None]
    scale = SB_DH ** -0.5
    outs = []
    for qi in range(nb):
        qblk = q[:, qi * SB_QBLOCK:(qi + 1) * SB_QBLOCK]

        def step(carry, inp, qblk=qblk):
            acc, out = carry
            kk, vv, is_diag = inp
            z = jnp.einsum('bqhd,bkhd->bhqk', qblk, kk).astype(jnp.float32) * scale
            mask = jnp.logical_or(jnp.logical_not(is_diag), diag_mask)
            log_1m = jnp.where(mask, jax.nn.log_sigmoid(-z), 0.0)
            after = jnp.einsum('bhqj,js->bhqs', log_1m, later) + acc[..., None]
            w = jnp.where(mask, jnp.exp(jax.nn.log_sigmoid(z) + after), 0.0)
            out = out + jnp.einsum('bhqk,bkhd->bqhd', w.astype(vv.dtype), vv).astype(jnp.float32)
            return (acc + jnp.sum(log_1m, axis=-1), out), None

        init = (jnp.zeros((bsz, SB_H, SB_QBLOCK), jnp.float32),
                jnp.zeros((bsz, SB_QBLOCK, SB_H, SB_DH), jnp.float32))
        is_diag = jnp.arange(qi + 1) == qi
        (_, o), _ = lax.scan(step, init, (kb[:qi + 1], vb[:qi + 1], is_diag), reverse=True)
        outs.append(o)
    return jnp.concatenate(outs, axis=1).astype(v.dtype)


def hybrid_layer(x, p, st, past_k, past_v):
    bsz, t, _ = x.shape
    h = rms_norm(x, p['norm_pre'])
    proj = h @ p['w_in']
    lru_x, lru_g, ssd_z, ssd_xbc, ssd_dt, cf_in, cf_g, sb_qkv, sb_g, merge = split_cols(proj, IN_SPLITS)

    lru_xc, lru_conv_new = causal_dwconv(lru_x, st['lru_conv'], p['lru_conv_w'], p['lru_conv_b'])
    lru_h, lru_h_new = rg_lru(lru_xc, st['lru_h'], p['lru_wa'], p['lru_ba'], p['lru_wx'], p['lru_bx'], p['lru_lambda'])
    u_a = lru_h * jax.nn.silu(lru_g)

    xbc, ssd_conv_new = causal_dwconv(ssd_xbc, st['ssd_conv'], p['ssd_conv_w'], p['ssd_conv_b'])
    xbc = jax.nn.silu(xbc)
    sx, sbm, scm = split_cols(xbc, (SSD_INNER, SSD_G * SSD_N, SSD_G * SSD_N))
    dt = jax.nn.softplus(ssd_dt.astype(jnp.float32) + p['ssd_dt_bias'].astype(jnp.float32))
    a = -jnp.exp(p['ssd_a_log'].astype(jnp.float32))
    xh = sx.reshape(bsz, t, SSD_H, SSD_P)
    y, ssd_new = ssd_scan(xh, dt, a, sbm.reshape(bsz, t, SSD_G, SSD_N), scm.reshape(bsz, t, SSD_G, SSD_N), st['ssd'])
    y = y + p['ssd_d'].astype(jnp.float32)[:, None] * xh.astype(jnp.float32)
    u_b = rms_norm(y.astype(x.dtype).reshape(bsz, t, SSD_INNER) * jax.nn.silu(ssd_z), p['ssd_norm'])

    glu = cf_in[..., :CF_W] * jax.nn.sigmoid(cf_in[..., CF_W:])
    cfc, cf_conv_new = causal_dwconv(glu, st['cf_conv'], p['cf_conv_w'], p['cf_conv_b'])
    u_c = jax.nn.silu(layer_norm(cfc, p['cf_ln_g'], p['cf_ln_b'])) * jax.nn.silu(cf_g)

    q, k, v = [m.reshape(bsz, t, SB_H, SB_DH) for m in split_cols(sb_qkv, (SB_W, SB_W, SB_W))]
    if past_k is None:
        o = stick_breaking_prompt(q, k, v)
    else:
        k_all = jnp.concatenate([past_k.astype(k.dtype), k], axis=1)
        v_all = jnp.concatenate([past_v.astype(v.dtype), v], axis=1)
        q_pos = past_k.shape[1] + jnp.arange(t)
        o = stick_breaking(q, k_all, v_all, q_pos)
    u_d = o.reshape(bsz, t, SB_W) * jax.nn.silu(sb_g)

    u = jnp.stack([u_a, u_b, u_c, u_d], axis=2)
    y_br = jnp.einsum('btnc,ncd->btnd', u, p['w_down'])
    gates = jax.nn.sigmoid(merge.reshape(bsz, t, N_BRANCH, D_MODEL))
    mixed = jnp.sum(gates * y_br, axis=2)
    out = rms_norm(mixed @ p['w_out'], p['norm_post'])
    new = dict(lru_h=lru_h_new, lru_conv=lru_conv_new, ssd=ssd_new.astype(x.dtype), ssd_conv=ssd_conv_new,
               cf_conv=cf_conv_new, k=k, v=v)
    return x + out, new


def setup_inputs(seed: int = 0) -> dict:
    key = jax.random.key(seed)
    ks = jax.random.split(key, 32)
    f32 = jnp.float32

    def nrm(i, shape, scale):
        return scale * jax.random.normal(ks[i], shape, f32)

    u = jax.random.uniform(ks[0], (DEPTH, LRU_W), f32, minval=0.9, maxval=0.999)
    s = u ** (1.0 / LRU_C)
    lru_lambda = jnp.log(s) - jnp.log1p(-s)
    dt0 = jnp.exp(jax.random.uniform(ks[1], (DEPTH, SSD_H), f32, minval=math.log(0.001), maxval=math.log(0.1)))
    ssd_dt_bias = dt0 + jnp.log(-jnp.expm1(-dt0))
    ssd_a_log = jnp.log(jax.random.uniform(ks[2], (DEPTH, SSD_H), f32, minval=1.0, maxval=16.0))
    return {
        'x_prompt': nrm(3, (BATCH, SEQ, D_MODEL), 1.0),
        'x_sample': nrm(4, (DEC_BATCH, DEC_SEQ, D_MODEL), 1.0),
        'state_lru_h': nrm(5, (DEPTH, DEC_BATCH, LRU_W), 0.5),
        'state_lru_conv': nrm(6, (DEPTH, DEC_BATCH, LRU_CONV - 1, LRU_W), 1.0),
        'state_ssd': nrm(7, (DEPTH, DEC_BATCH, SSD_H, SSD_P, SSD_N), 0.1),
        'state_ssd_conv': nrm(8, (DEPTH, DEC_BATCH, SSD_CONV - 1, SSD_CONV_DIM), 1.0),
        'state_cf_conv': nrm(9, (DEPTH, DEC_BATCH, CF_CONV - 1, CF_W), 1.0),
        'cache_sb_k': nrm(10, (DEPTH, DEC_BATCH, PAST_LEN, SB_H, SB_DH), 1.0),
        'cache_sb_v': nrm(11, (DEPTH, DEC_BATCH, PAST_LEN, SB_H, SB_DH), 1.0),
        'norm_pre': 1.0 + nrm(12, (DEPTH, D_MODEL), 0.05),
        'norm_post': 1.0 + nrm(13, (DEPTH, D_MODEL), 0.05),
        'w_in': nrm(14, (DEPTH, D_MODEL, D_IN), D_MODEL ** -0.5),
        'lru_conv_w': nrm(15, (DEPTH, LRU_CONV, LRU_W), LRU_CONV ** -0.5),
        'lru_conv_b': nrm(16, (DEPTH, LRU_W), 0.01),
        'lru_wa': nrm(17, (DEPTH, LRU_BLOCKS, LRU_BW, LRU_BW), LRU_BW ** -0.5),
        'lru_ba': nrm(18, (DEPTH, LRU_W), 0.01),
        'lru_wx': nrm(19, (DEPTH, LRU_BLOCKS, LRU_BW, LRU_BW), LRU_BW ** -0.5),
        'lru_bx': nrm(20, (DEPTH, LRU_W), 0.01),
        'lru_lambda': lru_lambda,
        'ssd_conv_w': nrm(21, (DEPTH, SSD_CONV, SSD_CONV_DIM), SSD_CONV ** -0.5),
        'ssd_conv_b': nrm(22, (DEPTH, SSD_CONV_DIM), 0.01),
        'ssd_dt_bias': ssd_dt_bias,
        'ssd_a_log': ssd_a_log,
        'ssd_d': 1.0 + nrm(23, (DEPTH, SSD_H), 0.1),
        'ssd_norm': 1.0 + nrm(24, (DEPTH, SSD_INNER), 0.05),
        'cf_conv_w': nrm(25, (DEPTH, CF_CONV, CF_W), CF_CONV ** -0.5),
        'cf_conv_b': nrm(26, (DEPTH, CF_W), 0.01),
        'cf_ln_g': 1.0 + nrm(27, (DEPTH, CF_W), 0.05),
        'cf_ln_b': nrm(28, (DEPTH, CF_W), 0.01),
        'w_down': nrm(29, (DEPTH, N_BRANCH, BRANCH_W, D_MODEL), BRANCH_W ** -0.5),
        'w_out': nrm(30, (DEPTH, D_MODEL, D_MODEL), D_MODEL ** -0.5),
    }


def reference(x_prompt, x_sample, state_lru_h, state_lru_conv, state_ssd, state_ssd_conv, state_cf_conv,
              cache_sb_k, cache_sb_v, norm_pre, norm_post, w_in, lru_conv_w, lru_conv_b, lru_wa, lru_ba,
              lru_wx, lru_bx, lru_lambda, ssd_conv_w, ssd_conv_b, ssd_dt_bias, ssd_a_log, ssd_d, ssd_norm,
              cf_conv_w, cf_conv_b, cf_ln_g, cf_ln_b, w_down, w_out):
    def params(l):
        return dict(norm_pre=norm_pre[l], norm_post=norm_post[l], w_in=w_in[l], lru_conv_w=lru_conv_w[l],
                    lru_conv_b=lru_conv_b[l], lru_wa=lru_wa[l], lru_ba=lru_ba[l], lru_wx=lru_wx[l],
                    lru_bx=lru_bx[l], lru_lambda=lru_lambda[l], ssd_conv_w=ssd_conv_w[l],
                    ssd_conv_b=ssd_conv_b[l], ssd_dt_bias=ssd_dt_bias[l], ssd_a_log=ssd_a_log[l],
                    ssd_d=ssd_d[l], ssd_norm=ssd_norm[l], cf_conv_w=cf_conv_w[l], cf_conv_b=cf_conv_b[l],
                    cf_ln_g=cf_ln_g[l], cf_ln_b=cf_ln_b[l], w_down=w_down[l], w_out=w_out[l])

    bp = x_prompt.shape[0]
    dtp = x_prompt.dtype
    zero_state = dict(lru_h=jnp.zeros((bp, LRU_W), dtp),
                      lru_conv=jnp.zeros((bp, LRU_CONV - 1, LRU_W), dtp),
                      ssd=jnp.zeros((bp, SSD_H, SSD_P, SSD_N), dtp),
                      ssd_conv=jnp.zeros((bp, SSD_CONV - 1, SSD_CONV_DIM), dtp),
                      cf_conv=jnp.zeros((bp, CF_CONV - 1, CF_W), dtp))
    y_p = x_prompt
    p_new = []
    for l in range(DEPTH):
        y_p, s = hybrid_layer(y_p, params(l), zero_state, None, None)
        p_new.append(s)

    y_s = x_sample
    s_new = []
    for l in range(DEPTH):
        st = dict(lru_h=state_lru_h[l], lru_conv=state_lru_conv[l], ssd=state_ssd[l],
                  ssd_conv=state_ssd_conv[l], cf_conv=state_cf_conv[l])
        y_s, s = hybrid_layer(y_s, params(l), st, cache_sb_k[l], cache_sb_v[l])
        s_new.append(s)

    def stack(lst, name):
        return jnp.stack([d[name] for d in lst])

    return (y_p, y_s,
            stack(p_new, 'lru_h'), stack(s_new, 'lru_h'),
            stack(p_new, 'lru_conv'), stack(s_new, 'lru_conv'),
            stack(p_new, 'ssd'), stack(s_new, 'ssd'),
            stack(p_new, 'ssd_conv'), stack(s_new, 'ssd_conv'),
            stack(p_new, 'cf_conv'), stack(s_new, 'cf_conv'),
            stack(p_new, 'k'), stack(s_new, 'k'),
            stack(p_new, 'v'), stack(s_new, 'v'))
```

```python
import functools

import jax
import jax.numpy as jnp
from jax import lax
from jax.experimental import pallas as pl
from jax.experimental.pallas import tpu as pltpu

F32 = jnp.float32
BF16 = jnp.bfloat16

EPS = 1e-6
LRU_C = 8.0
N_BRANCH = 4
LRU_BLOCKS = 8
SSD_P = 64
SSD_N = 64
SSD_G = 4
SSD_R = 2
SB_H = 8
SB_DH = 64

LANES = 128
SUBLANES = 8
VMEM_LIMIT_BYTES = 56 * 1024 * 1024

F32_EXP_ZERO_BELOW = -104.0
NEG_BIG = -1e30

COL_XBC = 0
COL_CF_IN = 1024
COL_LRU_X = 2048
COL_LRU_G = 2560
COL_SSD_Z = 3072
COL_CF_G = 3584
COL_Q = 4096
COL_SB_G = 4608
COL_MERGE = 5120
COL_DT = 9216
P_COLS = 9600
P_TILE_N = 1920


def _cparams(*sem):
    return pltpu.CompilerParams(dimension_semantics=sem, vmem_limit_bytes=VMEM_LIMIT_BYTES)


def _sigmoid(x):
    return 1.0 / (1.0 + jnp.exp(-x))


def _silu(x):
    return x * _sigmoid(x)


def _softplus(x):
    return jnp.maximum(x, 0.0) + jnp.log(1.0 + jnp.exp(-jnp.abs(x)))


def _split3(x):
    hi = x.astype(BF16)
    r = x - hi.astype(F32)
    mid = r.astype(BF16)
    lo = (r - mid.astype(F32)).astype(BF16)
    return hi, mid, lo


def _dot(a, b):
    return jnp.dot(a, b, preferred_element_type=F32)


def _dot_nt(a, b):
    return lax.dot_general(a, b, (((1,), (1,)), ((), ())), preferred_element_type=F32)


def _sel_left(m01, x):
    hi, mid, lo = _split3(x)
    return _dot(m01, hi) + _dot(m01, mid) + _dot(m01, lo)


def _sel_right(x, m01):
    hi, mid, lo = _split3(x)
    return _dot(hi, m01) + _dot(mid, m01) + _dot(lo, m01)


def _sel_nt(m01, x):
    hi, mid, lo = _split3(x)
    return _dot_nt(m01, hi) + _dot_nt(m01, mid) + _dot_nt(m01, lo)


def _iota(shape, dim):
    return lax.broadcasted_iota(jnp.int32, shape, dim)


def _div_pow2(x, n):
    shift = n.bit_length() - 1
    assert 1 << shift == n
    return lax.shift_right_logical(x, shift)


def _inproj_kernel(x_ref, g_ref, w_ref, o_ref, h_ref):
    @pl.when(pl.program_id(1) == 0)
    def _():
        x = x_ref[...]
        ms = jnp.mean(x * x, axis=-1, keepdims=True)
        h_ref[...] = (x * lax.rsqrt(ms + EPS) * g_ref[...]).astype(BF16)

    o_ref[...] = _dot(h_ref[...], w_ref[...]).astype(o_ref.dtype)


def _inproj(x2, g, w, out_dtype, tn):
    n, d = x2.shape
    cols = w.shape[1]
    tm = min(n, 1024)
    return pl.pallas_call(
        _inproj_kernel,
        grid=(n // tm, cols // tn),
        in_specs=[
            pl.BlockSpec((tm, d), lambda i, j: (i, 0)),
            pl.BlockSpec((1, d), lambda i, j: (0, 0)),
            pl.BlockSpec((d, tn), lambda i, j: (0, j)),
        ],
        out_specs=pl.BlockSpec((tm, tn), lambda i, j: (i, j)),
        out_shape=jax.ShapeDtypeStruct((n, cols), out_dtype),
        scratch_shapes=[pltpu.VMEM((tm, d), BF16)],
        compiler_params=_cparams("parallel", "arbitrary"),
        name="inproj_" + jnp.dtype(out_dtype).name,
    )(x2, g, w)


def _kv_kernel(x_ref, g_ref, wk_ref, wv_ref, k_ref, v_ref):
    x = x_ref[...]
    ms = jnp.mean(x * x, axis=-1, keepdims=True)
    h = (x * lax.rsqrt(ms + EPS) * g_ref[...]).astype(BF16)
    k_ref[...] = _dot(h, wk_ref[...])
    v_ref[...] = _dot(h, wv_ref[...])


def _kvproj(x2, g, wk, wv):
    n, d = x2.shape
    w = wk.shape[1]
    tm = min(n, 1024)
    return pl.pallas_call(
        _kv_kernel,
        grid=(n // tm,),
        in_specs=[
            pl.BlockSpec((tm, d), lambda i: (i, 0)),
            pl.BlockSpec((1, d), lambda i: (0, 0)),
            pl.BlockSpec((d, w), lambda i: (0, 0)),
            pl.BlockSpec((d, w), lambda i: (0, 0)),
        ],
        out_specs=[pl.BlockSpec((tm, w), lambda i: (i, 0))] * 2,
        out_shape=[jax.ShapeDtypeStruct((n, w), F32)] * 2,
        compiler_params=_cparams("parallel"),
        name="kvproj",
    )(x2, g, wk, wv)


def _conv_taps(xe_ref, w_ref, b_ref, base, rows, hist_off, k):
    acc = None
    for j in range(k):
        term = w_ref[j:j + 1, :] * xe_ref[pl.ds(base + hist_off + j, rows), :]
        acc = term if acc is None else acc + term
    return acc + b_ref[...]


LRU_HIST = 8


def _lru_kernel(x_ref, g_ref, cs_ref, h0_ref, cw_ref, cb_ref, wa_ref, ba_ref, wx_ref, bx_ref, lam_ref,
                u_ref, hl_ref, cn_ref, xe_ref, a_ref, b_ref, h_ref):
    tt, w = x_ref.shape
    kc = cw_ref.shape[0]
    hist = LRU_HIST - (kc - 1)

    @pl.when(pl.program_id(1) == 0)
    def _():
        xe_ref[hist:LRU_HIST, :] = cs_ref[...]
        h_ref[...] = h0_ref[...]

    xe_ref[LRU_HIST:LRU_HIST + tt, :] = x_ref[...].astype(F32)
    xc = _conv_taps(xe_ref, cw_ref, cb_ref, 0, tt, hist, kc)
    tail = xe_ref[tt + hist:tt + LRU_HIST, :]
    xe_ref[hist:LRU_HIST, :] = tail
    cn_ref[...] = tail

    xcb = xc.astype(BF16)
    gate_a = _dot(xcb, wa_ref[...]) + ba_ref[...]
    gate_x = _dot(xcb, wx_ref[...]) + bx_ref[...]
    r = _sigmoid(gate_a)
    i = _sigmoid(gate_x)
    log_a = (-LRU_C) * r * _softplus(-lam_ref[...])
    a = jnp.exp(log_a)
    one_m_a2 = (1.0 - a) * (1.0 + a)
    a_ref[...] = a
    b_ref[...] = jnp.sqrt(one_m_a2) * (i * xc)

    row = _iota((SUBLANES, w), 0)

    def body(s, h):
        off = pl.multiple_of(s * SUBLANES, SUBLANES)
        a8 = a_ref[pl.ds(off, SUBLANES), :]
        b8 = b_ref[pl.ds(off, SUBLANES), :]
        for d in (1, 2, 4):
            keep = row >= d
            a_sh = pltpu.roll(a8, d, axis=0)
            b_sh = pltpu.roll(b8, d, axis=0)
            b8 = jnp.where(keep, a8 * b_sh + b8, b8)
            a8 = jnp.where(keep, a8 * a_sh, a8)
        h8 = a8 * h + b8
        b_ref[pl.ds(off, SUBLANES), :] = h8
        return h8[SUBLANES - 1:SUBLANES, :]

    h_last = lax.fori_loop(0, tt // SUBLANES, body, h_ref[...])
    h_ref[...] = h_last
    hl_ref[...] = h_last
    u_ref[...] = (b_ref[...] * _silu(g_ref[...].astype(F32))).astype(u_ref.dtype)


def _lru_branch(p3, conv_state, h0, pw, tt):
    b, t, _ = p3.shape
    w = conv_state.shape[-1]
    kc = pw["lru_conv_w"].shape[0]
    col = lambda off: off // w
    full = lambda shape: pl.BlockSpec(shape, lambda bi, ti: (0,) * len(shape))
    return pl.pallas_call(
        _lru_kernel,
        grid=(b, t // tt),
        in_specs=[
            pl.BlockSpec((None, tt, w), lambda bi, ti: (bi, ti, col(COL_LRU_X))),
            pl.BlockSpec((None, tt, w), lambda bi, ti: (bi, ti, col(COL_LRU_G))),
            pl.BlockSpec((None, kc - 1, w), lambda bi, ti: (bi, 0, 0)),
            pl.BlockSpec((None, 1, w), lambda bi, ti: (bi, 0, 0)),
            full((kc, w)), full((1, w)), full((w, w)), full((1, w)), full((w, w)), full((1, w)), full((1, w)),
        ],
        out_specs=[
            pl.BlockSpec((None, tt, w), lambda bi, ti: (bi, ti, 0)),
            pl.BlockSpec((None, 1, w), lambda bi, ti: (bi, 0, 0)),
            pl.BlockSpec((None, kc - 1, w), lambda bi, ti: (bi, 0, 0)),
        ],
        out_shape=[
            jax.ShapeDtypeStruct((b, t, w), BF16),
            jax.ShapeDtypeStruct((b, 1, w), F32),
            jax.ShapeDtypeStruct((b, kc - 1, w), F32),
        ],
        scratch_shapes=[
            pltpu.VMEM((tt + LRU_HIST, w), F32),
            pltpu.VMEM((tt, w), F32),
            pltpu.VMEM((tt, w), F32),
            pltpu.VMEM((1, w), F32),
        ],
        compiler_params=_cparams("parallel", "arbitrary"),
        name="lru_branch",
    )(p3, p3, conv_state, h0, pw["lru_conv_w"], pw["lru_conv_b"], pw["lru_wa"], pw["lru_ba"],
      pw["lru_wx"], pw["lru_bx"], pw["lru_lambda"])


def _ssd_kernel(z_ref, xbc_ref, dt_ref, cs_ref, s0_ref, cw_ref, cb_ref, dtb_ref, alog_ref, dvec_ref, nw_ref,
                u_ref, sn_ref, cn_ref, xe_ref, st_ref):
    L = z_ref.shape[0]
    inner = z_ref.shape[1]
    kc = cw_ref.shape[0]
    hist = LRU_HIST - (kc - 1)
    gw = SSD_R * SSD_P

    @pl.when(pl.program_id(1) == 0)
    def _():
        xe_ref[hist:LRU_HIST, :] = cs_ref[...]
        st_ref[...] = s0_ref[...]

    xe_ref[LRU_HIST:LRU_HIST + L, :] = xbc_ref[...].astype(F32)
    xa = _silu(_conv_taps(xe_ref, cw_ref, cb_ref, 0, L, hist, kc))
    tail = xe_ref[L + hist:L + LRU_HIST, :]
    xe_ref[hist:LRU_HIST, :] = tail
    cn_ref[...] = tail

    x = xa[:, :inner]
    bm = xa[:, inner:inner + SSD_G * SSD_N]
    cm = xa[:, inner + SSD_G * SSD_N:]

    dt = _softplus(dt_ref[...].astype(F32) + dtb_ref[...])
    adt = dt * (-jnp.exp(alog_ref[...]))

    r_i = _iota((L, L), 0)
    c_i = _iota((L, L), 1)
    causal = r_i >= c_i
    tril = causal.astype(BF16)
    eye = (_iota((LANES, LANES), 0) == _iota((LANES, LANES), 1)).astype(BF16)
    eye_n = eye[:SSD_N, :SSD_N]
    expand_c = (_div_pow2(_iota((LANES, inner), 1), SSD_P) == _iota((LANES, inner), 0)).astype(BF16)
    n_heads = inner // SSD_P
    expand_l = (_div_pow2(_iota((LANES, n_heads * L), 1), L) == _iota((LANES, n_heads * L), 0)).astype(BF16)

    acs = _sel_left(tril, adt)
    acs_t = _sel_nt(eye, acs)
    dt_c = _sel_right(dt, expand_c)
    acs_c = _sel_right(acs, expand_c)
    acs_w = _sel_right(acs, expand_l)

    xdt = x * dt_c
    last = acs_c[L - 1:L, :]
    xs = xdt * jnp.exp(last - acs_c)
    e_acs = jnp.exp(acs_c)
    chunk_decay = jnp.exp(last)
    lane = _iota((L, gw), 1)

    ys = []
    for g in range(SSD_G):
        cg = cm[:, g * SSD_N:(g + 1) * SSD_N].astype(BF16)
        bg = bm[:, g * SSD_N:(g + 1) * SSD_N].astype(BF16)
        cb = _dot_nt(cg, bg)
        ms = []
        for r in range(SSD_R):
            h = g * SSD_R + r
            seg = acs_w[:, h * L:(h + 1) * L] - acs_t[h:h + 1, :]
            dec = jnp.exp(jnp.where(causal, seg, NEG_BIG))
            ms.append((cb * dec).astype(BF16))
        mcat = jnp.concatenate(ms, axis=1)
        xg = xdt[:, g * gw:(g + 1) * gw]
        rhs = jnp.concatenate([jnp.where(lane < SSD_P, xg, 0.0), jnp.where(lane >= SSD_P, xg, 0.0)],
                              axis=0).astype(BF16)
        y_diag = _dot(mcat, rhs)
        st = st_ref[g]
        y_off = _dot(cg, st.astype(BF16)) * e_acs[:, g * gw:(g + 1) * gw]
        bg_t = _dot_nt(eye_n, bg).astype(BF16)
        st_ref[g] = st * chunk_decay[:, g * gw:(g + 1) * gw] + _dot(bg_t, xs[:, g * gw:(g + 1) * gw].astype(BF16))
        ys.append(y_diag + y_off)
    y = jnp.concatenate(ys, axis=1) + dvec_ref[...] * x
    sn_ref[...] = st_ref[...]

    gated = y * _silu(z_ref[...].astype(F32))
    ms2 = jnp.mean(gated * gated, axis=-1, keepdims=True)
    u_ref[...] = (gated * lax.rsqrt(ms2 + EPS) * nw_ref[...]).astype(u_ref.dtype)


def _ssd_branch(p3, conv_state, st0, pw, L):
    b, t, _ = p3.shape
    cdim = conv_state.shape[-1]
    inner = pw["ssd_norm"].shape[-1]
    kc = pw["ssd_conv_w"].shape[0]
    gw = SSD_R * SSD_P
    full = lambda shape: pl.BlockSpec(shape, lambda bi, ti: (0,) * len(shape))
    return pl.pallas_call(
        _ssd_kernel,
        grid=(b, t // L),
        in_specs=[
            pl.BlockSpec((None, L, inner), lambda bi, ti: (bi, ti, COL_SSD_Z // inner)),
            pl.BlockSpec((None, L, cdim), lambda bi, ti: (bi, ti, COL_XBC // cdim)),
            pl.BlockSpec((None, L, LANES), lambda bi, ti: (bi, ti, COL_DT // LANES)),
            pl.BlockSpec((None, kc - 1, cdim), lambda bi, ti: (bi, 0, 0)),
            pl.BlockSpec((None, SSD_G, SSD_N, gw), lambda bi, ti: (bi, 0, 0, 0)),
            full((kc, cdim)), full((1, cdim)), full((1, LANES)), full((1, LANES)), full((1, inner)), full((1, inner)),
        ],
        out_specs=[
            pl.BlockSpec((None, L, inner), lambda bi, ti: (bi, ti, 0)),
            pl.BlockSpec((None, SSD_G, SSD_N, gw), lambda bi, ti: (bi, 0, 0, 0)),
            pl.BlockSpec((None, kc - 1, cdim), lambda bi, ti: (bi, 0, 0)),
        ],
        out_shape=[
            jax.ShapeDtypeStruct((b, t, inner), BF16),
            jax.ShapeDtypeStruct((b, SSD_G, SSD_N, gw), F32),
            jax.ShapeDtypeStruct((b, kc - 1, cdim), F32),
        ],
        scratch_shapes=[
            pltpu.VMEM((L + LRU_HIST, cdim), F32),
            pltpu.VMEM((SSD_G, SSD_N, gw), F32),
        ],
        compiler_params=_cparams("parallel", "arbitrary"),
        name="ssd_branch",
    )(p3, p3, p3, conv_state, st0, pw["ssd_conv_w"], pw["ssd_conv_b"], pw["ssd_dt_bias"], pw["ssd_a_log"],
      pw["ssd_d"], pw["ssd_norm"])


CF_HIST = 32
CF_ROWS = 64


def _cf_kernel(in_ref, g_ref, cs_ref, cw_ref, cb_ref, lg_ref, lb_ref, u_ref, cn_ref, xe_ref):
    tt = in_ref.shape[0]
    w = g_ref.shape[1]
    kc = cw_ref.shape[0]
    hist = CF_HIST - (kc - 1)

    @pl.when(pl.program_id(1) == 0)
    def _():
        xe_ref[hist:CF_HIST, :] = cs_ref[...]

    cf = in_ref[...].astype(F32)
    xe_ref[CF_HIST:CF_HIST + tt, :] = cf[:, :w] * _sigmoid(cf[:, w:])

    for s in range(tt // CF_ROWS):
        base = s * CF_ROWS
        c = _conv_taps(xe_ref, cw_ref, cb_ref, base, CF_ROWS, hist, kc)
        mu = jnp.mean(c, axis=-1, keepdims=True)
        cc = c - mu
        var = jnp.mean(cc * cc, axis=-1, keepdims=True)
        ln = cc * lax.rsqrt(var + EPS) * lg_ref[...] + lb_ref[...]
        gate = g_ref[base:base + CF_ROWS, :].astype(F32)
        u_ref[base:base + CF_ROWS, :] = (_silu(ln) * _silu(gate)).astype(u_ref.dtype)
    tail = xe_ref[tt + hist:tt + CF_HIST, :]
    xe_ref[hist:CF_HIST, :] = tail
    cn_ref[...] = tail


def _cf_branch(p3, conv_state, pw, tt):
    b, t, _ = p3.shape
    w = conv_state.shape[-1]
    kc = pw["cf_conv_w"].shape[0]
    full = lambda shape: pl.BlockSpec(shape, lambda bi, ti: (0,) * len(shape))
    return pl.pallas_call(
        _cf_kernel,
        grid=(b, t // tt),
        in_specs=[
            pl.BlockSpec((None, tt, 2 * w), lambda bi, ti: (bi, ti, COL_CF_IN // (2 * w))),
            pl.BlockSpec((None, tt, w), lambda bi, ti: (bi, ti, COL_CF_G // w)),
            pl.BlockSpec((None, kc - 1, w), lambda bi, ti: (bi, 0, 0)),
            full((kc, w)), full((1, w)), full((1, w)), full((1, w)),
        ],
        out_specs=[
            pl.BlockSpec((None, tt, w), lambda bi, ti: (bi, ti, 0)),
            pl.BlockSpec((None, kc - 1, w), lambda bi, ti: (bi, 0, 0)),
        ],
        out_shape=[
            jax.ShapeDtypeStruct((b, t, w), BF16),
            jax.ShapeDtypeStruct((b, kc - 1, w), F32),
        ],
        scratch_shapes=[pltpu.VMEM((tt + CF_HIST, w), F32)],
        compiler_params=_cparams("parallel", "arbitrary"),
        name="cf_branch",
    )(p3, p3, conv_state, pw["cf_conv_w"], pw["cf_conv_b"], pw["cf_ln_g"], pw["cf_ln_b"])


def _sb_kernel(q_ref, k_ref, v_ref, g_ref, u_ref, acc_ref, out_ref, *, diag0):
    tq = q_ref.shape[0]
    tk = tq
    n_h = q_ref.shape[1] // SB_DH
    kd = pl.program_id(2) + diag0

    q = q_ref[...]
    qs = [(q[:, h * SB_DH:(h + 1) * SB_DH].astype(F32) * (SB_DH ** -0.5)).astype(BF16) for h in range(n_h)]
    r_i = _iota((tq, tk), 0)
    c_i = _iota((tq, tk), 1)
    strictly_earlier = c_i < r_i
    later = (r_i > c_i).astype(BF16)

    acc_ref[...] = jnp.zeros_like(acc_ref)
    out_ref[...] = jnp.zeros_like(out_ref)

    def block(kb, masked):
        off = pl.multiple_of(kb * tk, tk)
        kblk = k_ref[pl.ds(off, tk), :].astype(BF16)
        vblk = v_ref[pl.ds(off, tk), :].astype(BF16)
        worst = None
        for h in range(n_h):
            sl = slice(h * SB_DH, (h + 1) * SB_DH)
            z = _dot_nt(qs[h], kblk[:, sl])
            lsm = -_softplus(z)
            if masked:
                lsm = jnp.where(strictly_earlier, lsm, 0.0)
            hi = lsm.astype(BF16)
            lo = (lsm - hi.astype(F32)).astype(BF16)
            inblock = _dot(hi, later) + _dot(lo, later)
            acc = acc_ref[h]
            lw = z + lsm + inblock + acc
            if masked:
                lw = jnp.where(strictly_earlier, lw, NEG_BIG)
            wgt = jnp.exp(lw).astype(BF16)
            out_ref[h] = out_ref[h] + _dot(wgt, vblk[:, sl])
            acc = acc + inblock[:, 0:1] + lsm[:, 0:1]
            acc_ref[h] = acc
            m = jnp.max(acc)
            worst = m if worst is None else jnp.maximum(worst, m)
        return worst

    worst0 = block(kd, True)

    def cond(c):
        kb, worst = c
        return jnp.logical_and(kb >= 0, worst >= F32_EXP_ZERO_BELOW)

    def body(c):
        kb, _ = c
        return kb - 1, block(kb, False)

    lax.while_loop(cond, body, (kd - 1, worst0))

    o = jnp.concatenate([out_ref[h] for h in range(n_h)], axis=1)
    u_ref[...] = (o * _silu(g_ref[...].astype(F32))).astype(u_ref.dtype)


def _sb_branch(p3, k_all, v_all, tq):
    b, t, _ = p3.shape
    tk_total = k_all.shape[1]
    w = k_all.shape[2]
    hw = LANES
    n_pairs = w // hw
    diag0 = (tk_total - t) // tq
    heads = hw // SB_DH
    return pl.pallas_call(
        functools.partial(_sb_kernel, diag0=diag0),
        grid=(b, n_pairs, t // tq),
        in_specs=[
            pl.BlockSpec((None, tq, hw), lambda bi, hi, qi: (bi, qi, COL_Q // hw + hi)),
            pl.BlockSpec((None, tk_total, hw), lambda bi, hi, qi: (bi, 0, hi)),
            pl.BlockSpec((None, tk_total, hw), lambda bi, hi, qi: (bi, 0, hi)),
            pl.BlockSpec((None, tq, hw), lambda bi, hi, qi: (bi, qi, COL_SB_G // hw + hi)),
        ],
        out_specs=pl.BlockSpec((None, tq, hw), lambda bi, hi, qi: (bi, qi, hi)),
        out_shape=jax.ShapeDtypeStruct((b, t, w), BF16),
        scratch_shapes=[
            pltpu.VMEM((heads, tq, 1), F32),
            pltpu.VMEM((heads, tq, SB_DH), F32),
        ],
        compiler_params=_cparams("parallel", "parallel", "arbitrary"),
        name="sb_branch",
    )(p3, k_all, v_all, p3)


def _mix_kernel(ua_ref, ub_ref, uc_ref, ud_ref, m0_ref, m1_ref, m2_ref, m3_ref, x_ref, wd_ref, wo_ref, g_ref, o_ref):
    us = (ua_ref, ub_ref, uc_ref, ud_ref)
    ms = (m0_ref, m1_ref, m2_ref, m3_ref)
    mixed = None
    for i in range(N_BRANCH):
        y = _dot(us[i][...], wd_ref[i])
        term = _sigmoid(ms[i][...].astype(F32)) * y
        mixed = term if mixed is None else mixed + term
    o = _dot(mixed.astype(BF16), wo_ref[...])
    ms2 = jnp.mean(o * o, axis=-1, keepdims=True)
    o_ref[...] = x_ref[...] + o * lax.rsqrt(ms2 + EPS) * g_ref[...]


def _mix(us, p2, x2, pw):
    n, d = x2.shape
    w = us[0].shape[-1]
    tm = min(n, 512)
    ublk = pl.BlockSpec((tm, w), lambda i: (i, 0))
    mblk = lambda j: pl.BlockSpec((tm, d), lambda i: (i, COL_MERGE // d + j))
    return pl.pallas_call(
        _mix_kernel,
        grid=(n // tm,),
        in_specs=[ublk] * 4 + [mblk(j) for j in range(N_BRANCH)] + [
            pl.BlockSpec((tm, d), lambda i: (i, 0)),
            pl.BlockSpec((N_BRANCH, w, d), lambda i: (0, 0, 0)),
            pl.BlockSpec((d, d), lambda i: (0, 0)),
            pl.BlockSpec((1, d), lambda i: (0, 0)),
        ],
        out_specs=pl.BlockSpec((tm, d), lambda i: (i, 0)),
        out_shape=jax.ShapeDtypeStruct((n, d), F32),
        compiler_params=_cparams("parallel"),
        name="mix",
    )(*us, p2, p2, p2, p2, x2, pw["w_down"], pw["w_out"], pw["norm_post"])


def _block_diag(wb):
    n, c, d = wb.shape
    eye = jnp.eye(n, dtype=wb.dtype)
    return (eye[:, None, :, None] * wb[:, :, None, :]).reshape(n * c, n * d)


def _prep_layer(p):
    d_model = p["w_in"].shape[0]
    bw = p["lru_lambda"].shape[-1]
    cdim = p["ssd_conv_b"].shape[-1]
    n_heads = p["ssd_dt_bias"].shape[-1]
    sizes = (bw, bw, bw, cdim, n_heads, 2 * bw, bw, 3 * bw, bw, N_BRANCH * d_model)
    offs = [0]
    for s in sizes:
        offs.append(offs[-1] + s)
    seg = lambda i: p["w_in"][:, offs[i]:offs[i + 1]]
    lru_x, lru_g, ssd_z, xbc, dt, cf_in, cf_g, qkv, sb_g, merge = [seg(i) for i in range(len(sizes))]
    q, k, v = qkv[:, :bw], qkv[:, bw:2 * bw], qkv[:, 2 * bw:]
    zeros = lambda c: jnp.zeros((d_model, c), p["w_in"].dtype)
    main = jnp.concatenate(
        [xbc, cf_in, lru_x, lru_g, ssd_z, cf_g, q, sb_g, merge, dt, zeros(LANES - n_heads),
         zeros(P_COLS - COL_DT - LANES)], axis=1)
    assert main.shape[1] == P_COLS
    pad_h = lambda a, fill: jnp.concatenate([a, jnp.full((LANES - n_heads,), fill, a.dtype)])[None, :]
    row = lambda a: a[None, :].astype(F32)
    return dict(
        w_main=main.astype(BF16), w_k=k.astype(BF16), w_v=v.astype(BF16),
        norm_pre=row(p["norm_pre"]), norm_post=row(p["norm_post"]),
        lru_conv_w=p["lru_conv_w"], lru_conv_b=row(p["lru_conv_b"]),
        lru_wa=_block_diag(p["lru_wa"]).astype(BF16), lru_ba=row(p["lru_ba"]),
        lru_wx=_block_diag(p["lru_wx"]).astype(BF16), lru_bx=row(p["lru_bx"]),
        lru_lambda=row(p["lru_lambda"]),
        ssd_conv_w=p["ssd_conv_w"], ssd_conv_b=row(p["ssd_conv_b"]),
        ssd_dt_bias=pad_h(p["ssd_dt_bias"], 0.0), ssd_a_log=pad_h(p["ssd_a_log"], 0.0),
        ssd_d=jnp.repeat(p["ssd_d"], SSD_P)[None, :], ssd_norm=row(p["ssd_norm"]),
        cf_conv_w=p["cf_conv_w"], cf_conv_b=row(p["cf_conv_b"]),
        cf_ln_g=row(p["cf_ln_g"]), cf_ln_b=row(p["cf_ln_b"]),
        w_down=p["w_down"].astype(BF16), w_out=p["w_out"].astype(BF16),
    )


def _ssd_state_in(s):
    b = s.shape[0]
    s = s.reshape(b, SSD_G, SSD_R, SSD_P, SSD_N)
    return jnp.transpose(s, (0, 1, 4, 2, 3)).reshape(b, SSD_G, SSD_N, SSD_R * SSD_P)


def _ssd_state_out(s):
    b = s.shape[0]
    s = s.reshape(b, SSD_G, SSD_N, SSD_R, SSD_P)
    return jnp.transpose(s, (0, 1, 3, 4, 2)).reshape(b, SSD_G * SSD_R, SSD_P, SSD_N)


def _layer(x, pw, st, past_k, past_v, *, seq_tile, ssd_chunk, sb_block):
    b, t, d = x.shape
    x2 = x.reshape(b * t, d)
    p2 = _inproj(x2, pw["norm_pre"], pw["w_main"], BF16, P_TILE_N)
    k2, v2 = _kvproj(x2, pw["norm_pre"], pw["w_k"], pw["w_v"])
    p3 = p2.reshape(b, t, P_COLS)
    k3 = k2.reshape(b, t, -1)
    v3 = v2.reshape(b, t, -1)

    u_a, lru_h, lru_conv = _lru_branch(p3, st["lru_conv"], st["lru_h"][:, None, :], pw, seq_tile)
    u_b, ssd_new, ssd_conv = _ssd_branch(p3, st["ssd_conv"], _ssd_state_in(st["ssd"]), pw, ssd_chunk)
    u_c, cf_conv = _cf_branch(p3, st["cf_conv"], pw, seq_tile)
    if past_k is None:
        k_all, v_all = k3, v3
    else:
        k_all = jnp.concatenate([past_k.reshape(b, past_k.shape[1], -1), k3], axis=1)
        v_all = jnp.concatenate([past_v.reshape(b, past_v.shape[1], -1), v3], axis=1)
    u_d = _sb_branch(p3, k_all, v_all, sb_block)

    w = u_a.shape[-1]
    us = [u.reshape(b * t, w) for u in (u_a, u_b, u_c, u_d)]
    y = _mix(us, p2, x2, pw).reshape(b, t, d)
    new = dict(lru_h=lru_h[:, 0, :], lru_conv=lru_conv, ssd=_ssd_state_out(ssd_new), ssd_conv=ssd_conv,
               cf_conv=cf_conv, k=k3.reshape(b, t, SB_H, SB_DH), v=v3.reshape(b, t, SB_H, SB_DH))
    return y, new


_STATE_KEYS = ("lru_h", "lru_conv", "ssd", "ssd_conv", "cf_conv", "k", "v")


def kernel(x_prompt, x_sample, state_lru_h, state_lru_conv, state_ssd, state_ssd_conv, state_cf_conv, cache_sb_k, cache_sb_v, norm_pre, norm_post, w_in, lru_conv_w, lru_conv_b, lru_wa, lru_ba, lru_wx, lru_bx, lru_lambda, ssd_conv_w, ssd_conv_b, ssd_dt_bias, ssd_a_log, ssd_d, ssd_norm, cf_conv_w, cf_conv_b, cf_ln_g, cf_ln_b, w_down, w_out):
    raw = dict(norm_pre=norm_pre, norm_post=norm_post, w_in=w_in, lru_conv_w=lru_conv_w, lru_conv_b=lru_conv_b,
               lru_wa=lru_wa, lru_ba=lru_ba, lru_wx=lru_wx, lru_bx=lru_bx, lru_lambda=lru_lambda,
               ssd_conv_w=ssd_conv_w, ssd_conv_b=ssd_conv_b, ssd_dt_bias=ssd_dt_bias, ssd_a_log=ssd_a_log,
               ssd_d=ssd_d, ssd_norm=ssd_norm, cf_conv_w=cf_conv_w, cf_conv_b=cf_conv_b, cf_ln_g=cf_ln_g,
               cf_ln_b=cf_ln_b, w_down=w_down, w_out=w_out)
    pws = jax.vmap(_prep_layer)(raw)

    bp = x_prompt.shape[0]
    dt = x_prompt.dtype
    bw = state_lru_h.shape[-1]
    zero_state = dict(lru_h=jnp.zeros((bp, bw), dt),
                      lru_conv=jnp.zeros((bp,) + state_lru_conv.shape[2:], dt),
                      ssd=jnp.zeros((bp,) + state_ssd.shape[2:], dt),
                      ssd_conv=jnp.zeros((bp,) + state_ssd_conv.shape[2:], dt),
                      cf_conv=jnp.zeros((bp,) + state_cf_conv.shape[2:], dt))

    def prompt_step(x, pw):
        y, new = _layer(x, pw, zero_state, None, None, seq_tile=512, ssd_chunk=128, sb_block=128)
        return y, tuple(new[k] for k in _STATE_KEYS)

    y_p, p_new = lax.scan(prompt_step, x_prompt, pws)

    t_s = x_sample.shape[1]

    def sample_step(x, xs):
        pw, st, pk, pv = xs
        y, new = _layer(x, pw, st, pk, pv, seq_tile=t_s, ssd_chunk=t_s, sb_block=t_s)
        return y, tuple(new[k] for k in _STATE_KEYS)

    st_s = dict(lru_h=state_lru_h, lru_conv=state_lru_conv, ssd=state_ssd, ssd_conv=state_ssd_conv,
                cf_conv=state_cf_conv)
    y_s, s_new = lax.scan(sample_step, x_sample, (pws, st_s, cache_sb_k, cache_sb_v))

    outs = [y_p, y_s]
    for i in range(len(_STATE_KEYS)):
        outs += [p_new[i], s_new[i]]
    return tuple(outs)
```

```python
import functools

import jax
import jax.numpy as jnp
from jax import lax
from jax.experimental import pallas as pl
from jax.experimental.pallas import tpu as pltpu

F32 = jnp.float32
BF16 = jnp.bfloat16

EPS = 1e-6
LRU_C = 8.0
N_BRANCH = 4
LRU_BLOCKS = 8
SSD_P = 64
SSD_N = 64
SSD_G = 4
SSD_R = 2
SB_H = 8
SB_DH = 64

LANES = 128
SUBLANES = 8
VMEM_LIMIT_BYTES = 56 * 1024 * 1024

F32_EXP_ZERO_BELOW = -104.0
NEG_BIG = -1e30

COL_XBC = 0
COL_CF_IN = 1024
COL_LRU_X = 2048
COL_LRU_G = 2560
COL_SSD_Z = 3072
COL_CF_G = 3584
COL_Q = 4096
COL_SB_G = 4608
COL_MERGE = 5120
COL_DT = 9216
P_COLS = 9600
P_TILE_N = 1920


def _cparams(*sem):
    return pltpu.CompilerParams(dimension_semantics=sem, vmem_limit_bytes=VMEM_LIMIT_BYTES)


def _sigmoid(x):
    return 0.5 * jnp.tanh(0.5 * x) + 0.5


def _silu(x):
    return x * _sigmoid(x)


def _softplus(x):
    return jnp.maximum(x, 0.0) + jnp.log(1.0 + jnp.exp(-jnp.abs(x)))


def _split3(x):
    hi = x.astype(BF16)
    r = x - hi.astype(F32)
    mid = r.astype(BF16)
    lo = (r - mid.astype(F32)).astype(BF16)
    return hi, mid, lo


def _dot(a, b):
    return jnp.dot(a, b, preferred_element_type=F32)


def _dot_nt(a, b):
    return lax.dot_general(a, b, (((1,), (1,)), ((), ())), preferred_element_type=F32)


def _sel_left(m01, x):
    hi, mid, lo = _split3(x)
    return _dot(m01, hi) + _dot(m01, mid) + _dot(m01, lo)


def _sel_right(x, m01):
    hi, mid, lo = _split3(x)
    return _dot(hi, m01) + _dot(mid, m01) + _dot(lo, m01)


def _sel_nt(m01, x):
    hi, mid, lo = _split3(x)
    return _dot_nt(m01, hi) + _dot_nt(m01, mid) + _dot_nt(m01, lo)


def _iota(shape, dim):
    return lax.broadcasted_iota(jnp.int32, shape, dim)


def _div_pow2(x, n):
    shift = n.bit_length() - 1
    assert 1 << shift == n
    return lax.shift_right_logical(x, shift)


def _inproj_kernel(x_ref, g_ref, w_ref, o_ref, h_ref):
    @pl.when(pl.program_id(1) == 0)
    def _():
        x = x_ref[...]
        ms = jnp.mean(x * x, axis=-1, keepdims=True)
        h_ref[...] = (x * lax.rsqrt(ms + EPS) * g_ref[...]).astype(BF16)

    o_ref[...] = _dot(h_ref[...], w_ref[...]).astype(o_ref.dtype)


def _inproj(x2, g, w, out_dtype, tn):
    n, d = x2.shape
    cols = w.shape[1]
    tm = min(n, 1024)
    return pl.pallas_call(
        _inproj_kernel,
        grid=(n // tm, cols // tn),
        in_specs=[
            pl.BlockSpec((tm, d), lambda i, j: (i, 0)),
            pl.BlockSpec((1, d), lambda i, j: (0, 0)),
            pl.BlockSpec((d, tn), lambda i, j: (0, j)),
        ],
        out_specs=pl.BlockSpec((tm, tn), lambda i, j: (i, j)),
        out_shape=jax.ShapeDtypeStruct((n, cols), out_dtype),
        scratch_shapes=[pltpu.VMEM((tm, d), BF16)],
        compiler_params=_cparams("parallel", "arbitrary"),
        name="inproj_" + jnp.dtype(out_dtype).name,
    )(x2, g, w)


def _kv_kernel(x_ref, g_ref, wk_ref, wv_ref, k_ref, v_ref, kb_ref, vb_ref):
    x = x_ref[...]
    ms = jnp.mean(x * x, axis=-1, keepdims=True)
    h = (x * lax.rsqrt(ms + EPS) * g_ref[...]).astype(BF16)
    k = _dot(h, wk_ref[...])
    v = _dot(h, wv_ref[...])
    k_ref[...] = k
    v_ref[...] = v
    kb_ref[...] = k.astype(BF16)
    vb_ref[...] = v.astype(BF16)


def _kvproj(x2, g, wk, wv):
    n, d = x2.shape
    w = wk.shape[1]
    tm = min(n, 1024)
    return pl.pallas_call(
        _kv_kernel,
        grid=(n // tm,),
        in_specs=[
            pl.BlockSpec((tm, d), lambda i: (i, 0)),
            pl.BlockSpec((1, d), lambda i: (0, 0)),
            pl.BlockSpec((d, w), lambda i: (0, 0)),
            pl.BlockSpec((d, w), lambda i: (0, 0)),
        ],
        out_specs=[pl.BlockSpec((tm, w), lambda i: (i, 0))] * 4,
        out_shape=[jax.ShapeDtypeStruct((n, w), F32)] * 2 + [jax.ShapeDtypeStruct((n, w), BF16)] * 2,
        compiler_params=_cparams("parallel"),
        name="kvproj",
    )(x2, g, wk, wv)


def _conv_taps(xe_ref, w_ref, b_ref, base, rows, hist_off, k):
    acc = None
    for j in range(k):
        term = w_ref[j:j + 1, :] * xe_ref[pl.ds(base + hist_off + j, rows), :]
        acc = term if acc is None else acc + term
    return acc + b_ref[...]


LRU_HIST = 8


def _lru_kernel(x_ref, g_ref, cs_ref, h0_ref, cw_ref, cb_ref, wa_ref, ba_ref, wx_ref, bx_ref, lam_ref,
                u_ref, hl_ref, cn_ref, xe_ref, a_ref, b_ref, h_ref):
    tt, w = x_ref.shape
    kc = cw_ref.shape[0]
    hist = LRU_HIST - (kc - 1)

    @pl.when(pl.program_id(1) == 0)
    def _():
        xe_ref[hist:LRU_HIST, :] = cs_ref[...]
        h_ref[...] = h0_ref[...]

    xe_ref[LRU_HIST:LRU_HIST + tt, :] = x_ref[...].astype(F32)
    xc = _conv_taps(xe_ref, cw_ref, cb_ref, 0, tt, hist, kc)
    tail = xe_ref[tt + hist:tt + LRU_HIST, :]
    xe_ref[hist:LRU_HIST, :] = tail
    cn_ref[...] = tail

    xcb = xc.astype(BF16)
    gate_a = _dot(xcb, wa_ref[...]) + ba_ref[...]
    gate_x = _dot(xcb, wx_ref[...]) + bx_ref[...]
    r = _sigmoid(gate_a)
    i = _sigmoid(gate_x)
    log_a = (-LRU_C) * r * _softplus(-lam_ref[...])
    a = jnp.exp(log_a)
    one_m_a2 = (1.0 - a) * (1.0 + a)
    a_ref[...] = a
    b_ref[...] = jnp.sqrt(one_m_a2) * (i * xc)

    row = _iota((SUBLANES, w), 0)

    def body(s, h):
        off = pl.multiple_of(s * SUBLANES, SUBLANES)
        a8 = a_ref[pl.ds(off, SUBLANES), :]
        b8 = b_ref[pl.ds(off, SUBLANES), :]
        for d in (1, 2, 4):
            keep = row >= d
            a_sh = pltpu.roll(a8, d, axis=0)
            b_sh = pltpu.roll(b8, d, axis=0)
            b8 = jnp.where(keep, a8 * b_sh + b8, b8)
            a8 = jnp.where(keep, a8 * a_sh, a8)
        h8 = a8 * h + b8
        b_ref[pl.ds(off, SUBLANES), :] = h8
        return h8[SUBLANES - 1:SUBLANES, :]

    h_last = lax.fori_loop(0, tt // SUBLANES, body, h_ref[...])
    h_ref[...] = h_last
    hl_ref[...] = h_last
    u_ref[...] = (b_ref[...] * _silu(g_ref[...].astype(F32))).astype(u_ref.dtype)


def _lru_branch(p3, conv_state, h0, pw, tt):
    b, t, _ = p3.shape
    w = conv_state.shape[-1]
    kc = pw["lru_conv_w"].shape[0]
    col = lambda off: off // w
    full = lambda shape: pl.BlockSpec(shape, lambda bi, ti: (0,) * len(shape))
    return pl.pallas_call(
        _lru_kernel,
        grid=(b, t // tt),
        in_specs=[
            pl.BlockSpec((None, tt, w), lambda bi, ti: (bi, ti, col(COL_LRU_X))),
            pl.BlockSpec((None, tt, w), lambda bi, ti: (bi, ti, col(COL_LRU_G))),
            pl.BlockSpec((None, kc - 1, w), lambda bi, ti: (bi, 0, 0)),
            pl.BlockSpec((None, 1, w), lambda bi, ti: (bi, 0, 0)),
            full((kc, w)), full((1, w)), full((w, w)), full((1, w)), full((w, w)), full((1, w)), full((1, w)),
        ],
        out_specs=[
            pl.BlockSpec((None, tt, w), lambda bi, ti: (bi, ti, 0)),
            pl.BlockSpec((None, 1, w), lambda bi, ti: (bi, 0, 0)),
            pl.BlockSpec((None, kc - 1, w), lambda bi, ti: (bi, 0, 0)),
        ],
        out_shape=[
            jax.ShapeDtypeStruct((b, t, w), BF16),
            jax.ShapeDtypeStruct((b, 1, w), F32),
            jax.ShapeDtypeStruct((b, kc - 1, w), F32),
        ],
        scratch_shapes=[
            pltpu.VMEM((tt + LRU_HIST, w), F32),
            pltpu.VMEM((tt, w), F32),
            pltpu.VMEM((tt, w), F32),
            pltpu.VMEM((1, w), F32),
        ],
        compiler_params=_cparams("parallel", "arbitrary"),
        name="lru_branch",
    )(p3, p3, conv_state, h0, pw["lru_conv_w"], pw["lru_conv_b"], pw["lru_wa"], pw["lru_ba"],
      pw["lru_wx"], pw["lru_bx"], pw["lru_lambda"])


def _ssd_kernel(z_ref, xbc_ref, dt_ref, cs_ref, s0_ref, cw_ref, cb_ref, dtb_ref, alog_ref, dvec_ref, nw_ref,
                u_ref, sn_ref, cn_ref, xe_ref, st_ref):
    L = z_ref.shape[0]
    inner = z_ref.shape[1]
    kc = cw_ref.shape[0]
    hist = LRU_HIST - (kc - 1)
    gw = SSD_R * SSD_P

    @pl.when(pl.program_id(1) == 0)
    def _():
        xe_ref[hist:LRU_HIST, :] = cs_ref[...]
        st_ref[...] = s0_ref[...]

    xe_ref[LRU_HIST:LRU_HIST + L, :] = xbc_ref[...].astype(F32)
    xa = _silu(_conv_taps(xe_ref, cw_ref, cb_ref, 0, L, hist, kc))
    tail = xe_ref[L + hist:L + LRU_HIST, :]
    xe_ref[hist:LRU_HIST, :] = tail
    cn_ref[...] = tail

    x = xa[:, :inner]
    bm = xa[:, inner:inner + SSD_G * SSD_N]
    cm = xa[:, inner + SSD_G * SSD_N:]

    dt = _softplus(dt_ref[...].astype(F32) + dtb_ref[...])
    adt = dt * (-jnp.exp(alog_ref[...]))

    r_i = _iota((L, L), 0)
    c_i = _iota((L, L), 1)
    causal = r_i >= c_i
    tril = causal.astype(BF16)
    eye = (_iota((LANES, LANES), 0) == _iota((LANES, LANES), 1)).astype(BF16)
    eye_n = eye[:SSD_N, :SSD_N]
    expand_c = (_div_pow2(_iota((LANES, inner), 1), SSD_P) == _iota((LANES, inner), 0)).astype(BF16)
    n_heads = inner // SSD_P
    expand_l = (_div_pow2(_iota((LANES, n_heads * L), 1), L) == _iota((LANES, n_heads * L), 0)).astype(BF16)

    acs = _sel_left(tril, adt)
    acs_t = _sel_nt(eye, acs)
    dt_c = _sel_right(dt, expand_c)
    acs_c = _sel_right(acs, expand_c)
    acs_w = _sel_right(acs, expand_l)

    xdt = x * dt_c
    last = acs_c[L - 1:L, :]
    xs = xdt * jnp.exp(last - acs_c)
    e_acs = jnp.exp(acs_c)
    chunk_decay = jnp.exp(last)
    lane = _iota((L, gw), 1)

    ys = []
    for g in range(SSD_G):
        cg = cm[:, g * SSD_N:(g + 1) * SSD_N].astype(BF16)
        bg = bm[:, g * SSD_N:(g + 1) * SSD_N].astype(BF16)
        cb = _dot_nt(cg, bg)
        ms = []
        for r in range(SSD_R):
            h = g * SSD_R + r
            seg = acs_w[:, h * L:(h + 1) * L] - acs_t[h:h + 1, :]
            dec = jnp.exp(jnp.where(causal, seg, NEG_BIG))
            ms.append((cb * dec).astype(BF16))
        mcat = jnp.concatenate(ms, axis=1)
        xg = xdt[:, g * gw:(g + 1) * gw]
        rhs = jnp.concatenate([jnp.where(lane < SSD_P, xg, 0.0), jnp.where(lane >= SSD_P, xg, 0.0)],
                              axis=0).astype(BF16)
        y_diag = _dot(mcat, rhs)
        st = st_ref[g]
        y_off = _dot(cg, st.astype(BF16)) * e_acs[:, g * gw:(g + 1) * gw]
        bg_t = _dot_nt(eye_n, bg).astype(BF16)
        st_ref[g] = st * chunk_decay[:, g * gw:(g + 1) * gw] + _dot(bg_t, xs[:, g * gw:(g + 1) * gw].astype(BF16))
        ys.append(y_diag + y_off)
    y = jnp.concatenate(ys, axis=1) + dvec_ref[...] * x
    sn_ref[...] = st_ref[...]

    gated = y * _silu(z_ref[...].astype(F32))
    ms2 = jnp.mean(gated * gated, axis=-1, keepdims=True)
    u_ref[...] = (gated * lax.rsqrt(ms2 + EPS) * nw_ref[...]).astype(u_ref.dtype)


def _ssd_branch(p3, conv_state, st0, pw, L):
    b, t, _ = p3.shape
    cdim = conv_state.shape[-1]
    inner = pw["ssd_norm"].shape[-1]
    kc = pw["ssd_conv_w"].shape[0]
    gw = SSD_R * SSD_P
    full = lambda shape: pl.BlockSpec(shape, lambda bi, ti: (0,) * len(shape))
    return pl.pallas_call(
        _ssd_kernel,
        grid=(b, t // L),
        in_specs=[
            pl.BlockSpec((None, L, inner), lambda bi, ti: (bi, ti, COL_SSD_Z // inner)),
            pl.BlockSpec((None, L, cdim), lambda bi, ti: (bi, ti, COL_XBC // cdim)),
            pl.BlockSpec((None, L, LANES), lambda bi, ti: (bi, ti, COL_DT // LANES)),
            pl.BlockSpec((None, kc - 1, cdim), lambda bi, ti: (bi, 0, 0)),
            pl.BlockSpec((None, SSD_G, SSD_N, gw), lambda bi, ti: (bi, 0, 0, 0)),
            full((kc, cdim)), full((1, cdim)), full((1, LANES)), full((1, LANES)), full((1, inner)), full((1, inner)),
        ],
        out_specs=[
            pl.BlockSpec((None, L, inner), lambda bi, ti: (bi, ti, 0)),
            pl.BlockSpec((None, SSD_G, SSD_N, gw), lambda bi, ti: (bi, 0, 0, 0)),
            pl.BlockSpec((None, kc - 1, cdim), lambda bi, ti: (bi, 0, 0)),
        ],
        out_shape=[
            jax.ShapeDtypeStruct((b, t, inner), BF16),
            jax.ShapeDtypeStruct((b, SSD_G, SSD_N, gw), F32),
            jax.ShapeDtypeStruct((b, kc - 1, cdim), F32),
        ],
        scratch_shapes=[
            pltpu.VMEM((L + LRU_HIST, cdim), F32),
            pltpu.VMEM((SSD_G, SSD_N, gw), F32),
        ],
        compiler_params=_cparams("parallel", "arbitrary"),
        name="ssd_branch",
    )(p3, p3, p3, conv_state, st0, pw["ssd_conv_w"], pw["ssd_conv_b"], pw["ssd_dt_bias"], pw["ssd_a_log"],
      pw["ssd_d"], pw["ssd_norm"])


CF_HIST = 32
CF_ROWS = 64


def _cf_kernel(in_ref, g_ref, cs_ref, cw_ref, cb_ref, lg_ref, lb_ref, u_ref, cn_ref, xe_ref):
    tt = in_ref.shape[0]
    w = g_ref.shape[1]
    kc = cw_ref.shape[0]
    hist = CF_HIST - (kc - 1)

    @pl.when(pl.program_id(1) == 0)
    def _():
        xe_ref[hist:CF_HIST, :] = cs_ref[...]

    cf = in_ref[...].astype(F32)
    xe_ref[CF_HIST:CF_HIST + tt, :] = cf[:, :w] * _sigmoid(cf[:, w:])

    def body(s, carry):
        base = pl.multiple_of(s * CF_ROWS, CF_ROWS)
        c = None
        for r in range(SUBLANES):
            rows = CF_ROWS + (SUBLANES if r else 0)
            part = None
            for o in range(hist + (r - hist) % SUBLANES, hist + kc, SUBLANES):
                term = cw_ref[o - hist:o - hist + 1, :] * xe_ref[pl.ds(base + (o - r), rows), :]
                part = term if part is None else part + term
            piece = part[r:r + CF_ROWS, :]
            c = piece if c is None else c + piece
        c = c + cb_ref[...]
        mu = jnp.mean(c, axis=-1, keepdims=True)
        cc = c - mu
        var = jnp.mean(cc * cc, axis=-1, keepdims=True)
        ln = cc * lax.rsqrt(var + EPS) * lg_ref[...] + lb_ref[...]
        gate = g_ref[pl.ds(base, CF_ROWS), :].astype(F32)
        u_ref[pl.ds(base, CF_ROWS), :] = (_silu(ln) * _silu(gate)).astype(u_ref.dtype)
        return carry

    lax.fori_loop(0, tt // CF_ROWS, body, 0)
    tail = xe_ref[tt + hist:tt + CF_HIST, :]
    xe_ref[hist:CF_HIST, :] = tail
    cn_ref[...] = tail


def _cf_branch(p3, conv_state, pw, tt):
    b, t, _ = p3.shape
    w = conv_state.shape[-1]
    kc = pw["cf_conv_w"].shape[0]
    full = lambda shape: pl.BlockSpec(shape, lambda bi, ti: (0,) * len(shape))
    return pl.pallas_call(
        _cf_kernel,
        grid=(b, t // tt),
        in_specs=[
            pl.BlockSpec((None, tt, 2 * w), lambda bi, ti: (bi, ti, COL_CF_IN // (2 * w))),
            pl.BlockSpec((None, tt, w), lambda bi, ti: (bi, ti, COL_CF_G // w)),
            pl.BlockSpec((None, kc - 1, w), lambda bi, ti: (bi, 0, 0)),
            full((kc, w)), full((1, w)), full((1, w)), full((1, w)),
        ],
        out_specs=[
            pl.BlockSpec((None, tt, w), lambda bi, ti: (bi, ti, 0)),
            pl.BlockSpec((None, kc - 1, w), lambda bi, ti: (bi, 0, 0)),
        ],
        out_shape=[
            jax.ShapeDtypeStruct((b, t, w), BF16),
            jax.ShapeDtypeStruct((b, kc - 1, w), F32),
        ],
        scratch_shapes=[pltpu.VMEM((tt + CF_HIST, w), F32)],
        compiler_params=_cparams("parallel", "arbitrary"),
        name="cf_branch",
    )(p3, p3, conv_state, pw["cf_conv_w"], pw["cf_conv_b"], pw["cf_ln_g"], pw["cf_ln_b"])


def _sb_kernel(q_ref, k_ref, v_ref, g_ref, u_ref, acc_ref, out_ref, *, diag0):
    tq, w = q_ref.shape
    tk = tq
    n_pairs = w // LANES
    per_pair = LANES // SB_DH
    kd = pl.program_id(1) + diag0

    r_i = _iota((tq, tk), 0)
    c_i = _iota((tq, tk), 1)
    strictly_earlier = c_i < r_i
    later = (r_i > c_i).astype(BF16)
    ones = jnp.ones((tk, tk), BF16)
    top = jnp.concatenate([later, ones], axis=1)
    sums_rhs = jnp.concatenate([top, top], axis=0)

    head_of_lane = _div_pow2(_iota((1, LANES), 1), SB_DH)
    lane_masks = [(head_of_lane == r).astype(BF16) for r in range(per_pair)]
    q_masks = [m * (SB_DH ** -0.5) for m in lane_masks]
    qm = []
    for p in range(n_pairs):
        qp = q_ref[:, p * LANES:(p + 1) * LANES]
        qm.append([qp * m for m in q_masks])

    acc_ref[...] = jnp.zeros_like(acc_ref)
    out_ref[...] = jnp.zeros_like(out_ref)

    def block(kb, masked):
        off = pl.multiple_of(kb * tk, tk)
        worst = None
        for p in range(n_pairs):
            cs = slice(p * LANES, (p + 1) * LANES)
            kp = k_ref[pl.ds(off, tk), cs]
            vp = v_ref[pl.ds(off, tk), cs]
            v2 = jnp.concatenate([vp * m for m in lane_masks], axis=0)
            ws = []
            for r in range(per_pair):
                h = p * per_pair + r
                z = _dot_nt(qm[p][r], kp)
                lsm = -_softplus(z)
                if masked:
                    lsm = jnp.where(strictly_earlier, lsm, 0.0)
                hi = lsm.astype(BF16)
                lo = (lsm - hi.astype(F32)).astype(BF16)
                sums = _dot(jnp.concatenate([hi, lo], axis=1), sums_rhs)
                acc = acc_ref[h]
                lw = z + lsm + sums[:, :tk] + acc
                if masked:
                    lw = jnp.where(strictly_earlier, lw, NEG_BIG)
                ws.append(jnp.exp(lw).astype(BF16))
                acc = acc + sums[:, tk:]
                acc_ref[h] = acc
                worst = acc if worst is None else jnp.maximum(worst, acc)
            out_ref[p] = out_ref[p] + _dot(jnp.concatenate(ws, axis=1), v2)
        return jnp.max(worst)

    worst0 = block(kd, True)

    def cond(c):
        kb, worst = c
        return jnp.logical_and(kb >= 0, worst >= F32_EXP_ZERO_BELOW)

    def body(c):
        kb, _ = c
        return kb - 1, block(kb, False)

    lax.while_loop(cond, body, (kd - 1, worst0))

    o = jnp.concatenate([out_ref[p] for p in range(n_pairs)], axis=1)
    u_ref[...] = (o * _silu(g_ref[...].astype(F32))).astype(u_ref.dtype)


def _sb_branch(p3, k_all, v_all, tq):
    b, t, _ = p3.shape
    tk_total = k_all.shape[1]
    w = k_all.shape[2]
    diag0 = (tk_total - t) // tq
    return pl.pallas_call(
        functools.partial(_sb_kernel, diag0=diag0),
        grid=(b, t // tq),
        in_specs=[
            pl.BlockSpec((None, tq, w), lambda bi, qi: (bi, qi, COL_Q // w)),
            pl.BlockSpec((None, tk_total, w), lambda bi, qi: (bi, 0, 0)),
            pl.BlockSpec((None, tk_total, w), lambda bi, qi: (bi, 0, 0)),
            pl.BlockSpec((None, tq, w), lambda bi, qi: (bi, qi, COL_SB_G // w)),
        ],
        out_specs=pl.BlockSpec((None, tq, w), lambda bi, qi: (bi, qi, 0)),
        out_shape=jax.ShapeDtypeStruct((b, t, w), BF16),
        scratch_shapes=[
            pltpu.VMEM((w // SB_DH, tq, tq), F32),
            pltpu.VMEM((w // LANES, tq, LANES), F32),
        ],
        compiler_params=_cparams("parallel", "arbitrary"),
        name="sb_branch",
    )(p3, k_all, v_all, p3)


def _mix_kernel(ua_ref, ub_ref, uc_ref, ud_ref, m0_ref, m1_ref, m2_ref, m3_ref, x_ref, wd_ref, wo_ref, g_ref, o_ref):
    us = (ua_ref, ub_ref, uc_ref, ud_ref)
    ms = (m0_ref, m1_ref, m2_ref, m3_ref)
    mixed = None
    for i in range(N_BRANCH):
        y = _dot(us[i][...], wd_ref[i])
        term = _sigmoid(ms[i][...].astype(F32)) * y
        mixed = term if mixed is None else mixed + term
    o = _dot(mixed.astype(BF16), wo_ref[...])
    ms2 = jnp.mean(o * o, axis=-1, keepdims=True)
    o_ref[...] = x_ref[...] + o * lax.rsqrt(ms2 + EPS) * g_ref[...]


def _mix(us, p2, x2, pw):
    n, d = x2.shape
    w = us[0].shape[-1]
    tm = min(n, 512)
    ublk = pl.BlockSpec((tm, w), lambda i: (i, 0))
    mblk = lambda j: pl.BlockSpec((tm, d), lambda i: (i, COL_MERGE // d + j))
    return pl.pallas_call(
        _mix_kernel,
        grid=(n // tm,),
        in_specs=[ublk] * 4 + [mblk(j) for j in range(N_BRANCH)] + [
            pl.BlockSpec((tm, d), lambda i: (i, 0)),
            pl.BlockSpec((N_BRANCH, w, d), lambda i: (0, 0, 0)),
            pl.BlockSpec((d, d), lambda i: (0, 0)),
            pl.BlockSpec((1, d), lambda i: (0, 0)),
        ],
        out_specs=pl.BlockSpec((tm, d), lambda i: (i, 0)),
        out_shape=jax.ShapeDtypeStruct((n, d), F32),
        compiler_params=_cparams("parallel"),
        name="mix",
    )(*us, p2, p2, p2, p2, x2, pw["w_down"], pw["w_out"], pw["norm_post"])


def _block_diag(wb):
    n, c, d = wb.shape
    eye = jnp.eye(n, dtype=wb.dtype)
    return (eye[:, None, :, None] * wb[:, :, None, :]).reshape(n * c, n * d)


def _prep_layer(p):
    d_model = p["w_in"].shape[0]
    bw = p["lru_lambda"].shape[-1]
    cdim = p["ssd_conv_b"].shape[-1]
    n_heads = p["ssd_dt_bias"].shape[-1]
    sizes = (bw, bw, bw, cdim, n_heads, 2 * bw, bw, 3 * bw, bw, N_BRANCH * d_model)
    offs = [0]
    for s in sizes:
        offs.append(offs[-1] + s)
    seg = lambda i: p["w_in"][:, offs[i]:offs[i + 1]]
    lru_x, lru_g, ssd_z, xbc, dt, cf_in, cf_g, qkv, sb_g, merge = [seg(i) for i in range(len(sizes))]
    q, k, v = qkv[:, :bw], qkv[:, bw:2 * bw], qkv[:, 2 * bw:]
    zeros = lambda c: jnp.zeros((d_model, c), p["w_in"].dtype)
    main = jnp.concatenate(
        [xbc, cf_in, lru_x, lru_g, ssd_z, cf_g, q, sb_g, merge, dt, zeros(LANES - n_heads),
         zeros(P_COLS - COL_DT - LANES)], axis=1)
    assert main.shape[1] == P_COLS
    pad_h = lambda a, fill: jnp.concatenate([a, jnp.full((LANES - n_heads,), fill, a.dtype)])[None, :]
    row = lambda a: a[None, :].astype(F32)
    return dict(
        w_main=main.astype(BF16), w_k=k.astype(BF16), w_v=v.astype(BF16),
        norm_pre=row(p["norm_pre"]), norm_post=row(p["norm_post"]),
        lru_conv_w=p["lru_conv_w"], lru_conv_b=row(p["lru_conv_b"]),
        lru_wa=_block_diag(p["lru_wa"]).astype(BF16), lru_ba=row(p["lru_ba"]),
        lru_wx=_block_diag(p["lru_wx"]).astype(BF16), lru_bx=row(p["lru_bx"]),
        lru_lambda=row(p["lru_lambda"]),
        ssd_conv_w=p["ssd_conv_w"], ssd_conv_b=row(p["ssd_conv_b"]),
        ssd_dt_bias=pad_h(p["ssd_dt_bias"], 0.0), ssd_a_log=pad_h(p["ssd_a_log"], 0.0),
        ssd_d=jnp.repeat(p["ssd_d"], SSD_P)[None, :], ssd_norm=row(p["ssd_norm"]),
        cf_conv_w=p["cf_conv_w"], cf_conv_b=row(p["cf_conv_b"]),
        cf_ln_g=row(p["cf_ln_g"]), cf_ln_b=row(p["cf_ln_b"]),
        w_down=p["w_down"].astype(BF16), w_out=p["w_out"].astype(BF16),
    )


def _ssd_state_in(s):
    b = s.shape[0]
    s = s.reshape(b, SSD_G, SSD_R, SSD_P, SSD_N)
    return jnp.transpose(s, (0, 1, 4, 2, 3)).reshape(b, SSD_G, SSD_N, SSD_R * SSD_P)


def _ssd_state_out(s):
    b = s.shape[0]
    s = s.reshape(b, SSD_G, SSD_N, SSD_R, SSD_P)
    return jnp.transpose(s, (0, 1, 3, 4, 2)).reshape(b, SSD_G * SSD_R, SSD_P, SSD_N)


def _layer(x, pw, st, past_k, past_v, *, seq_tile, ssd_chunk, sb_block):
    b, t, d = x.shape
    x2 = x.reshape(b * t, d)
    p2 = _inproj(x2, pw["norm_pre"], pw["w_main"], BF16, P_TILE_N)
    k2, v2, kb2, vb2 = _kvproj(x2, pw["norm_pre"], pw["w_k"], pw["w_v"])
    p3 = p2.reshape(b, t, P_COLS)
    k3 = k2.reshape(b, t, -1)
    v3 = v2.reshape(b, t, -1)
    kb3 = kb2.reshape(b, t, -1)
    vb3 = vb2.reshape(b, t, -1)

    u_a, lru_h, lru_conv = _lru_branch(p3, st["lru_conv"], st["lru_h"][:, None, :], pw, seq_tile)
    u_b, ssd_new, ssd_conv = _ssd_branch(p3, st["ssd_conv"], _ssd_state_in(st["ssd"]), pw, ssd_chunk)
    u_c, cf_conv = _cf_branch(p3, st["cf_conv"], pw, seq_tile)
    if past_k is None:
        k_all, v_all = kb3, vb3
    else:
        k_all = jnp.concatenate([past_k.reshape(b, past_k.shape[1], -1).astype(BF16), kb3], axis=1)
        v_all = jnp.concatenate([past_v.reshape(b, past_v.shape[1], -1).astype(BF16), vb3], axis=1)
    u_d = _sb_branch(p3, k_all, v_all, sb_block)

    w = u_a.shape[-1]
    us = [u.reshape(b * t, w) for u in (u_a, u_b, u_c, u_d)]
    y = _mix(us, p2, x2, pw).reshape(b, t, d)
    new = dict(lru_h=lru_h[:, 0, :], lru_conv=lru_conv, ssd=_ssd_state_out(ssd_new), ssd_conv=ssd_conv,
               cf_conv=cf_conv, k=k3.reshape(b, t, SB_H, SB_DH), v=v3.reshape(b, t, SB_H, SB_DH))
    return y, new


_STATE_KEYS = ("lru_h", "lru_conv", "ssd", "ssd_conv", "cf_conv", "k", "v")


def kernel(x_prompt, x_sample, state_lru_h, state_lru_conv, state_ssd, state_ssd_conv, state_cf_conv, cache_sb_k, cache_sb_v, norm_pre, norm_post, w_in, lru_conv_w, lru_conv_b, lru_wa, lru_ba, lru_wx, lru_bx, lru_lambda, ssd_conv_w, ssd_conv_b, ssd_dt_bias, ssd_a_log, ssd_d, ssd_norm, cf_conv_w, cf_conv_b, cf_ln_g, cf_ln_b, w_down, w_out):
    raw = dict(norm_pre=norm_pre, norm_post=norm_post, w_in=w_in, lru_conv_w=lru_conv_w, lru_conv_b=lru_conv_b,
               lru_wa=lru_wa, lru_ba=lru_ba, lru_wx=lru_wx, lru_bx=lru_bx, lru_lambda=lru_lambda,
               ssd_conv_w=ssd_conv_w, ssd_conv_b=ssd_conv_b, ssd_dt_bias=ssd_dt_bias, ssd_a_log=ssd_a_log,
               ssd_d=ssd_d, ssd_norm=ssd_norm, cf_conv_w=cf_conv_w, cf_conv_b=cf_conv_b, cf_ln_g=cf_ln_g,
               cf_ln_b=cf_ln_b, w_down=w_down, w_out=w_out)
    pws = jax.vmap(_prep_layer)(raw)

    bp = x_prompt.shape[0]
    dt = x_prompt.dtype
    bw = state_lru_h.shape[-1]
    zero_state = dict(lru_h=jnp.zeros((bp, bw), dt),
                      lru_conv=jnp.zeros((bp,) + state_lru_conv.shape[2:], dt),
                      ssd=jnp.zeros((bp,) + state_ssd.shape[2:], dt),
                      ssd_conv=jnp.zeros((bp,) + state_ssd_conv.shape[2:], dt),
                      cf_conv=jnp.zeros((bp,) + state_cf_conv.shape[2:], dt))

    def prompt_step(x, pw):
        y, new = _layer(x, pw, zero_state, None, None, seq_tile=512, ssd_chunk=128, sb_block=128)
        return y, tuple(new[k] for k in _STATE_KEYS)

    y_p, p_new = lax.scan(prompt_step, x_prompt, pws)

    t_s = x_sample.shape[1]

    def sample_step(x, xs):
        pw, st, pk, pv = xs
        y, new = _layer(x, pw, st, pk, pv, seq_tile=t_s, ssd_chunk=t_s, sb_block=t_s)
        return y, tuple(new[k] for k in _STATE_KEYS)

    st_s = dict(lru_h=state_lru_h, lru_conv=state_lru_conv, ssd=state_ssd, ssd_conv=state_ssd_conv,
                cf_conv=state_cf_conv)
    y_s, s_new = lax.scan(sample_step, x_sample, (pws, st_s, cache_sb_k, cache_sb_v))

    outs = [y_p, y_s]
    for i in range(len(_STATE_KEYS)):
        outs += [p_new[i], s_new[i]]
    return tuple(outs)
```

```python
import functools

import jax
import jax.numpy as jnp
from jax import lax
from jax.experimental import pallas as pl
from jax.experimental.pallas import tpu as pltpu

F32 = jnp.float32
BF16 = jnp.bfloat16

EPS = 1e-6
LRU_C = 8.0
N_BRANCH = 4
LRU_BLOCKS = 8
SSD_P = 64
SSD_N = 64
SSD_G = 4
SSD_R = 2
SB_H = 8
SB_DH = 64

LANES = 128
SUBLANES = 8
VMEM_LIMIT_BYTES = 56 * 1024 * 1024

F32_EXP_ZERO_BELOW = -104.0
NEG_BIG = -1e30

COL_XBC = 0
COL_CF_IN = 1024
COL_LRU_X = 2048
COL_LRU_G = 2560
COL_SSD_Z = 3072
COL_CF_G = 3584
COL_Q = 4096
COL_SB_G = 4608
COL_MERGE = 5120
COL_DT = 9216
P_COLS = 9600
P_TILE_N = 1920


def _cparams(*sem):
    return pltpu.CompilerParams(dimension_semantics=sem, vmem_limit_bytes=VMEM_LIMIT_BYTES)


def _sigmoid(x):
    return 0.5 * jnp.tanh(0.5 * x) + 0.5


def _silu(x):
    return x * _sigmoid(x)


def _softplus(x):
    return jnp.maximum(x, 0.0) + jnp.log(1.0 + jnp.exp(-jnp.abs(x)))


def _split3(x):
    hi = x.astype(BF16)
    r = x - hi.astype(F32)
    mid = r.astype(BF16)
    lo = (r - mid.astype(F32)).astype(BF16)
    return hi, mid, lo


def _dot(a, b):
    return jnp.dot(a, b, preferred_element_type=F32)


def _dot_nt(a, b):
    return lax.dot_general(a, b, (((1,), (1,)), ((), ())), preferred_element_type=F32)


def _sel_left(m01, x):
    hi, mid, lo = _split3(x)
    return _dot(m01, hi) + _dot(m01, mid) + _dot(m01, lo)


def _sel_right(x, m01):
    hi, mid, lo = _split3(x)
    return _dot(hi, m01) + _dot(mid, m01) + _dot(lo, m01)


def _sel_nt(m01, x):
    hi, mid, lo = _split3(x)
    return _dot_nt(m01, hi) + _dot_nt(m01, mid) + _dot_nt(m01, lo)


def _iota(shape, dim):
    return lax.broadcasted_iota(jnp.int32, shape, dim)


def _div_pow2(x, n):
    shift = n.bit_length() - 1
    assert 1 << shift == n
    return lax.shift_right_logical(x, shift)


def _inproj_kernel(x_ref, g_ref, w_ref, o_ref, h_ref):
    @pl.when(pl.program_id(1) == 0)
    def _():
        x = x_ref[...]
        ms = jnp.mean(x * x, axis=-1, keepdims=True)
        h_ref[...] = (x * lax.rsqrt(ms + EPS) * g_ref[...]).astype(BF16)

    o_ref[...] = _dot(h_ref[...], w_ref[...]).astype(o_ref.dtype)


def _inproj(x2, g, w, out_dtype, tn):
    n, d = x2.shape
    cols = w.shape[1]
    tm = min(n, 1024)
    return pl.pallas_call(
        _inproj_kernel,
        grid=(n // tm, cols // tn),
        in_specs=[
            pl.BlockSpec((tm, d), lambda i, j: (i, 0)),
            pl.BlockSpec((1, d), lambda i, j: (0, 0)),
            pl.BlockSpec((d, tn), lambda i, j: (0, j)),
        ],
        out_specs=pl.BlockSpec((tm, tn), lambda i, j: (i, j)),
        out_shape=jax.ShapeDtypeStruct((n, cols), out_dtype),
        scratch_shapes=[pltpu.VMEM((tm, d), BF16)],
        compiler_params=_cparams("parallel", "arbitrary"),
        name="inproj_" + jnp.dtype(out_dtype).name,
    )(x2, g, w)


def _kv_kernel(x_ref, g_ref, wk_ref, wv_ref, k_ref, v_ref, kb_ref, vb_ref):
    x = x_ref[...]
    ms = jnp.mean(x * x, axis=-1, keepdims=True)
    h = (x * lax.rsqrt(ms + EPS) * g_ref[...]).astype(BF16)
    k = _dot(h, wk_ref[...])
    v = _dot(h, wv_ref[...])
    k_ref[...] = k
    v_ref[...] = v
    kb_ref[...] = k.astype(BF16)
    vb_ref[...] = v.astype(BF16)


def _kvproj(x2, g, wk, wv):
    n, d = x2.shape
    w = wk.shape[1]
    tm = min(n, 1024)
    return pl.pallas_call(
        _kv_kernel,
        grid=(n // tm,),
        in_specs=[
            pl.BlockSpec((tm, d), lambda i: (i, 0)),
            pl.BlockSpec((1, d), lambda i: (0, 0)),
            pl.BlockSpec((d, w), lambda i: (0, 0)),
            pl.BlockSpec((d, w), lambda i: (0, 0)),
        ],
        out_specs=[pl.BlockSpec((tm, w), lambda i: (i, 0))] * 4,
        out_shape=[jax.ShapeDtypeStruct((n, w), F32)] * 2 + [jax.ShapeDtypeStruct((n, w), BF16)] * 2,
        compiler_params=_cparams("parallel"),
        name="kvproj",
    )(x2, g, wk, wv)


def _conv_taps(xe_ref, w_ref, b_ref, base, rows, hist_off, k):
    acc = None
    for j in range(k):
        term = w_ref[j:j + 1, :] * xe_ref[pl.ds(base + hist_off + j, rows), :]
        acc = term if acc is None else acc + term
    return acc + b_ref[...]


LRU_HIST = 8


def _lru_kernel(x_ref, g_ref, cs_ref, h0_ref, cw_ref, cb_ref, wa_ref, ba_ref, wx_ref, bx_ref, lam_ref,
                u_ref, hl_ref, cn_ref, xe_ref, a_ref, b_ref, h_ref):
    tt, w = x_ref.shape
    kc = cw_ref.shape[0]
    hist = LRU_HIST - (kc - 1)

    @pl.when(pl.program_id(1) == 0)
    def _():
        xe_ref[hist:LRU_HIST, :] = cs_ref[...]
        h_ref[...] = h0_ref[...]

    xe_ref[LRU_HIST:LRU_HIST + tt, :] = x_ref[...].astype(F32)
    xc = _conv_taps(xe_ref, cw_ref, cb_ref, 0, tt, hist, kc)
    tail = xe_ref[tt + hist:tt + LRU_HIST, :]
    xe_ref[hist:LRU_HIST, :] = tail
    cn_ref[...] = tail

    xcb = xc.astype(BF16)
    gate_a = _dot(xcb, wa_ref[...]) + ba_ref[...]
    gate_x = _dot(xcb, wx_ref[...]) + bx_ref[...]
    r = _sigmoid(gate_a)
    i = _sigmoid(gate_x)
    log_a = (-LRU_C) * r * _softplus(-lam_ref[...])
    a = jnp.exp(log_a)
    one_m_a2 = (1.0 - a) * (1.0 + a)
    a_ref[...] = a
    b_ref[...] = jnp.sqrt(one_m_a2) * (i * xc)

    row = _iota((SUBLANES, w), 0)

    def body(s, h):
        off = pl.multiple_of(s * SUBLANES, SUBLANES)
        a8 = a_ref[pl.ds(off, SUBLANES), :]
        b8 = b_ref[pl.ds(off, SUBLANES), :]
        for d in (1, 2, 4):
            keep = row >= d
            a_sh = pltpu.roll(a8, d, axis=0)
            b_sh = pltpu.roll(b8, d, axis=0)
            b8 = jnp.where(keep, a8 * b_sh + b8, b8)
            a8 = jnp.where(keep, a8 * a_sh, a8)
        h8 = a8 * h + b8
        b_ref[pl.ds(off, SUBLANES), :] = h8
        return h8[SUBLANES - 1:SUBLANES, :]

    h_last = lax.fori_loop(0, tt // SUBLANES, body, h_ref[...])
    h_ref[...] = h_last
    hl_ref[...] = h_last
    u_ref[...] = (b_ref[...] * _silu(g_ref[...].astype(F32))).astype(u_ref.dtype)


def _lru_branch(p3, conv_state, h0, pw, tt):
    b, t, _ = p3.shape
    w = conv_state.shape[-1]
    kc = pw["lru_conv_w"].shape[0]
    col = lambda off: off // w
    full = lambda shape: pl.BlockSpec(shape, lambda bi, ti: (0,) * len(shape))
    return pl.pallas_call(
        _lru_kernel,
        grid=(b, t // tt),
        in_specs=[
            pl.BlockSpec((None, tt, w), lambda bi, ti: (bi, ti, col(COL_LRU_X))),
            pl.BlockSpec((None, tt, w), lambda bi, ti: (bi, ti, col(COL_LRU_G))),
            pl.BlockSpec((None, kc - 1, w), lambda bi, ti: (bi, 0, 0)),
            pl.BlockSpec((None, 1, w), lambda bi, ti: (bi, 0, 0)),
            full((kc, w)), full((1, w)), full((w, w)), full((1, w)), full((w, w)), full((1, w)), full((1, w)),
        ],
        out_specs=[
            pl.BlockSpec((None, tt, w), lambda bi, ti: (bi, ti, 0)),
            pl.BlockSpec((None, 1, w), lambda bi, ti: (bi, 0, 0)),
            pl.BlockSpec((None, kc - 1, w), lambda bi, ti: (bi, 0, 0)),
        ],
        out_shape=[
            jax.ShapeDtypeStruct((b, t, w), BF16),
            jax.ShapeDtypeStruct((b, 1, w), F32),
            jax.ShapeDtypeStruct((b, kc - 1, w), F32),
        ],
        scratch_shapes=[
            pltpu.VMEM((tt + LRU_HIST, w), F32),
            pltpu.VMEM((tt, w), F32),
            pltpu.VMEM((tt, w), F32),
            pltpu.VMEM((1, w), F32),
        ],
        compiler_params=_cparams("parallel", "arbitrary"),
        name="lru_branch",
    )(p3, p3, conv_state, h0, pw["lru_conv_w"], pw["lru_conv_b"], pw["lru_wa"], pw["lru_ba"],
      pw["lru_wx"], pw["lru_bx"], pw["lru_lambda"])


def _ssd_kernel(z_ref, xbc_ref, dt_ref, cs_ref, s0_ref, cw_ref, cb_ref, dtb_ref, alog_ref, dvec_ref, nw_ref,
                u_ref, sn_ref, cn_ref, xe_ref, st_ref):
    L = z_ref.shape[0]
    inner = z_ref.shape[1]
    kc = cw_ref.shape[0]
    hist = LRU_HIST - (kc - 1)
    gw = SSD_R * SSD_P

    @pl.when(pl.program_id(1) == 0)
    def _():
        xe_ref[hist:LRU_HIST, :] = cs_ref[...]
        st_ref[...] = s0_ref[...]

    xe_ref[LRU_HIST:LRU_HIST + L, :] = xbc_ref[...].astype(F32)
    xa = _silu(_conv_taps(xe_ref, cw_ref, cb_ref, 0, L, hist, kc))
    tail = xe_ref[L + hist:L + LRU_HIST, :]
    xe_ref[hist:LRU_HIST, :] = tail
    cn_ref[...] = tail

    x = xa[:, :inner]
    bm = xa[:, inner:inner + SSD_G * SSD_N]
    cm = xa[:, inner + SSD_G * SSD_N:]

    dt = _softplus(dt_ref[...].astype(F32) + dtb_ref[...])
    adt = dt * (-jnp.exp(alog_ref[...]))

    r_i = _iota((L, L), 0)
    c_i = _iota((L, L), 1)
    causal = r_i >= c_i
    tril = causal.astype(BF16)
    eye = (_iota((LANES, LANES), 0) == _iota((LANES, LANES), 1)).astype(BF16)
    eye_n = eye[:SSD_N, :SSD_N]
    expand_c = (_div_pow2(_iota((LANES, inner), 1), SSD_P) == _iota((LANES, inner), 0)).astype(BF16)
    n_heads = inner // SSD_P
    expand_l = (_div_pow2(_iota((LANES, n_heads * L), 1), L) == _iota((LANES, n_heads * L), 0)).astype(BF16)

    acs = _sel_left(tril, adt)
    acs_t = _sel_nt(eye, acs)
    dt_c = _sel_right(dt, expand_c)
    acs_c = _sel_right(acs, expand_c)
    acs_w = _sel_right(acs, expand_l)

    xdt = x * dt_c
    last = acs_c[L - 1:L, :]
    xs = xdt * jnp.exp(last - acs_c)
    e_acs = jnp.exp(acs_c)
    chunk_decay = jnp.exp(last)
    lane = _iota((L, gw), 1)

    ys = []
    for g in range(SSD_G):
        cg = cm[:, g * SSD_N:(g + 1) * SSD_N].astype(BF16)
        bg = bm[:, g * SSD_N:(g + 1) * SSD_N].astype(BF16)
        cb = _dot_nt(cg, bg)
        ms = []
        for r in range(SSD_R):
            h = g * SSD_R + r
            seg = acs_w[:, h * L:(h + 1) * L] - acs_t[h:h + 1, :]
            dec = jnp.exp(jnp.where(causal, seg, NEG_BIG))
            ms.append((cb * dec).astype(BF16))
        mcat = jnp.concatenate(ms, axis=1)
        xg = xdt[:, g * gw:(g + 1) * gw]
        rhs = jnp.concatenate([jnp.where(lane < SSD_P, xg, 0.0), jnp.where(lane >= SSD_P, xg, 0.0)],
                              axis=0).astype(BF16)
        y_diag = _dot(mcat, rhs)
        st = st_ref[g]
        y_off = _dot(cg, st.astype(BF16)) * e_acs[:, g * gw:(g + 1) * gw]
        bg_t = _dot_nt(eye_n, bg).astype(BF16)
        st_ref[g] = st * chunk_decay[:, g * gw:(g + 1) * gw] + _dot(bg_t, xs[:, g * gw:(g + 1) * gw].astype(BF16))
        ys.append(y_diag + y_off)
    y = jnp.concatenate(ys, axis=1) + dvec_ref[...] * x
    sn_ref[...] = st_ref[...]

    gated = y * _silu(z_ref[...].astype(F32))
    ms2 = jnp.mean(gated * gated, axis=-1, keepdims=True)
    u_ref[...] = (gated * lax.rsqrt(ms2 + EPS) * nw_ref[...]).astype(u_ref.dtype)


def _ssd_branch(p3, conv_state, st0, pw, L):
    b, t, _ = p3.shape
    cdim = conv_state.shape[-1]
    inner = pw["ssd_norm"].shape[-1]
    kc = pw["ssd_conv_w"].shape[0]
    gw = SSD_R * SSD_P
    full = lambda shape: pl.BlockSpec(shape, lambda bi, ti: (0,) * len(shape))
    return pl.pallas_call(
        _ssd_kernel,
        grid=(b, t // L),
        in_specs=[
            pl.BlockSpec((None, L, inner), lambda bi, ti: (bi, ti, COL_SSD_Z // inner)),
            pl.BlockSpec((None, L, cdim), lambda bi, ti: (bi, ti, COL_XBC // cdim)),
            pl.BlockSpec((None, L, LANES), lambda bi, ti: (bi, ti, COL_DT // LANES)),
            pl.BlockSpec((None, kc - 1, cdim), lambda bi, ti: (bi, 0, 0)),
            pl.BlockSpec((None, SSD_G, SSD_N, gw), lambda bi, ti: (bi, 0, 0, 0)),
            full((kc, cdim)), full((1, cdim)), full((1, LANES)), full((1, LANES)), full((1, inner)), full((1, inner)),
        ],
        out_specs=[
            pl.BlockSpec((None, L, inner), lambda bi, ti: (bi, ti, 0)),
            pl.BlockSpec((None, SSD_G, SSD_N, gw), lambda bi, ti: (bi, 0, 0, 0)),
            pl.BlockSpec((None, kc - 1, cdim), lambda bi, ti: (bi, 0, 0)),
        ],
        out_shape=[
            jax.ShapeDtypeStruct((b, t, inner), BF16),
            jax.ShapeDtypeStruct((b, SSD_G, SSD_N, gw), F32),
            jax.ShapeDtypeStruct((b, kc - 1, cdim), F32),
        ],
        scratch_shapes=[
            pltpu.VMEM((L + LRU_HIST, cdim), F32),
            pltpu.VMEM((SSD_G, SSD_N, gw), F32),
        ],
        compiler_params=_cparams("parallel", "arbitrary"),
        name="ssd_branch",
    )(p3, p3, p3, conv_state, st0, pw["ssd_conv_w"], pw["ssd_conv_b"], pw["ssd_dt_bias"], pw["ssd_a_log"],
      pw["ssd_d"], pw["ssd_norm"])


CF_HIST = 32
CF_ROWS = 64


def _cf_kernel(in_ref, g_ref, cs_ref, cw_ref, cb_ref, lg_ref, lb_ref, u_ref, cn_ref, xe_ref):
    tt = in_ref.shape[0]
    w = g_ref.shape[1]
    kc = cw_ref.shape[0]
    hist = CF_HIST - (kc - 1)

    @pl.when(pl.program_id(1) == 0)
    def _():
        xe_ref[hist:CF_HIST, :] = cs_ref[...]

    cf = in_ref[...].astype(F32)
    xe_ref[CF_HIST:CF_HIST + tt, :] = cf[:, :w] * _sigmoid(cf[:, w:])

    def body(s, carry):
        base = pl.multiple_of(s * CF_ROWS, CF_ROWS)
        c = None
        for r in range(SUBLANES):
            rows = CF_ROWS + (SUBLANES if r else 0)
            part = None
            for o in range(hist + (r - hist) % SUBLANES, hist + kc, SUBLANES):
                term = cw_ref[o - hist:o - hist + 1, :] * xe_ref[pl.ds(base + (o - r), rows), :]
                part = term if part is None else part + term
            piece = part[r:r + CF_ROWS, :]
            c = piece if c is None else c + piece
        c = c + cb_ref[...]
        mu = jnp.mean(c, axis=-1, keepdims=True)
        cc = c - mu
        var = jnp.mean(cc * cc, axis=-1, keepdims=True)
        ln = cc * lax.rsqrt(var + EPS) * lg_ref[...] + lb_ref[...]
        gate = g_ref[pl.ds(base, CF_ROWS), :].astype(F32)
        u_ref[pl.ds(base, CF_ROWS), :] = (_silu(ln) * _silu(gate)).astype(u_ref.dtype)
        return carry

    lax.fori_loop(0, tt // CF_ROWS, body, 0)
    tail = xe_ref[tt + hist:tt + CF_HIST, :]
    xe_ref[hist:CF_HIST, :] = tail
    cn_ref[...] = tail


def _cf_branch(p3, conv_state, pw, tt):
    b, t, _ = p3.shape
    w = conv_state.shape[-1]
    kc = pw["cf_conv_w"].shape[0]
    full = lambda shape: pl.BlockSpec(shape, lambda bi, ti: (0,) * len(shape))
    return pl.pallas_call(
        _cf_kernel,
        grid=(b, t // tt),
        in_specs=[
            pl.BlockSpec((None, tt, 2 * w), lambda bi, ti: (bi, ti, COL_CF_IN // (2 * w))),
            pl.BlockSpec((None, tt, w), lambda bi, ti: (bi, ti, COL_CF_G // w)),
            pl.BlockSpec((None, kc - 1, w), lambda bi, ti: (bi, 0, 0)),
            full((kc, w)), full((1, w)), full((1, w)), full((1, w)),
        ],
        out_specs=[
            pl.BlockSpec((None, tt, w), lambda bi, ti: (bi, ti, 0)),
            pl.BlockSpec((None, kc - 1, w), lambda bi, ti: (bi, 0, 0)),
        ],
        out_shape=[
            jax.ShapeDtypeStruct((b, t, w), BF16),
            jax.ShapeDtypeStruct((b, kc - 1, w), F32),
        ],
        scratch_shapes=[pltpu.VMEM((tt + CF_HIST, w), F32)],
        compiler_params=_cparams("parallel", "arbitrary"),
        name="cf_branch",
    )(p3, p3, conv_state, pw["cf_conv_w"], pw["cf_conv_b"], pw["cf_ln_g"], pw["cf_ln_b"])


def _sb_kernel(q_ref, k_ref, v_ref, g_ref, u_ref, acc_ref, out_ref, *, diag0):
    tq, w = q_ref.shape
    tk = tq
    n_pairs = w // LANES
    per_pair = LANES // SB_DH
    kd = pl.program_id(1) + diag0

    r_i = _iota((tq, tk), 0)
    c_i = _iota((tq, tk), 1)
    strictly_earlier = c_i < r_i
    later = (r_i > c_i).astype(BF16)
    ones = jnp.ones((tk, tk), BF16)
    top = jnp.concatenate([later, ones], axis=1)
    sums_rhs = jnp.concatenate([top, top], axis=0)

    head_of_lane = _div_pow2(_iota((1, LANES), 1), SB_DH)
    lane_masks = [(head_of_lane == r).astype(BF16) for r in range(per_pair)]
    q_masks = [m * (SB_DH ** -0.5) for m in lane_masks]
    qm = []
    for p in range(n_pairs):
        qp = q_ref[:, p * LANES:(p + 1) * LANES]
        qm.append(jnp.concatenate([qp * m for m in q_masks], axis=0))
    earlier_stacked = jnp.concatenate([strictly_earlier] * per_pair, axis=0)

    acc_ref[...] = jnp.zeros_like(acc_ref)
    out_ref[...] = jnp.zeros_like(out_ref)

    def block(kb, masked):
        off = pl.multiple_of(kb * tk, tk)
        zs, lsms, hls = [], [], []
        for p in range(n_pairs):
            kp = k_ref[pl.ds(off, tk), p * LANES:(p + 1) * LANES]
            z = _dot_nt(qm[p], kp)
            lsm = -_softplus(z)
            if masked:
                lsm = jnp.where(earlier_stacked, lsm, 0.0)
            hi = lsm.astype(BF16)
            lo = (lsm - hi.astype(F32)).astype(BF16)
            zs.append(z)
            lsms.append(lsm)
            hls.append(jnp.concatenate([hi, lo], axis=1))
        sums = _dot(jnp.concatenate(hls, axis=0), sums_rhs)
        worst = None
        for p in range(n_pairs):
            rows = slice(p * per_pair * tq, (p + 1) * per_pair * tq)
            acc = acc_ref[p]
            lw = zs[p] + lsms[p] + sums[rows, :tk] + acc
            if masked:
                lw = jnp.where(earlier_stacked, lw, NEG_BIG)
            wgt = jnp.exp(lw).astype(BF16)
            acc = acc + sums[rows, tk:]
            acc_ref[p] = acc
            worst = acc if worst is None else jnp.maximum(worst, acc)
            vp = v_ref[pl.ds(off, tk), p * LANES:(p + 1) * LANES]
            v2 = jnp.concatenate([vp * m for m in lane_masks], axis=0)
            wcat = jnp.concatenate([wgt[r * tq:(r + 1) * tq] for r in range(per_pair)], axis=1)
            out_ref[p] = out_ref[p] + _dot(wcat, v2)
        return jnp.max(worst)

    worst0 = block(kd, True)

    def cond(c):
        kb, worst = c
        return jnp.logical_and(kb >= 0, worst >= F32_EXP_ZERO_BELOW)

    def body(c):
        kb, _ = c
        return kb - 1, block(kb, False)

    lax.while_loop(cond, body, (kd - 1, worst0))

    o = jnp.concatenate([out_ref[p] for p in range(n_pairs)], axis=1)
    u_ref[...] = (o * _silu(g_ref[...].astype(F32))).astype(u_ref.dtype)


def _sb_branch(p3, k_all, v_all, tq):
    b, t, _ = p3.shape
    tk_total = k_all.shape[1]
    w = k_all.shape[2]
    diag0 = (tk_total - t) // tq
    return pl.pallas_call(
        functools.partial(_sb_kernel, diag0=diag0),
        grid=(b, t // tq),
        in_specs=[
            pl.BlockSpec((None, tq, w), lambda bi, qi: (bi, qi, COL_Q // w)),
            pl.BlockSpec((None, tk_total, w), lambda bi, qi: (bi, 0, 0)),
            pl.BlockSpec((None, tk_total, w), lambda bi, qi: (bi, 0, 0)),
            pl.BlockSpec((None, tq, w), lambda bi, qi: (bi, qi, COL_SB_G // w)),
        ],
        out_specs=pl.BlockSpec((None, tq, w), lambda bi, qi: (bi, qi, 0)),
        out_shape=jax.ShapeDtypeStruct((b, t, w), BF16),
        scratch_shapes=[
            pltpu.VMEM((w // LANES, (LANES // SB_DH) * tq, tq), F32),
            pltpu.VMEM((w // LANES, tq, LANES), F32),
        ],
        compiler_params=_cparams("parallel", "arbitrary"),
        name="sb_branch",
    )(p3, k_all, v_all, p3)


def _mix_kernel(ua_ref, ub_ref, uc_ref, ud_ref, m0_ref, m1_ref, m2_ref, m3_ref, x_ref, wd_ref, wo_ref, g_ref, o_ref):
    us = (ua_ref, ub_ref, uc_ref, ud_ref)
    ms = (m0_ref, m1_ref, m2_ref, m3_ref)
    mixed = None
    for i in range(N_BRANCH):
        y = _dot(us[i][...], wd_ref[i])
        term = _sigmoid(ms[i][...].astype(F32)) * y
        mixed = term if mixed is None else mixed + term
    o = _dot(mixed.astype(BF16), wo_ref[...])
    ms2 = jnp.mean(o * o, axis=-1, keepdims=True)
    o_ref[...] = x_ref[...] + o * lax.rsqrt(ms2 + EPS) * g_ref[...]


def _mix(us, p2, x2, pw):
    n, d = x2.shape
    w = us[0].shape[-1]
    tm = min(n, 512)
    ublk = pl.BlockSpec((tm, w), lambda i: (i, 0))
    mblk = lambda j: pl.BlockSpec((tm, d), lambda i: (i, COL_MERGE // d + j))
    return pl.pallas_call(
        _mix_kernel,
        grid=(n // tm,),
        in_specs=[ublk] * 4 + [mblk(j) for j in range(N_BRANCH)] + [
            pl.BlockSpec((tm, d), lambda i: (i, 0)),
            pl.BlockSpec((N_BRANCH, w, d), lambda i: (0, 0, 0)),
            pl.BlockSpec((d, d), lambda i: (0, 0)),
            pl.BlockSpec((1, d), lambda i: (0, 0)),
        ],
        out_specs=pl.BlockSpec((tm, d), lambda i: (i, 0)),
        out_shape=jax.ShapeDtypeStruct((n, d), F32),
        input_output_aliases={len(us) + N_BRANCH: 0},
        compiler_params=_cparams("parallel"),
        name="mix",
    )(*us, p2, p2, p2, p2, x2, pw["w_down"], pw["w_out"], pw["norm_post"])


def _block_diag(wb):
    n, c, d = wb.shape
    eye = jnp.eye(n, dtype=wb.dtype)
    return (eye[:, None, :, None] * wb[:, :, None, :]).reshape(n * c, n * d)


def _prep_layer(p):
    d_model = p["w_in"].shape[0]
    bw = p["lru_lambda"].shape[-1]
    cdim = p["ssd_conv_b"].shape[-1]
    n_heads = p["ssd_dt_bias"].shape[-1]
    sizes = (bw, bw, bw, cdim, n_heads, 2 * bw, bw, 3 * bw, bw, N_BRANCH * d_model)
    offs = [0]
    for s in sizes:
        offs.append(offs[-1] + s)
    seg = lambda i: p["w_in"][:, offs[i]:offs[i + 1]]
    lru_x, lru_g, ssd_z, xbc, dt, cf_in, cf_g, qkv, sb_g, merge = [seg(i) for i in range(len(sizes))]
    q, k, v = qkv[:, :bw], qkv[:, bw:2 * bw], qkv[:, 2 * bw:]
    zeros = lambda c: jnp.zeros((d_model, c), p["w_in"].dtype)
    main = jnp.concatenate(
        [xbc, cf_in, lru_x, lru_g, ssd_z, cf_g, q, sb_g, merge, dt, zeros(LANES - n_heads),
         zeros(P_COLS - COL_DT - LANES)], axis=1)
    assert main.shape[1] == P_COLS
    pad_h = lambda a, fill: jnp.concatenate([a, jnp.full((LANES - n_heads,), fill, a.dtype)])[None, :]
    row = lambda a: a[None, :].astype(F32)
    return dict(
        w_main=main.astype(BF16), w_k=k.astype(BF16), w_v=v.astype(BF16),
        norm_pre=row(p["norm_pre"]), norm_post=row(p["norm_post"]),
        lru_conv_w=p["lru_conv_w"], lru_conv_b=row(p["lru_conv_b"]),
        lru_wa=_block_diag(p["lru_wa"]).astype(BF16), lru_ba=row(p["lru_ba"]),
        lru_wx=_block_diag(p["lru_wx"]).astype(BF16), lru_bx=row(p["lru_bx"]),
        lru_lambda=row(p["lru_lambda"]),
        ssd_conv_w=p["ssd_conv_w"], ssd_conv_b=row(p["ssd_conv_b"]),
        ssd_dt_bias=pad_h(p["ssd_dt_bias"], 0.0), ssd_a_log=pad_h(p["ssd_a_log"], 0.0),
        ssd_d=jnp.repeat(p["ssd_d"], SSD_P)[None, :], ssd_norm=row(p["ssd_norm"]),
        cf_conv_w=p["cf_conv_w"], cf_conv_b=row(p["cf_conv_b"]),
        cf_ln_g=row(p["cf_ln_g"]), cf_ln_b=row(p["cf_ln_b"]),
        w_down=p["w_down"].astype(BF16), w_out=p["w_out"].astype(BF16),
    )


def _ssd_state_in(s):
    b = s.shape[0]
    s = s.reshape(b, SSD_G, SSD_R, SSD_P, SSD_N)
    return jnp.transpose(s, (0, 1, 4, 2, 3)).reshape(b, SSD_G, SSD_N, SSD_R * SSD_P)


def _ssd_state_out(s):
    b = s.shape[0]
    s = s.reshape(b, SSD_G, SSD_N, SSD_R, SSD_P)
    return jnp.transpose(s, (0, 1, 3, 4, 2)).reshape(b, SSD_G * SSD_R, SSD_P, SSD_N)


def _layer(x, pw, st, past_k, past_v, *, seq_tile, ssd_chunk, sb_block):
    b, t, d = x.shape
    x2 = x.reshape(b * t, d)
    p2 = _inproj(x2, pw["norm_pre"], pw["w_main"], BF16, P_TILE_N)
    k2, v2, kb2, vb2 = _kvproj(x2, pw["norm_pre"], pw["w_k"], pw["w_v"])
    p3 = p2.reshape(b, t, P_COLS)
    k3 = k2.reshape(b, t, -1)
    v3 = v2.reshape(b, t, -1)
    kb3 = kb2.reshape(b, t, -1)
    vb3 = vb2.reshape(b, t, -1)

    u_a, lru_h, lru_conv = _lru_branch(p3, st["lru_conv"], st["lru_h"][:, None, :], pw, seq_tile)
    u_b, ssd_new, ssd_conv = _ssd_branch(p3, st["ssd_conv"], _ssd_state_in(st["ssd"]), pw, ssd_chunk)
    u_c, cf_conv = _cf_branch(p3, st["cf_conv"], pw, seq_tile)
    if past_k is None:
        k_all, v_all = kb3, vb3
    else:
        k_all = jnp.concatenate([past_k.reshape(b, past_k.shape[1], -1).astype(BF16), kb3], axis=1)
        v_all = jnp.concatenate([past_v.reshape(b, past_v.shape[1], -1).astype(BF16), vb3], axis=1)
    u_d = _sb_branch(p3, k_all, v_all, sb_block)

    w = u_a.shape[-1]
    us = [u.reshape(b * t, w) for u in (u_a, u_b, u_c, u_d)]
    y = _mix(us, p2, x2, pw).reshape(b, t, d)
    new = dict(lru_h=lru_h[:, 0, :], lru_conv=lru_conv, ssd=_ssd_state_out(ssd_new), ssd_conv=ssd_conv,
               cf_conv=cf_conv, k=k3.reshape(b, t, SB_H, SB_DH), v=v3.reshape(b, t, SB_H, SB_DH))
    return y, new


_STATE_KEYS = ("lru_h", "lru_conv", "ssd", "ssd_conv", "cf_conv", "k", "v")


def kernel(x_prompt, x_sample, state_lru_h, state_lru_conv, state_ssd, state_ssd_conv, state_cf_conv, cache_sb_k, cache_sb_v, norm_pre, norm_post, w_in, lru_conv_w, lru_conv_b, lru_wa, lru_ba, lru_wx, lru_bx, lru_lambda, ssd_conv_w, ssd_conv_b, ssd_dt_bias, ssd_a_log, ssd_d, ssd_norm, cf_conv_w, cf_conv_b, cf_ln_g, cf_ln_b, w_down, w_out):
    raw = dict(norm_pre=norm_pre, norm_post=norm_post, w_in=w_in, lru_conv_w=lru_conv_w, lru_conv_b=lru_conv_b,
               lru_wa=lru_wa, lru_ba=lru_ba, lru_wx=lru_wx, lru_bx=lru_bx, lru_lambda=lru_lambda,
               ssd_conv_w=ssd_conv_w, ssd_conv_b=ssd_conv_b, ssd_dt_bias=ssd_dt_bias, ssd_a_log=ssd_a_log,
               ssd_d=ssd_d, ssd_norm=ssd_norm, cf_conv_w=cf_conv_w, cf_conv_b=cf_conv_b, cf_ln_g=cf_ln_g,
               cf_ln_b=cf_ln_b, w_down=w_down, w_out=w_out)
    pws = jax.vmap(_prep_layer)(raw)

    bp = x_prompt.shape[0]
    dt = x_prompt.dtype
    bw = state_lru_h.shape[-1]
    zero_state = dict(lru_h=jnp.zeros((bp, bw), dt),
                      lru_conv=jnp.zeros((bp,) + state_lru_conv.shape[2:], dt),
                      ssd=jnp.zeros((bp,) + state_ssd.shape[2:], dt),
                      ssd_conv=jnp.zeros((bp,) + state_ssd_conv.shape[2:], dt),
                      cf_conv=jnp.zeros((bp,) + state_cf_conv.shape[2:], dt))

    def prompt_step(x, pw):
        y, new = _layer(x, pw, zero_state, None, None, seq_tile=512, ssd_chunk=128, sb_block=128)
        return y, tuple(new[k] for k in _STATE_KEYS)

    y_p, p_new = lax.scan(prompt_step, x_prompt, pws)

    t_s = x_sample.shape[1]

    def sample_step(x, xs):
        pw, st, pk, pv = xs
        y, new = _layer(x, pw, st, pk, pv, seq_tile=t_s, ssd_chunk=t_s, sb_block=t_s)
        return y, tuple(new[k] for k in _STATE_KEYS)

    st_s = dict(lru_h=state_lru_h, lru_conv=state_lru_conv, ssd=state_ssd, ssd_conv=state_ssd_conv,
                cf_conv=state_cf_conv)
    y_s, s_new = lax.scan(sample_step, x_sample, (pws, st_s, cache_sb_k, cache_sb_v))

    outs = [y_p, y_s]
    for i in range(len(_STATE_KEYS)):
        outs += [p_new[i], s_new[i]]
    return tuple(outs)
```

```python
import functools

import jax
import jax.numpy as jnp
from jax import lax
from jax.experimental import pallas as pl
from jax.experimental.pallas import tpu as pltpu

F32 = jnp.float32
BF16 = jnp.bfloat16

EPS = 1e-6
LRU_C = 8.0
N_BRANCH = 4
LRU_BLOCKS = 8
SSD_P = 64
SSD_N = 64
SSD_G = 4
SSD_R = 2
SB_H = 8
SB_DH = 64

LANES = 128
SUBLANES = 8
VMEM_LIMIT_BYTES = 56 * 1024 * 1024

F32_EXP_ZERO_BELOW = -104.0
NEG_BIG = -1e30

COL_XBC = 0
COL_CF_IN = 1024
COL_LRU_X = 2048
COL_LRU_G = 2560
COL_SSD_Z = 3072
COL_CF_G = 3584
COL_Q = 4096
COL_SB_G = 4608
COL_MERGE = 5120
COL_DT = 9216
P_COLS = 9472
P_TILE_N = P_COLS // 2
INPROJ_CHUNK = 512


def _cparams(*sem):
    return pltpu.CompilerParams(dimension_semantics=sem, vmem_limit_bytes=VMEM_LIMIT_BYTES)


def _sigmoid(x):
    return 0.5 * jnp.tanh(0.5 * x) + 0.5


def _silu(x):
    return x * _sigmoid(x)


def _softplus(x):
    return jnp.maximum(x, 0.0) + jnp.log(1.0 + jnp.exp(-jnp.abs(x)))


def _split3(x):
    hi = x.astype(BF16)
    r = x - hi.astype(F32)
    mid = r.astype(BF16)
    lo = (r - mid.astype(F32)).astype(BF16)
    return hi, mid, lo


def _dot(a, b):
    return jnp.dot(a, b, preferred_element_type=F32)


def _dot_nt(a, b):
    return lax.dot_general(a, b, (((1,), (1,)), ((), ())), preferred_element_type=F32)


def _sel_left(m01, x):
    hi, mid, lo = _split3(x)
    return _dot(m01, hi) + _dot(m01, mid) + _dot(m01, lo)


def _sel_right(x, m01):
    hi, mid, lo = _split3(x)
    return _dot(hi, m01) + _dot(mid, m01) + _dot(lo, m01)


def _sel_nt(m01, x):
    hi, mid, lo = _split3(x)
    return _dot_nt(m01, hi) + _dot_nt(m01, mid) + _dot_nt(m01, lo)


def _iota(shape, dim):
    return lax.broadcasted_iota(jnp.int32, shape, dim)


def _div_pow2(x, n):
    shift = n.bit_length() - 1
    assert 1 << shift == n
    return lax.shift_right_logical(x, shift)


def _inproj_kernel(x_ref, g_ref, w_ref, o_ref, h_ref):
    @pl.when(pl.program_id(1) == 0)
    def _():
        x = x_ref[...]
        ms = jnp.mean(x * x, axis=-1, keepdims=True)
        h_ref[...] = (x * lax.rsqrt(ms + EPS) * g_ref[...]).astype(BF16)

    h = h_ref[...]
    tn = o_ref.shape[1]
    for c0 in range(0, tn, INPROJ_CHUNK):
        c1 = min(c0 + INPROJ_CHUNK, tn)
        o_ref[:, c0:c1] = _dot(h, w_ref[:, c0:c1]).astype(o_ref.dtype)


def _inproj(x2, g, w, out_dtype, tn):
    n, d = x2.shape
    cols = w.shape[1]
    tm = min(n, 1024)
    return pl.pallas_call(
        _inproj_kernel,
        grid=(n // tm, cols // tn),
        in_specs=[
            pl.BlockSpec((tm, d), lambda i, j: (i, 0)),
            pl.BlockSpec((1, d), lambda i, j: (0, 0)),
            pl.BlockSpec((d, tn), lambda i, j: (0, j)),
        ],
        out_specs=pl.BlockSpec((tm, tn), lambda i, j: (i, j)),
        out_shape=jax.ShapeDtypeStruct((n, cols), out_dtype),
        scratch_shapes=[pltpu.VMEM((tm, d), BF16)],
        compiler_params=_cparams("parallel", "arbitrary"),
        name="inproj_" + jnp.dtype(out_dtype).name,
    )(x2, g, w)


KV_COPY_ROWS = 32


def _kv_kernel(layer_ref, x_ref, g_ref, wk_ref, wv_ref, kin_ref, vin_ref, k_ref, v_ref, kb_ref, vb_ref,
               ks_ref, vs_ref):
    del layer_ref, kin_ref, vin_ref
    x = x_ref[...]
    ms = jnp.mean(x * x, axis=-1, keepdims=True)
    h = (x * lax.rsqrt(ms + EPS) * g_ref[...]).astype(BF16)
    k = _dot(h, wk_ref[...])
    v = _dot(h, wv_ref[...])
    tm, n_h, dh = k_ref.shape
    for src, stage, dst in ((k, ks_ref, k_ref), (v, vs_ref, v_ref)):
        for c in range(src.shape[1] // LANES):
            blk = src[:, c * LANES:(c + 1) * LANES]
            stage[pl.ds(2 * c, tm, stride=n_h), :] = blk
            stage[pl.ds(2 * c + 1, tm, stride=n_h), :] = pltpu.roll(blk, dh, axis=1)

        def copy_out(s, carry, stage=stage, dst=dst):
            r0 = pl.multiple_of(s * KV_COPY_ROWS, KV_COPY_ROWS)
            rows = stage[pl.ds(r0 * n_h, KV_COPY_ROWS * n_h), :]
            dst[pl.ds(r0, KV_COPY_ROWS)] = rows.reshape(KV_COPY_ROWS, n_h, LANES)[:, :, :dh]
            return carry

        lax.fori_loop(0, tm // KV_COPY_ROWS, copy_out, 0)
    kb_ref[...] = k.astype(BF16)
    vb_ref[...] = v.astype(BF16)


def _kvproj(x2, g, wk, wv, k_stack, v_stack, layer):
    n, d = x2.shape
    w = wk.shape[1]
    tm = min(n, 1024)
    stack_blk = pl.BlockSpec((None, tm, SB_H, SB_DH), lambda i, l: (l[0], i, 0, 0))
    grid_spec = pltpu.PrefetchScalarGridSpec(
        num_scalar_prefetch=1,
        grid=(n // tm,),
        in_specs=[
            pl.BlockSpec((tm, d), lambda i, l: (i, 0)),
            pl.BlockSpec((1, d), lambda i, l: (0, 0)),
            pl.BlockSpec((d, w), lambda i, l: (0, 0)),
            pl.BlockSpec((d, w), lambda i, l: (0, 0)),
            pl.BlockSpec(memory_space=pl.ANY),
            pl.BlockSpec(memory_space=pl.ANY),
        ],
        out_specs=[stack_blk, stack_blk] + [pl.BlockSpec((tm, w), lambda i, l: (i, 0))] * 2,
        scratch_shapes=[pltpu.VMEM((tm * SB_H, LANES), F32)] * 2,
    )
    return pl.pallas_call(
        _kv_kernel,
        grid_spec=grid_spec,
        out_shape=[jax.ShapeDtypeStruct(k_stack.shape, F32), jax.ShapeDtypeStruct(v_stack.shape, F32)]
        + [jax.ShapeDtypeStruct((n, w), BF16)] * 2,
        input_output_aliases={5: 0, 6: 1},
        compiler_params=_cparams("arbitrary"),
        name="kvproj",
    )(jnp.reshape(layer, (1,)).astype(jnp.int32), x2, g, wk, wv, k_stack, v_stack)


def _conv_taps(xe_ref, w_ref, b_ref, base, rows, hist_off, k):
    acc = None
    for j in range(k):
        term = w_ref[j:j + 1, :] * xe_ref[pl.ds(base + hist_off + j, rows), :]
        acc = term if acc is None else acc + term
    return acc + b_ref[...]


LRU_HIST = 8


def _lru_kernel(x_ref, g_ref, cs_ref, h0_ref, cw_ref, cb_ref, wa_ref, ba_ref, wx_ref, bx_ref, lam_ref,
                u_ref, hl_ref, cn_ref, xe_ref, a_ref, b_ref, h_ref):
    tt, w = x_ref.shape
    kc = cw_ref.shape[0]
    hist = LRU_HIST - (kc - 1)

    @pl.when(pl.program_id(1) == 0)
    def _():
        xe_ref[hist:LRU_HIST, :] = cs_ref[...]
        h_ref[...] = h0_ref[...]

    xe_ref[LRU_HIST:LRU_HIST + tt, :] = x_ref[...].astype(F32)
    xc = _conv_taps(xe_ref, cw_ref, cb_ref, 0, tt, hist, kc)
    tail = xe_ref[tt + hist:tt + LRU_HIST, :]
    xe_ref[hist:LRU_HIST, :] = tail
    cn_ref[...] = tail

    xcb = xc.astype(BF16)
    gate_a = _dot(xcb, wa_ref[...]) + ba_ref[...]
    gate_x = _dot(xcb, wx_ref[...]) + bx_ref[...]
    r = _sigmoid(gate_a)
    i = _sigmoid(gate_x)
    log_a = (-LRU_C) * r * _softplus(-lam_ref[...])
    a = jnp.exp(log_a)
    one_m_a2 = (1.0 - a) * (1.0 + a)
    a_ref[...] = a
    b_ref[...] = jnp.sqrt(one_m_a2) * (i * xc)

    row = _iota((SUBLANES, w), 0)

    def body(s, h):
        off = pl.multiple_of(s * SUBLANES, SUBLANES)
        a8 = a_ref[pl.ds(off, SUBLANES), :]
        b8 = b_ref[pl.ds(off, SUBLANES), :]
        for d in (1, 2, 4):
            keep = row >= d
            a_sh = pltpu.roll(a8, d, axis=0)
            b_sh = pltpu.roll(b8, d, axis=0)
            b8 = jnp.where(keep, a8 * b_sh + b8, b8)
            a8 = jnp.where(keep, a8 * a_sh, a8)
        h8 = a8 * h + b8
        b_ref[pl.ds(off, SUBLANES), :] = h8
        return h8[SUBLANES - 1:SUBLANES, :]

    h_last = lax.fori_loop(0, tt // SUBLANES, body, h_ref[...])
    h_ref[...] = h_last
    hl_ref[...] = h_last
    u_ref[...] = (b_ref[...] * _silu(g_ref[...].astype(F32))).astype(u_ref.dtype)


def _lru_branch(p3, conv_state, h0, pw, tt):
    b, t, _ = p3.shape
    w = conv_state.shape[-1]
    kc = pw["lru_conv_w"].shape[0]
    col = lambda off: off // w
    full = lambda shape: pl.BlockSpec(shape, lambda bi, ti: (0,) * len(shape))
    return pl.pallas_call(
        _lru_kernel,
        grid=(b, t // tt),
        in_specs=[
            pl.BlockSpec((None, tt, w), lambda bi, ti: (bi, ti, col(COL_LRU_X))),
            pl.BlockSpec((None, tt, w), lambda bi, ti: (bi, ti, col(COL_LRU_G))),
            pl.BlockSpec((None, kc - 1, w), lambda bi, ti: (bi, 0, 0)),
            pl.BlockSpec((None, 1, w), lambda bi, ti: (bi, 0, 0)),
            full((kc, w)), full((1, w)), full((w, w)), full((1, w)), full((w, w)), full((1, w)), full((1, w)),
        ],
        out_specs=[
            pl.BlockSpec((None, tt, w), lambda bi, ti: (bi, ti, 0)),
            pl.BlockSpec((None, 1, w), lambda bi, ti: (bi, 0, 0)),
            pl.BlockSpec((None, kc - 1, w), lambda bi, ti: (bi, 0, 0)),
        ],
        out_shape=[
            jax.ShapeDtypeStruct((b, t, w), BF16),
            jax.ShapeDtypeStruct((b, 1, w), F32),
            jax.ShapeDtypeStruct((b, kc - 1, w), F32),
        ],
        scratch_shapes=[
            pltpu.VMEM((tt + LRU_HIST, w), F32),
            pltpu.VMEM((tt, w), F32),
            pltpu.VMEM((tt, w), F32),
            pltpu.VMEM((1, w), F32),
        ],
        compiler_params=_cparams("parallel", "arbitrary"),
        name="lru_branch",
    )(p3, p3, conv_state, h0, pw["lru_conv_w"], pw["lru_conv_b"], pw["lru_wa"], pw["lru_ba"],
      pw["lru_wx"], pw["lru_bx"], pw["lru_lambda"])


def _ssd_kernel(z_ref, xbc_ref, dt_ref, cs_ref, s0_ref, cw_ref, cb_ref, dtb_ref, alog_ref, dvec_ref, nw_ref,
                u_ref, sn_ref, cn_ref, xe_ref, st_ref):
    L = z_ref.shape[0]
    inner = z_ref.shape[1]
    kc = cw_ref.shape[0]
    hist = LRU_HIST - (kc - 1)
    gw = SSD_R * SSD_P

    @pl.when(pl.program_id(1) == 0)
    def _():
        xe_ref[hist:LRU_HIST, :] = cs_ref[...]
        st_ref[...] = s0_ref[...]

    xe_ref[LRU_HIST:LRU_HIST + L, :] = xbc_ref[...].astype(F32)
    xa = _silu(_conv_taps(xe_ref, cw_ref, cb_ref, 0, L, hist, kc))
    tail = xe_ref[L + hist:L + LRU_HIST, :]
    xe_ref[hist:LRU_HIST, :] = tail
    cn_ref[...] = tail

    x = xa[:, :inner]
    bm = xa[:, inner:inner + SSD_G * SSD_N]
    cm = xa[:, inner + SSD_G * SSD_N:]

    dt = _softplus(dt_ref[...].astype(F32) + dtb_ref[...])
    adt = dt * (-jnp.exp(alog_ref[...]))

    r_i = _iota((L, L), 0)
    c_i = _iota((L, L), 1)
    causal = r_i >= c_i
    tril = causal.astype(BF16)
    eye = (_iota((LANES, LANES), 0) == _iota((LANES, LANES), 1)).astype(BF16)
    eye_n = eye[:SSD_N, :SSD_N]
    expand_c = (_div_pow2(_iota((LANES, inner), 1), SSD_P) == _iota((LANES, inner), 0)).astype(BF16)
    n_heads = inner // SSD_P
    expand_l = (_div_pow2(_iota((LANES, n_heads * L), 1), L) == _iota((LANES, n_heads * L), 0)).astype(BF16)

    acs = _sel_left(tril, adt)
    acs_t = _sel_nt(eye, acs)
    dt_c = _sel_right(dt, expand_c)
    acs_c = _sel_right(acs, expand_c)
    acs_w = _sel_right(acs, expand_l)

    xdt = x * dt_c
    last = acs_c[L - 1:L, :]
    xs = xdt * jnp.exp(last - acs_c)
    e_acs = jnp.exp(acs_c)
    chunk_decay = jnp.exp(last)
    lane = _iota((L, gw), 1)

    ys = []
    for g in range(SSD_G):
        cg = cm[:, g * SSD_N:(g + 1) * SSD_N].astype(BF16)
        bg = bm[:, g * SSD_N:(g + 1) * SSD_N].astype(BF16)
        cb = _dot_nt(cg, bg)
        ms = []
        for r in range(SSD_R):
            h = g * SSD_R + r
            seg = acs_w[:, h * L:(h + 1) * L] - acs_t[h:h + 1, :]
            dec = jnp.exp(jnp.where(causal, seg, NEG_BIG))
            ms.append((cb * dec).astype(BF16))
        mcat = jnp.concatenate(ms, axis=1)
        xg = xdt[:, g * gw:(g + 1) * gw]
        rhs = jnp.concatenate([jnp.where(lane < SSD_P, xg, 0.0), jnp.where(lane >= SSD_P, xg, 0.0)],
                              axis=0).astype(BF16)
        y_diag = _dot(mcat, rhs)
        st = st_ref[g]
        y_off = _dot(cg, st.astype(BF16)) * e_acs[:, g * gw:(g + 1) * gw]
        bg_t = _dot_nt(eye_n, bg).astype(BF16)
        st_ref[g] = st * chunk_decay[:, g * gw:(g + 1) * gw] + _dot(bg_t, xs[:, g * gw:(g + 1) * gw].astype(BF16))
        ys.append(y_diag + y_off)
    y = jnp.concatenate(ys, axis=1) + dvec_ref[...] * x
    sn_ref[...] = st_ref[...]

    gated = y * _silu(z_ref[...].astype(F32))
    ms2 = jnp.mean(gated * gated, axis=-1, keepdims=True)
    u_ref[...] = (gated * lax.rsqrt(ms2 + EPS) * nw_ref[...]).astype(u_ref.dtype)


def _ssd_branch(p3, conv_state, st0, pw, L):
    b, t, _ = p3.shape
    cdim = conv_state.shape[-1]
    inner = pw["ssd_norm"].shape[-1]
    kc = pw["ssd_conv_w"].shape[0]
    gw = SSD_R * SSD_P
    full = lambda shape: pl.BlockSpec(shape, lambda bi, ti: (0,) * len(shape))
    return pl.pallas_call(
        _ssd_kernel,
        grid=(b, t // L),
        in_specs=[
            pl.BlockSpec((None, L, inner), lambda bi, ti: (bi, ti, COL_SSD_Z // inner)),
            pl.BlockSpec((None, L, cdim), lambda bi, ti: (bi, ti, COL_XBC // cdim)),
            pl.BlockSpec((None, L, LANES), lambda bi, ti: (bi, ti, COL_DT // LANES)),
            pl.BlockSpec((None, kc - 1, cdim), lambda bi, ti: (bi, 0, 0)),
            pl.BlockSpec((None, SSD_G, SSD_N, gw), lambda bi, ti: (bi, 0, 0, 0)),
            full((kc, cdim)), full((1, cdim)), full((1, LANES)), full((1, LANES)), full((1, inner)), full((1, inner)),
        ],
        out_specs=[
            pl.BlockSpec((None, L, inner), lambda bi, ti: (bi, ti, 0)),
            pl.BlockSpec((None, SSD_G, SSD_N, gw), lambda bi, ti: (bi, 0, 0, 0)),
            pl.BlockSpec((None, kc - 1, cdim), lambda bi, ti: (bi, 0, 0)),
        ],
        out_shape=[
            jax.ShapeDtypeStruct((b, t, inner), BF16),
            jax.ShapeDtypeStruct((b, SSD_G, SSD_N, gw), F32),
            jax.ShapeDtypeStruct((b, kc - 1, cdim), F32),
        ],
        scratch_shapes=[
            pltpu.VMEM((L + LRU_HIST, cdim), F32),
            pltpu.VMEM((SSD_G, SSD_N, gw), F32),
        ],
        compiler_params=_cparams("parallel", "arbitrary"),
        name="ssd_branch",
    )(p3, p3, p3, conv_state, st0, pw["ssd_conv_w"], pw["ssd_conv_b"], pw["ssd_dt_bias"], pw["ssd_a_log"],
      pw["ssd_d"], pw["ssd_norm"])


CF_HIST = 32
CF_ROWS = 64


def _cf_kernel(in_ref, g_ref, cs_ref, cw_ref, cb_ref, lg_ref, lb_ref, u_ref, cn_ref, xe_ref):
    tt = in_ref.shape[0]
    w = g_ref.shape[1]
    kc = cw_ref.shape[0]
    hist = CF_HIST - (kc - 1)

    @pl.when(pl.program_id(1) == 0)
    def _():
        xe_ref[hist:CF_HIST, :] = cs_ref[...]

    cf = in_ref[...].astype(F32)
    xe_ref[CF_HIST:CF_HIST + tt, :] = cf[:, :w] * _sigmoid(cf[:, w:])

    def body(s, carry):
        base = pl.multiple_of(s * CF_ROWS, CF_ROWS)
        c = None
        for r in range(SUBLANES):
            rows = CF_ROWS + (SUBLANES if r else 0)
            part = None
            for o in range(hist + (r - hist) % SUBLANES, hist + kc, SUBLANES):
                term = cw_ref[o - hist:o - hist + 1, :] * xe_ref[pl.ds(base + (o - r), rows), :]
                part = term if part is None else part + term
            piece = part[r:r + CF_ROWS, :]
            c = piece if c is None else c + piece
        c = c + cb_ref[...]
        mu = jnp.mean(c, axis=-1, keepdims=True)
        cc = c - mu
        var = jnp.mean(cc * cc, axis=-1, keepdims=True)
        ln = cc * lax.rsqrt(var + EPS) * lg_ref[...] + lb_ref[...]
        gate = g_ref[pl.ds(base, CF_ROWS), :].astype(F32)
        u_ref[pl.ds(base, CF_ROWS), :] = (_silu(ln) * _silu(gate)).astype(u_ref.dtype)
        return carry

    lax.fori_loop(0, tt // CF_ROWS, body, 0)
    tail = xe_ref[tt + hist:tt + CF_HIST, :]
    xe_ref[hist:CF_HIST, :] = tail
    cn_ref[...] = tail


def _cf_branch(p3, conv_state, pw, tt):
    b, t, _ = p3.shape
    w = conv_state.shape[-1]
    kc = pw["cf_conv_w"].shape[0]
    full = lambda shape: pl.BlockSpec(shape, lambda bi, ti: (0,) * len(shape))
    return pl.pallas_call(
        _cf_kernel,
        grid=(b, t // tt),
        in_specs=[
            pl.BlockSpec((None, tt, 2 * w), lambda bi, ti: (bi, ti, COL_CF_IN // (2 * w))),
            pl.BlockSpec((None, tt, w), lambda bi, ti: (bi, ti, COL_CF_G // w)),
            pl.BlockSpec((None, kc - 1, w), lambda bi, ti: (bi, 0, 0)),
            full((kc, w)), full((1, w)), full((1, w)), full((1, w)),
        ],
        out_specs=[
            pl.BlockSpec((None, tt, w), lambda bi, ti: (bi, ti, 0)),
            pl.BlockSpec((None, kc - 1, w), lambda bi, ti: (bi, 0, 0)),
        ],
        out_shape=[
            jax.ShapeDtypeStruct((b, t, w), BF16),
            jax.ShapeDtypeStruct((b, kc - 1, w), F32),
        ],
        scratch_shapes=[pltpu.VMEM((tt + CF_HIST, w), F32)],
        compiler_params=_cparams("parallel", "arbitrary"),
        name="cf_branch",
    )(p3, p3, conv_state, pw["cf_conv_w"], pw["cf_conv_b"], pw["cf_ln_g"], pw["cf_ln_b"])


def _sb_kernel(q_ref, k_ref, v_ref, g_ref, u_ref, acc_ref, out_ref, *, diag0):
    tq, w = q_ref.shape
    tk = tq
    n_pairs = w // LANES
    per_pair = LANES // SB_DH
    kd = pl.program_id(1) + diag0

    r_i = _iota((tq, tk), 0)
    c_i = _iota((tq, tk), 1)
    strictly_earlier = c_i < r_i
    later = (r_i > c_i).astype(BF16)
    ones = jnp.ones((tk, tk), BF16)
    top = jnp.concatenate([later, ones], axis=1)
    sums_rhs = jnp.concatenate([top, top], axis=0)

    head_of_lane = _div_pow2(_iota((1, LANES), 1), SB_DH)
    lane_masks = [(head_of_lane == r).astype(BF16) for r in range(per_pair)]
    q_masks = [m * (SB_DH ** -0.5) for m in lane_masks]
    qm = []
    for p in range(n_pairs):
        qp = q_ref[:, p * LANES:(p + 1) * LANES]
        qm.append(jnp.concatenate([qp * m for m in q_masks], axis=0))
    earlier_stacked = jnp.concatenate([strictly_earlier] * per_pair, axis=0)

    acc_ref[...] = jnp.zeros_like(acc_ref)
    out_ref[...] = jnp.zeros_like(out_ref)

    def block(kb, masked):
        off = pl.multiple_of(kb * tk, tk)
        zs, lsms, hls = [], [], []
        for p in range(n_pairs):
            kp = k_ref[pl.ds(off, tk), p * LANES:(p + 1) * LANES]
            z = _dot_nt(qm[p], kp)
            lsm = -_softplus(z)
            if masked:
                lsm = jnp.where(earlier_stacked, lsm, 0.0)
            hi = lsm.astype(BF16)
            lo = (lsm - hi.astype(F32)).astype(BF16)
            zs.append(z)
            lsms.append(lsm)
            hls.append(jnp.concatenate([hi, lo], axis=1))
        sums = _dot(jnp.concatenate(hls, axis=0), sums_rhs)
        worst = None
        for p in range(n_pairs):
            rows = slice(p * per_pair * tq, (p + 1) * per_pair * tq)
            acc = acc_ref[p]
            lw = zs[p] + lsms[p] + sums[rows, :tk] + acc
            if masked:
                lw = jnp.where(earlier_stacked, lw, NEG_BIG)
            wgt = jnp.exp(lw).astype(BF16)
            acc = acc + sums[rows, tk:]
            acc_ref[p] = acc
            worst = acc if worst is None else jnp.maximum(worst, acc)
            vp = v_ref[pl.ds(off, tk), p * LANES:(p + 1) * LANES]
            v2 = jnp.concatenate([vp * m for m in lane_masks], axis=0)
            wcat = jnp.concatenate([wgt[r * tq:(r + 1) * tq] for r in range(per_pair)], axis=1)
            out_ref[p] = out_ref[p] + _dot(wcat, v2)
        return jnp.max(worst)

    worst0 = block(kd, True)

    def cond(c):
        kb, worst = c
        return jnp.logical_and(kb >= 0, worst >= F32_EXP_ZERO_BELOW)

    def body(c):
        kb, _ = c
        return kb - 1, block(kb, False)

    lax.while_loop(cond, body, (kd - 1, worst0))

    o = jnp.concatenate([out_ref[p] for p in range(n_pairs)], axis=1)
    u_ref[...] = (o * _silu(g_ref[...].astype(F32))).astype(u_ref.dtype)


def _sb_branch(p3, k_all, v_all, tq):
    b, t, _ = p3.shape
    tk_total = k_all.shape[1]
    w = k_all.shape[2]
    diag0 = (tk_total - t) // tq
    return pl.pallas_call(
        functools.partial(_sb_kernel, diag0=diag0),
        grid=(b, t // tq),
        in_specs=[
            pl.BlockSpec((None, tq, w), lambda bi, qi: (bi, qi, COL_Q // w)),
            pl.BlockSpec((None, tk_total, w), lambda bi, qi: (bi, 0, 0)),
            pl.BlockSpec((None, tk_total, w), lambda bi, qi: (bi, 0, 0)),
            pl.BlockSpec((None, tq, w), lambda bi, qi: (bi, qi, COL_SB_G // w)),
        ],
        out_specs=pl.BlockSpec((None, tq, w), lambda bi, qi: (bi, qi, 0)),
        out_shape=jax.ShapeDtypeStruct((b, t, w), BF16),
        scratch_shapes=[
            pltpu.VMEM((w // LANES, (LANES // SB_DH) * tq, tq), F32),
            pltpu.VMEM((w // LANES, tq, LANES), F32),
        ],
        compiler_params=_cparams("parallel", "arbitrary"),
        name="sb_branch",
    )(p3, k_all, v_all, p3)


def _mix_kernel(ua_ref, ub_ref, uc_ref, ud_ref, m0_ref, m1_ref, m2_ref, m3_ref, x_ref, wd_ref, wo_ref, g_ref, o_ref):
    us = (ua_ref, ub_ref, uc_ref, ud_ref)
    ms = (m0_ref, m1_ref, m2_ref, m3_ref)
    mixed = None
    for i in range(N_BRANCH):
        y = _dot(us[i][...], wd_ref[i])
        term = _sigmoid(ms[i][...].astype(F32)) * y
        mixed = term if mixed is None else mixed + term
    o = _dot(mixed.astype(BF16), wo_ref[...])
    ms2 = jnp.mean(o * o, axis=-1, keepdims=True)
    o_ref[...] = x_ref[...] + o * lax.rsqrt(ms2 + EPS) * g_ref[...]


def _mix(us, p2, x2, pw):
    n, d = x2.shape
    w = us[0].shape[-1]
    tm = min(n, 512)
    ublk = pl.BlockSpec((tm, w), lambda i: (i, 0))
    mblk = lambda j: pl.BlockSpec((tm, d), lambda i: (i, COL_MERGE // d + j))
    return pl.pallas_call(
        _mix_kernel,
        grid=(n // tm,),
        in_specs=[ublk] * 4 + [mblk(j) for j in range(N_BRANCH)] + [
            pl.BlockSpec((tm, d), lambda i: (i, 0)),
            pl.BlockSpec((N_BRANCH, w, d), lambda i: (0, 0, 0)),
            pl.BlockSpec((d, d), lambda i: (0, 0)),
            pl.BlockSpec((1, d), lambda i: (0, 0)),
        ],
        out_specs=pl.BlockSpec((tm, d), lambda i: (i, 0)),
        out_shape=jax.ShapeDtypeStruct((n, d), F32),
        input_output_aliases={len(us) + N_BRANCH: 0},
        compiler_params=_cparams("parallel"),
        name="mix",
    )(*us, p2, p2, p2, p2, x2, pw["w_down"], pw["w_out"], pw["norm_post"])


def _block_diag(wb):
    n, c, d = wb.shape
    eye = jnp.eye(n, dtype=wb.dtype)
    return (eye[:, None, :, None] * wb[:, :, None, :]).reshape(n * c, n * d)


def _prep_layer(p):
    d_model = p["w_in"].shape[0]
    bw = p["lru_lambda"].shape[-1]
    cdim = p["ssd_conv_b"].shape[-1]
    n_heads = p["ssd_dt_bias"].shape[-1]
    sizes = (bw, bw, bw, cdim, n_heads, 2 * bw, bw, 3 * bw, bw, N_BRANCH * d_model)
    offs = [0]
    for s in sizes:
        offs.append(offs[-1] + s)
    seg = lambda i: p["w_in"][:, offs[i]:offs[i + 1]]
    lru_x, lru_g, ssd_z, xbc, dt, cf_in, cf_g, qkv, sb_g, merge = [seg(i) for i in range(len(sizes))]
    q, k, v = qkv[:, :bw], qkv[:, bw:2 * bw], qkv[:, 2 * bw:]
    zeros = lambda c: jnp.zeros((d_model, c), p["w_in"].dtype)
    main = jnp.concatenate(
        [xbc, cf_in, lru_x, lru_g, ssd_z, cf_g, q, sb_g, merge, dt, zeros(LANES - n_heads),
         zeros(P_COLS - COL_DT - LANES)], axis=1)
    assert main.shape[1] == P_COLS
    pad_h = lambda a, fill: jnp.concatenate([a, jnp.full((LANES - n_heads,), fill, a.dtype)])[None, :]
    row = lambda a: a[None, :].astype(F32)
    return dict(
        w_main=main.astype(BF16), w_k=k.astype(BF16), w_v=v.astype(BF16),
        norm_pre=row(p["norm_pre"]), norm_post=row(p["norm_post"]),
        lru_conv_w=p["lru_conv_w"], lru_conv_b=row(p["lru_conv_b"]),
        lru_wa=_block_diag(p["lru_wa"]).astype(BF16), lru_ba=row(p["lru_ba"]),
        lru_wx=_block_diag(p["lru_wx"]).astype(BF16), lru_bx=row(p["lru_bx"]),
        lru_lambda=row(p["lru_lambda"]),
        ssd_conv_w=p["ssd_conv_w"], ssd_conv_b=row(p["ssd_conv_b"]),
        ssd_dt_bias=pad_h(p["ssd_dt_bias"], 0.0), ssd_a_log=pad_h(p["ssd_a_log"], 0.0),
        ssd_d=jnp.repeat(p["ssd_d"], SSD_P)[None, :], ssd_norm=row(p["ssd_norm"]),
        cf_conv_w=p["cf_conv_w"], cf_conv_b=row(p["cf_conv_b"]),
        cf_ln_g=row(p["cf_ln_g"]), cf_ln_b=row(p["cf_ln_b"]),
        w_down=p["w_down"].astype(BF16), w_out=p["w_out"].astype(BF16),
    )


def _ssd_state_in(s):
    b = s.shape[0]
    s = s.reshape(b, SSD_G, SSD_R, SSD_P, SSD_N)
    return jnp.transpose(s, (0, 1, 4, 2, 3)).reshape(b, SSD_G, SSD_N, SSD_R * SSD_P)


def _ssd_state_out(s):
    b = s.shape[0]
    s = s.reshape(b, SSD_G, SSD_N, SSD_R, SSD_P)
    return jnp.transpose(s, (0, 1, 3, 4, 2)).reshape(b, SSD_G * SSD_R, SSD_P, SSD_N)


def _layer(x, pw, st, past_k, past_v, k_stack, v_stack, layer, *, seq_tile, ssd_chunk, sb_block):
    b, t, d = x.shape
    x2 = x.reshape(b * t, d)
    p2 = _inproj(x2, pw["norm_pre"], pw["w_main"], BF16, P_TILE_N)
    k_stack, v_stack, kb2, vb2 = _kvproj(x2, pw["norm_pre"], pw["w_k"], pw["w_v"], k_stack, v_stack, layer)
    p3 = p2.reshape(b, t, P_COLS)
    kb3 = kb2.reshape(b, t, -1)
    vb3 = vb2.reshape(b, t, -1)

    u_a, lru_h, lru_conv = _lru_branch(p3, st["lru_conv"], st["lru_h"][:, None, :], pw, seq_tile)
    u_b, ssd_new, ssd_conv = _ssd_branch(p3, st["ssd_conv"], _ssd_state_in(st["ssd"]), pw, ssd_chunk)
    u_c, cf_conv = _cf_branch(p3, st["cf_conv"], pw, seq_tile)
    if past_k is None:
        k_all, v_all = kb3, vb3
    else:
        k_all = jnp.concatenate([past_k.reshape(b, past_k.shape[1], -1).astype(BF16), kb3], axis=1)
        v_all = jnp.concatenate([past_v.reshape(b, past_v.shape[1], -1).astype(BF16), vb3], axis=1)
    u_d = _sb_branch(p3, k_all, v_all, sb_block)

    w = u_a.shape[-1]
    us = [u.reshape(b * t, w) for u in (u_a, u_b, u_c, u_d)]
    y = _mix(us, p2, x2, pw).reshape(b, t, d)
    new = dict(lru_h=lru_h[:, 0, :], lru_conv=lru_conv, ssd=_ssd_state_out(ssd_new), ssd_conv=ssd_conv,
               cf_conv=cf_conv)
    return y, new, k_stack, v_stack


_STATE_KEYS = ("lru_h", "lru_conv", "ssd", "ssd_conv", "cf_conv")


def kernel(x_prompt, x_sample, state_lru_h, state_lru_conv, state_ssd, state_ssd_conv, state_cf_conv, cache_sb_k, cache_sb_v, norm_pre, norm_post, w_in, lru_conv_w, lru_conv_b, lru_wa, lru_ba, lru_wx, lru_bx, lru_lambda, ssd_conv_w, ssd_conv_b, ssd_dt_bias, ssd_a_log, ssd_d, ssd_norm, cf_conv_w, cf_conv_b, cf_ln_g, cf_ln_b, w_down, w_out):
    raw = dict(norm_pre=norm_pre, norm_post=norm_post, w_in=w_in, lru_conv_w=lru_conv_w, lru_conv_b=lru_conv_b,
               lru_wa=lru_wa, lru_ba=lru_ba, lru_wx=lru_wx, lru_bx=lru_bx, lru_lambda=lru_lambda,
               ssd_conv_w=ssd_conv_w, ssd_conv_b=ssd_conv_b, ssd_dt_bias=ssd_dt_bias, ssd_a_log=ssd_a_log,
               ssd_d=ssd_d, ssd_norm=ssd_norm, cf_conv_w=cf_conv_w, cf_conv_b=cf_conv_b, cf_ln_g=cf_ln_g,
               cf_ln_b=cf_ln_b, w_down=w_down, w_out=w_out)
    pws = jax.vmap(_prep_layer)(raw)

    bp = x_prompt.shape[0]
    dt = x_prompt.dtype
    bw = state_lru_h.shape[-1]
    zero_state = dict(lru_h=jnp.zeros((bp, bw), dt),
                      lru_conv=jnp.zeros((bp,) + state_lru_conv.shape[2:], dt),
                      ssd=jnp.zeros((bp,) + state_ssd.shape[2:], dt),
                      ssd_conv=jnp.zeros((bp,) + state_ssd_conv.shape[2:], dt),
                      cf_conv=jnp.zeros((bp,) + state_cf_conv.shape[2:], dt))

    depth = w_in.shape[0]
    layers = jnp.arange(depth, dtype=jnp.int32)

    def kv_stacks(x):
        shape = (depth, x.shape[0] * x.shape[1], SB_H, SB_DH)
        return jnp.zeros(shape, F32), jnp.zeros(shape, F32)

    def prompt_step(carry, xs):
        x, k_stack, v_stack = carry
        pw, layer = xs
        y, new, k_stack, v_stack = _layer(x, pw, zero_state, None, None, k_stack, v_stack, layer,
                                          seq_tile=512, ssd_chunk=128, sb_block=128)
        return (y, k_stack, v_stack), tuple(new[k] for k in _STATE_KEYS)

    (y_p, k_p, v_p), p_new = lax.scan(prompt_step, (x_prompt,) + kv_stacks(x_prompt), (pws, layers))

    t_s = x_sample.shape[1]

    def sample_step(carry, xs):
        x, k_stack, v_stack = carry
        pw, st, pk, pv, layer = xs
        y, new, k_stack, v_stack = _layer(x, pw, st, pk, pv, k_stack, v_stack, layer,
                                          seq_tile=t_s, ssd_chunk=t_s, sb_block=t_s)
        return (y, k_stack, v_stack), tuple(new[k] for k in _STATE_KEYS)

    st_s = dict(lru_h=state_lru_h, lru_conv=state_lru_conv, ssd=state_ssd, ssd_conv=state_ssd_conv,
                cf_conv=state_cf_conv)
    (y_s, k_s, v_s), s_new = lax.scan(sample_step, (x_sample,) + kv_stacks(x_sample),
                                      (pws, st_s, cache_sb_k, cache_sb_v, layers))

    outs = [y_p, y_s]
    for i in range(len(_STATE_KEYS)):
        outs += [p_new[i], s_new[i]]
    kv_shape = lambda x: (depth,) + x.shape[:2] + (SB_H, SB_DH)
    outs += [k_p.reshape(kv_shape(x_prompt)), k_s.reshape(kv_shape(x_sample)),
             v_p.reshape(kv_shape(x_prompt)), v_s.reshape(kv_shape(x_sample))]
    return tuple(outs)
```

```python
import functools

import jax
import jax.numpy as jnp
from jax import lax
from jax.experimental import pallas as pl
from jax.experimental.pallas import tpu as pltpu

F32 = jnp.float32
BF16 = jnp.bfloat16

EPS = 1e-6
LRU_C = 8.0
N_BRANCH = 4
LRU_BLOCKS = 8
SSD_P = 64
SSD_N = 64
SSD_G = 4
SSD_R = 2
SB_H = 8
SB_DH = 64
SB_KEY_BLOCK = 128

LANES = 128
SUBLANES = 8
VMEM_LIMIT_BYTES = 56 * 1024 * 1024

F32_EXP_ZERO_BELOW = -104.0
NEG_BIG = -1e30

COL_XBC = 0
COL_CF_IN = 1024
COL_LRU_X = 2048
COL_LRU_G = 2560
COL_SSD_Z = 3072
COL_CF_G = 3584
COL_Q = 4096
COL_SB_G = 4608
COL_MERGE = 5120
COL_DT = 9216
P_COLS = 9472
P_TILE_N = P_COLS // 2
INPROJ_CHUNK = 512


def _cparams(*sem):
    return pltpu.CompilerParams(dimension_semantics=sem, vmem_limit_bytes=VMEM_LIMIT_BYTES)


def _sigmoid(x):
    return 0.5 * jnp.tanh(0.5 * x) + 0.5


def _silu(x):
    return x * _sigmoid(x)


def _softplus(x):
    return jnp.maximum(x, 0.0) + jnp.log(1.0 + jnp.exp(-jnp.abs(x)))


def _split3(x):
    hi = x.astype(BF16)
    r = x - hi.astype(F32)
    mid = r.astype(BF16)
    lo = (r - mid.astype(F32)).astype(BF16)
    return hi, mid, lo


def _dot(a, b):
    return jnp.dot(a, b, preferred_element_type=F32)


def _dot_nt(a, b):
    return lax.dot_general(a, b, (((1,), (1,)), ((), ())), preferred_element_type=F32)


def _sel_left(m01, x):
    hi, mid, lo = _split3(x)
    return _dot(m01, hi) + _dot(m01, mid) + _dot(m01, lo)


def _sel_right(x, m01):
    hi, mid, lo = _split3(x)
    return _dot(hi, m01) + _dot(mid, m01) + _dot(lo, m01)


def _sel_nt(m01, x):
    hi, mid, lo = _split3(x)
    return _dot_nt(m01, hi) + _dot_nt(m01, mid) + _dot_nt(m01, lo)


def _iota(shape, dim):
    return lax.broadcasted_iota(jnp.int32, shape, dim)


def _div_pow2(x, n):
    shift = n.bit_length() - 1
    assert 1 << shift == n
    return lax.shift_right_logical(x, shift)


def _inproj_kernel(x_ref, g_ref, w_ref, o_ref, h_ref):
    @pl.when(pl.program_id(1) == 0)
    def _():
        x = x_ref[...]
        ms = jnp.mean(x * x, axis=-1, keepdims=True)
        h_ref[...] = (x * lax.rsqrt(ms + EPS) * g_ref[...]).astype(BF16)

    h = h_ref[...]
    tn = o_ref.shape[1]
    for c0 in range(0, tn, INPROJ_CHUNK):
        c1 = min(c0 + INPROJ_CHUNK, tn)
        o_ref[:, c0:c1] = _dot(h, w_ref[:, c0:c1]).astype(o_ref.dtype)


def _inproj(x2, g, w, out_dtype, tn):
    n, d = x2.shape
    cols = w.shape[1]
    tm = min(n, 1024)
    return pl.pallas_call(
        _inproj_kernel,
        grid=(n // tm, cols // tn),
        in_specs=[
            pl.BlockSpec((tm, d), lambda i, j: (i, 0)),
            pl.BlockSpec((1, d), lambda i, j: (0, 0)),
            pl.BlockSpec((d, tn), lambda i, j: (0, j)),
        ],
        out_specs=pl.BlockSpec((tm, tn), lambda i, j: (i, j)),
        out_shape=jax.ShapeDtypeStruct((n, cols), out_dtype),
        scratch_shapes=[pltpu.VMEM((tm, d), BF16)],
        compiler_params=_cparams("parallel", "arbitrary"),
        name="inproj_" + jnp.dtype(out_dtype).name,
    )(x2, g, w)


def _kv_kernel(layer_ref, x_ref, g_ref, wkt_ref, wvt_ref, wv_ref, kin_ref, vin_ref, k_ref, v_ref, kb_ref, vb_ref):
    del layer_ref, kin_ref, vin_ref
    x = x_ref[...]
    ms = jnp.mean(x * x, axis=-1, keepdims=True)
    h = (x * lax.rsqrt(ms + EPS) * g_ref[...]).astype(BF16)
    kt = _dot_nt(wkt_ref[...], h)
    k_ref[...] = kt
    v_ref[...] = _dot_nt(wvt_ref[...], h)
    kb_ref[...] = kt.astype(BF16)
    vb_ref[...] = _dot(h, wv_ref[...]).astype(BF16)


def _kvproj(x, g, wkt, wvt, wv, k_stack, v_stack, layer):
    b, t, d = x.shape
    w = wv.shape[1]
    tm = min(t, 1024)
    stack_blk = pl.BlockSpec((None, None, w, tm), lambda bi, ti, l: (l[0], bi, 0, ti))
    full = lambda shape: pl.BlockSpec(shape, lambda bi, ti, l: (0,) * len(shape))
    grid_spec = pltpu.PrefetchScalarGridSpec(
        num_scalar_prefetch=1,
        grid=(b, t // tm),
        in_specs=[
            pl.BlockSpec((None, tm, d), lambda bi, ti, l: (bi, ti, 0)),
            full((1, d)), full((w, d)), full((w, d)), full((d, w)),
            pl.BlockSpec(memory_space=pl.ANY),
            pl.BlockSpec(memory_space=pl.ANY),
        ],
        out_specs=[stack_blk, stack_blk,
                   pl.BlockSpec((None, w, tm), lambda bi, ti, l: (bi, 0, ti)),
                   pl.BlockSpec((None, tm, w), lambda bi, ti, l: (bi, ti, 0))],
    )
    return pl.pallas_call(
        _kv_kernel,
        grid_spec=grid_spec,
        out_shape=[jax.ShapeDtypeStruct(k_stack.shape, F32), jax.ShapeDtypeStruct(v_stack.shape, F32),
                   jax.ShapeDtypeStruct((b, w, t), BF16), jax.ShapeDtypeStruct((b, t, w), BF16)],
        input_output_aliases={6: 0, 7: 1},
        compiler_params=_cparams("arbitrary", "arbitrary"),
        name="kvproj",
    )(jnp.reshape(layer, (1,)).astype(jnp.int32), x, g, wkt, wvt, wv, k_stack, v_stack)


def _conv_taps(xe_ref, w_ref, b_ref, base, rows, hist_off, k):
    acc = None
    for j in range(k):
        term = w_ref[j:j + 1, :] * xe_ref[pl.ds(base + hist_off + j, rows), :]
        acc = term if acc is None else acc + term
    return acc + b_ref[...]


LRU_HIST = 8


def _lru_kernel(x_ref, g_ref, cs_ref, h0_ref, cw_ref, cb_ref, wa_ref, ba_ref, wx_ref, bx_ref, lam_ref,
                u_ref, hl_ref, cn_ref, xe_ref, a_ref, b_ref, h_ref):
    tt, w = x_ref.shape
    kc = cw_ref.shape[0]
    hist = LRU_HIST - (kc - 1)

    @pl.when(pl.program_id(1) == 0)
    def _():
        xe_ref[hist:LRU_HIST, :] = cs_ref[...]
        h_ref[...] = h0_ref[...]

    xe_ref[LRU_HIST:LRU_HIST + tt, :] = x_ref[...].astype(F32)
    xc = _conv_taps(xe_ref, cw_ref, cb_ref, 0, tt, hist, kc)
    tail = xe_ref[tt + hist:tt + LRU_HIST, :]
    xe_ref[hist:LRU_HIST, :] = tail
    cn_ref[...] = tail

    xcb = xc.astype(BF16)
    gate_a = _dot(xcb, wa_ref[...]) + ba_ref[...]
    gate_x = _dot(xcb, wx_ref[...]) + bx_ref[...]
    r = _sigmoid(gate_a)
    i = _sigmoid(gate_x)
    log_a = (-LRU_C) * r * _softplus(-lam_ref[...])
    a = jnp.exp(log_a)
    one_m_a2 = (1.0 - a) * (1.0 + a)
    a_ref[...] = a
    b_ref[...] = jnp.sqrt(one_m_a2) * (i * xc)

    row = _iota((SUBLANES, w), 0)

    def body(s, h):
        off = pl.multiple_of(s * SUBLANES, SUBLANES)
        a8 = a_ref[pl.ds(off, SUBLANES), :]
        b8 = b_ref[pl.ds(off, SUBLANES), :]
        for d in (1, 2, 4):
            keep = row >= d
            a_sh = pltpu.roll(a8, d, axis=0)
            b_sh = pltpu.roll(b8, d, axis=0)
            b8 = jnp.where(keep, a8 * b_sh + b8, b8)
            a8 = jnp.where(keep, a8 * a_sh, a8)
        h8 = a8 * h + b8
        b_ref[pl.ds(off, SUBLANES), :] = h8
        return h8[SUBLANES - 1:SUBLANES, :]

    h_last = lax.fori_loop(0, tt // SUBLANES, body, h_ref[...])
    h_ref[...] = h_last
    hl_ref[...] = h_last
    u_ref[...] = (b_ref[...] * _silu(g_ref[...].astype(F32))).astype(u_ref.dtype)


def _lru_branch(p3, conv_state, h0, pw, tt):
    b, t, _ = p3.shape
    w = conv_state.shape[-1]
    kc = pw["lru_conv_w"].shape[0]
    col = lambda off: off // w
    full = lambda shape: pl.BlockSpec(shape, lambda bi, ti: (0,) * len(shape))
    return pl.pallas_call(
        _lru_kernel,
        grid=(b, t // tt),
        in_specs=[
            pl.BlockSpec((None, tt, w), lambda bi, ti: (bi, ti, col(COL_LRU_X))),
            pl.BlockSpec((None, tt, w), lambda bi, ti: (bi, ti, col(COL_LRU_G))),
            pl.BlockSpec((None, kc - 1, w), lambda bi, ti: (bi, 0, 0)),
            pl.BlockSpec((None, 1, w), lambda bi, ti: (bi, 0, 0)),
            full((kc, w)), full((1, w)), full((w, w)), full((1, w)), full((w, w)), full((1, w)), full((1, w)),
        ],
        out_specs=[
            pl.BlockSpec((None, tt, w), lambda bi, ti: (bi, ti, 0)),
            pl.BlockSpec((None, 1, w), lambda bi, ti: (bi, 0, 0)),
            pl.BlockSpec((None, kc - 1, w), lambda bi, ti: (bi, 0, 0)),
        ],
        out_shape=[
            jax.ShapeDtypeStruct((b, t, w), BF16),
            jax.ShapeDtypeStruct((b, 1, w), F32),
            jax.ShapeDtypeStruct((b, kc - 1, w), F32),
        ],
        scratch_shapes=[
            pltpu.VMEM((tt + LRU_HIST, w), F32),
            pltpu.VMEM((tt, w), F32),
            pltpu.VMEM((tt, w), F32),
            pltpu.VMEM((1, w), F32),
        ],
        compiler_params=_cparams("parallel", "arbitrary"),
        name="lru_branch",
    )(p3, p3, conv_state, h0, pw["lru_conv_w"], pw["lru_conv_b"], pw["lru_wa"], pw["lru_ba"],
      pw["lru_wx"], pw["lru_bx"], pw["lru_lambda"])


def _ssd_kernel(z_ref, xbc_ref, dt_ref, cs_ref, s0_ref, cw_ref, cb_ref, dtb_ref, alog_ref, dvec_ref, nw_ref,
                u_ref, sn_ref, cn_ref, xe_ref, st_ref):
    L = z_ref.shape[0]
    inner = z_ref.shape[1]
    kc = cw_ref.shape[0]
    hist = LRU_HIST - (kc - 1)
    gw = SSD_R * SSD_P

    @pl.when(pl.program_id(1) == 0)
    def _():
        xe_ref[hist:LRU_HIST, :] = cs_ref[...]
        st_ref[...] = s0_ref[...]

    xe_ref[LRU_HIST:LRU_HIST + L, :] = xbc_ref[...].astype(F32)
    xa = _silu(_conv_taps(xe_ref, cw_ref, cb_ref, 0, L, hist, kc))
    tail = xe_ref[L + hist:L + LRU_HIST, :]
    xe_ref[hist:LRU_HIST, :] = tail
    cn_ref[...] = tail

    x = xa[:, :inner]
    bm = xa[:, inner:inner + SSD_G * SSD_N]
    cm = xa[:, inner + SSD_G * SSD_N:]

    dt = _softplus(dt_ref[...].astype(F32) + dtb_ref[...])
    adt = dt * (-jnp.exp(alog_ref[...]))

    r_i = _iota((L, L), 0)
    c_i = _iota((L, L), 1)
    causal = r_i >= c_i
    tril = causal.astype(BF16)
    eye = (_iota((LANES, LANES), 0) == _iota((LANES, LANES), 1)).astype(BF16)
    eye_n = eye[:SSD_N, :SSD_N]
    expand_c = (_div_pow2(_iota((LANES, inner), 1), SSD_P) == _iota((LANES, inner), 0)).astype(BF16)
    n_heads = inner // SSD_P
    expand_l = (_div_pow2(_iota((LANES, n_heads * L), 1), L) == _iota((LANES, n_heads * L), 0)).astype(BF16)

    acs = _sel_left(tril, adt)
    acs_t = _sel_nt(eye, acs)
    dt_c = _sel_right(dt, expand_c)
    acs_c = _sel_right(acs, expand_c)
    acs_w = _sel_right(acs, expand_l)

    xdt = x * dt_c
    last = acs_c[L - 1:L, :]
    xs = xdt * jnp.exp(last - acs_c)
    e_acs = jnp.exp(acs_c)
    chunk_decay = jnp.exp(last)
    lane = _iota((L, gw), 1)

    ys = []
    for g in range(SSD_G):
        cg = cm[:, g * SSD_N:(g + 1) * SSD_N].astype(BF16)
        bg = bm[:, g * SSD_N:(g + 1) * SSD_N].astype(BF16)
        cb = _dot_nt(cg, bg)
        ms = []
        for r in range(SSD_R):
            h = g * SSD_R + r
            seg = acs_w[:, h * L:(h + 1) * L] - acs_t[h:h + 1, :]
            dec = jnp.exp(jnp.where(causal, seg, NEG_BIG))
            ms.append((cb * dec).astype(BF16))
        mcat = jnp.concatenate(ms, axis=1)
        xg = xdt[:, g * gw:(g + 1) * gw]
        rhs = jnp.concatenate([jnp.where(lane < SSD_P, xg, 0.0), jnp.where(lane >= SSD_P, xg, 0.0)],
                              axis=0).astype(BF16)
        y_diag = _dot(mcat, rhs)
        st = st_ref[g]
        y_off = _dot(cg, st.astype(BF16)) * e_acs[:, g * gw:(g + 1) * gw]
        bg_t = _dot_nt(eye_n, bg).astype(BF16)
        st_ref[g] = st * chunk_decay[:, g * gw:(g + 1) * gw] + _dot(bg_t, xs[:, g * gw:(g + 1) * gw].astype(BF16))
        ys.append(y_diag + y_off)
    y = jnp.concatenate(ys, axis=1) + dvec_ref[...] * x
    sn_ref[...] = st_ref[...]

    gated = y * _silu(z_ref[...].astype(F32))
    ms2 = jnp.mean(gated * gated, axis=-1, keepdims=True)
    u_ref[...] = (gated * lax.rsqrt(ms2 + EPS) * nw_ref[...]).astype(u_ref.dtype)


def _ssd_branch(p3, conv_state, st0, pw, L):
    b, t, _ = p3.shape
    cdim = conv_state.shape[-1]
    inner = pw["ssd_norm"].shape[-1]
    kc = pw["ssd_conv_w"].shape[0]
    gw = SSD_R * SSD_P
    full = lambda shape: pl.BlockSpec(shape, lambda bi, ti: (0,) * len(shape))
    return pl.pallas_call(
        _ssd_kernel,
        grid=(b, t // L),
        in_specs=[
            pl.BlockSpec((None, L, inner), lambda bi, ti: (bi, ti, COL_SSD_Z // inner)),
            pl.BlockSpec((None, L, cdim), lambda bi, ti: (bi, ti, COL_XBC // cdim)),
            pl.BlockSpec((None, L, LANES), lambda bi, ti: (bi, ti, COL_DT // LANES)),
            pl.BlockSpec((None, kc - 1, cdim), lambda bi, ti: (bi, 0, 0)),
            pl.BlockSpec((None, SSD_G, SSD_N, gw), lambda bi, ti: (bi, 0, 0, 0)),
            full((kc, cdim)), full((1, cdim)), full((1, LANES)), full((1, LANES)), full((1, inner)), full((1, inner)),
        ],
        out_specs=[
            pl.BlockSpec((None, L, inner), lambda bi, ti: (bi, ti, 0)),
            pl.BlockSpec((None, SSD_G, SSD_N, gw), lambda bi, ti: (bi, 0, 0, 0)),
            pl.BlockSpec((None, kc - 1, cdim), lambda bi, ti: (bi, 0, 0)),
        ],
        out_shape=[
            jax.ShapeDtypeStruct((b, t, inner), BF16),
            jax.ShapeDtypeStruct((b, SSD_G, SSD_N, gw), F32),
            jax.ShapeDtypeStruct((b, kc - 1, cdim), F32),
        ],
        scratch_shapes=[
            pltpu.VMEM((L + LRU_HIST, cdim), F32),
            pltpu.VMEM((SSD_G, SSD_N, gw), F32),
        ],
        compiler_params=_cparams("parallel", "arbitrary"),
        name="ssd_branch",
    )(p3, p3, p3, conv_state, st0, pw["ssd_conv_w"], pw["ssd_conv_b"], pw["ssd_dt_bias"], pw["ssd_a_log"],
      pw["ssd_d"], pw["ssd_norm"])


CF_HIST = 32
CF_ROWS = 64


def _cf_kernel(in_ref, g_ref, cs_ref, cw_ref, cb_ref, lg_ref, lb_ref, u_ref, cn_ref, xe_ref):
    tt = in_ref.shape[0]
    w = g_ref.shape[1]
    kc = cw_ref.shape[0]
    hist = CF_HIST - (kc - 1)

    @pl.when(pl.program_id(1) == 0)
    def _():
        xe_ref[hist:CF_HIST, :] = cs_ref[...]

    cf = in_ref[...].astype(F32)
    xe_ref[CF_HIST:CF_HIST + tt, :] = cf[:, :w] * _sigmoid(cf[:, w:])

    def body(s, carry):
        base = pl.multiple_of(s * CF_ROWS, CF_ROWS)
        c = None
        for r in range(SUBLANES):
            rows = CF_ROWS + (SUBLANES if r else 0)
            part = None
            for o in range(hist + (r - hist) % SUBLANES, hist + kc, SUBLANES):
                term = cw_ref[o - hist:o - hist + 1, :] * xe_ref[pl.ds(base + (o - r), rows), :]
                part = term if part is None else part + term
            piece = part[r:r + CF_ROWS, :]
            c = piece if c is None else c + piece
        c = c + cb_ref[...]
        mu = jnp.mean(c, axis=-1, keepdims=True)
        cc = c - mu
        var = jnp.mean(cc * cc, axis=-1, keepdims=True)
        ln = cc * lax.rsqrt(var + EPS) * lg_ref[...] + lb_ref[...]
        gate = g_ref[pl.ds(base, CF_ROWS), :].astype(F32)
        u_ref[pl.ds(base, CF_ROWS), :] = (_silu(ln) * _silu(gate)).astype(u_ref.dtype)
        return carry

    lax.fori_loop(0, tt // CF_ROWS, body, 0)
    tail = xe_ref[tt + hist:tt + CF_HIST, :]
    xe_ref[hist:CF_HIST, :] = tail
    cn_ref[...] = tail


def _cf_branch(p3, conv_state, pw, tt):
    b, t, _ = p3.shape
    w = conv_state.shape[-1]
    kc = pw["cf_conv_w"].shape[0]
    full = lambda shape: pl.BlockSpec(shape, lambda bi, ti: (0,) * len(shape))
    return pl.pallas_call(
        _cf_kernel,
        grid=(b, t // tt),
        in_specs=[
            pl.BlockSpec((None, tt, 2 * w), lambda bi, ti: (bi, ti, COL_CF_IN // (2 * w))),
            pl.BlockSpec((None, tt, w), lambda bi, ti: (bi, ti, COL_CF_G // w)),
            pl.BlockSpec((None, kc - 1, w), lambda bi, ti: (bi, 0, 0)),
            full((kc, w)), full((1, w)), full((1, w)), full((1, w)),
        ],
        out_specs=[
            pl.BlockSpec((None, tt, w), lambda bi, ti: (bi, ti, 0)),
            pl.BlockSpec((None, kc - 1, w), lambda bi, ti: (bi, 0, 0)),
        ],
        out_shape=[
            jax.ShapeDtypeStruct((b, t, w), BF16),
            jax.ShapeDtypeStruct((b, kc - 1, w), F32),
        ],
        scratch_shapes=[pltpu.VMEM((tt + CF_HIST, w), F32)],
        compiler_params=_cparams("parallel", "arbitrary"),
        name="cf_branch",
    )(p3, p3, conv_state, pw["cf_conv_w"], pw["cf_conv_b"], pw["cf_ln_g"], pw["cf_ln_b"])


def _sb_kernel(q_ref, kt_ref, v_ref, g_ref, u_ref, acc_ref, out_ref, *, q_pos0, tk):
    tq, w = q_ref.shape
    n_pairs = w // LANES
    per_pair = LANES // SB_DH
    q_start = q_pos0 + pl.program_id(1) * tq
    kd = (q_start + (tq - 1)) // tk

    strictly_earlier = (kd * tk + _iota((tq, tk), 1)) < (q_start + _iota((tq, tk), 0))
    later = (_iota((tk, tk), 0) > _iota((tk, tk), 1)).astype(BF16)
    ones = jnp.ones((tk, tk), BF16)
    top = jnp.concatenate([later, ones], axis=1)
    sums_rhs = jnp.concatenate([top, top], axis=0)

    head_of_lane = _div_pow2(_iota((1, LANES), 1), SB_DH)
    lane_masks = [(head_of_lane == r).astype(BF16) for r in range(per_pair)]
    q_masks = [m * (SB_DH ** -0.5) for m in lane_masks]
    qm = []
    for p in range(n_pairs):
        qp = q_ref[:, p * LANES:(p + 1) * LANES]
        qm.append(jnp.concatenate([qp * m for m in q_masks], axis=0))
    earlier_stacked = jnp.concatenate([strictly_earlier] * per_pair, axis=0)

    acc_ref[...] = jnp.zeros_like(acc_ref)
    out_ref[...] = jnp.zeros_like(out_ref)

    def block(kb, masked):
        off = pl.multiple_of(kb * tk, tk)
        zs, lsms, hls = [], [], []
        for p in range(n_pairs):
            kpt = kt_ref[p * LANES:(p + 1) * LANES, pl.ds(off, tk)]
            z = _dot(qm[p], kpt)
            lsm = -_softplus(z)
            if masked:
                lsm = jnp.where(earlier_stacked, lsm, 0.0)
            hi = lsm.astype(BF16)
            lo = (lsm - hi.astype(F32)).astype(BF16)
            zs.append(z)
            lsms.append(lsm)
            hls.append(jnp.concatenate([hi, lo], axis=1))
        sums = _dot(jnp.concatenate(hls, axis=0), sums_rhs)
        worst = None
        for p in range(n_pairs):
            rows = slice(p * per_pair * tq, (p + 1) * per_pair * tq)
            acc = acc_ref[p]
            lw = zs[p] + lsms[p] + sums[rows, :tk] + acc
            if masked:
                lw = jnp.where(earlier_stacked, lw, NEG_BIG)
            wgt = jnp.exp(lw).astype(BF16)
            acc = acc + sums[rows, tk:]
            acc_ref[p] = acc
            worst = acc if worst is None else jnp.maximum(worst, acc)
            vp = v_ref[pl.ds(off, tk), p * LANES:(p + 1) * LANES]
            v2 = jnp.concatenate([vp * m for m in lane_masks], axis=0)
            wcat = jnp.concatenate([wgt[r * tq:(r + 1) * tq] for r in range(per_pair)], axis=1)
            out_ref[p] = out_ref[p] + _dot(wcat, v2)
        return jnp.max(worst)

    worst0 = block(kd, True)

    def cond(c):
        kb, worst = c
        return jnp.logical_and(kb >= 0, worst >= F32_EXP_ZERO_BELOW)

    def body(c):
        kb, _ = c
        return kb - 1, block(kb, False)

    lax.while_loop(cond, body, (kd - 1, worst0))

    o = jnp.concatenate([out_ref[p] for p in range(n_pairs)], axis=1)
    u_ref[...] = (o * _silu(g_ref[...].astype(F32))).astype(u_ref.dtype)


def _sb_branch(p3, kt_all, v_all, q_pos0, tq):
    b, t, _ = p3.shape
    w, keys = kt_all.shape[1:]
    tk = SB_KEY_BLOCK
    assert tk % tq == 0 and q_pos0 % tq == 0 and keys % tk == 0 and v_all.shape[1] == keys
    return pl.pallas_call(
        functools.partial(_sb_kernel, q_pos0=q_pos0, tk=tk),
        grid=(b, t // tq),
        in_specs=[
            pl.BlockSpec((None, tq, w), lambda bi, qi: (bi, qi, COL_Q // w)),
            pl.BlockSpec((None, w, keys), lambda bi, qi: (bi, 0, 0)),
            pl.BlockSpec((None, keys, w), lambda bi, qi: (bi, 0, 0)),
            pl.BlockSpec((None, tq, w), lambda bi, qi: (bi, qi, COL_SB_G // w)),
        ],
        out_specs=pl.BlockSpec((None, tq, w), lambda bi, qi: (bi, qi, 0)),
        out_shape=jax.ShapeDtypeStruct((b, t, w), BF16),
        scratch_shapes=[
            pltpu.VMEM((w // LANES, (LANES // SB_DH) * tq, tk), F32),
            pltpu.VMEM((w // LANES, tq, LANES), F32),
        ],
        compiler_params=_cparams("parallel", "arbitrary"),
        name="sb_branch",
    )(p3, kt_all, v_all, p3)


def _mix_kernel(ua_ref, ub_ref, uc_ref, ud_ref, m0_ref, m1_ref, m2_ref, m3_ref, x_ref, wd_ref, wo_ref, g_ref, o_ref):
    us = (ua_ref, ub_ref, uc_ref, ud_ref)
    ms = (m0_ref, m1_ref, m2_ref, m3_ref)
    mixed = None
    for i in range(N_BRANCH):
        y = _dot(us[i][...], wd_ref[i])
        term = _sigmoid(ms[i][...].astype(F32)) * y
        mixed = term if mixed is None else mixed + term
    o = _dot(mixed.astype(BF16), wo_ref[...])
    ms2 = jnp.mean(o * o, axis=-1, keepdims=True)
    o_ref[...] = x_ref[...] + o * lax.rsqrt(ms2 + EPS) * g_ref[...]


def _mix(us, p2, x2, pw):
    n, d = x2.shape
    w = us[0].shape[-1]
    tm = min(n, 512)
    ublk = pl.BlockSpec((tm, w), lambda i: (i, 0))
    mblk = lambda j: pl.BlockSpec((tm, d), lambda i: (i, COL_MERGE // d + j))
    return pl.pallas_call(
        _mix_kernel,
        grid=(n // tm,),
        in_specs=[ublk] * 4 + [mblk(j) for j in range(N_BRANCH)] + [
            pl.BlockSpec((tm, d), lambda i: (i, 0)),
            pl.BlockSpec((N_BRANCH, w, d), lambda i: (0, 0, 0)),
            pl.BlockSpec((d, d), lambda i: (0, 0)),
            pl.BlockSpec((1, d), lambda i: (0, 0)),
        ],
        out_specs=pl.BlockSpec((tm, d), lambda i: (i, 0)),
        out_shape=jax.ShapeDtypeStruct((n, d), F32),
        input_output_aliases={len(us) + N_BRANCH: 0},
        compiler_params=_cparams("parallel"),
        name="mix",
    )(*us, p2, p2, p2, p2, x2, pw["w_down"], pw["w_out"], pw["norm_post"])


def _block_diag(wb):
    n, c, d = wb.shape
    eye = jnp.eye(n, dtype=wb.dtype)
    return (eye[:, None, :, None] * wb[:, :, None, :]).reshape(n * c, n * d)


def _prep_layer(p):
    d_model = p["w_in"].shape[0]
    bw = p["lru_lambda"].shape[-1]
    cdim = p["ssd_conv_b"].shape[-1]
    n_heads = p["ssd_dt_bias"].shape[-1]
    sizes = (bw, bw, bw, cdim, n_heads, 2 * bw, bw, 3 * bw, bw, N_BRANCH * d_model)
    offs = [0]
    for s in sizes:
        offs.append(offs[-1] + s)
    seg = lambda i: p["w_in"][:, offs[i]:offs[i + 1]]
    lru_x, lru_g, ssd_z, xbc, dt, cf_in, cf_g, qkv, sb_g, merge = [seg(i) for i in range(len(sizes))]
    q, k, v = qkv[:, :bw], qkv[:, bw:2 * bw], qkv[:, 2 * bw:]
    zeros = lambda c: jnp.zeros((d_model, c), p["w_in"].dtype)
    main = jnp.concatenate(
        [xbc, cf_in, lru_x, lru_g, ssd_z, cf_g, q, sb_g, merge, dt, zeros(LANES - n_heads),
         zeros(P_COLS - COL_DT - LANES)], axis=1)
    assert main.shape[1] == P_COLS
    pad_h = lambda a, fill: jnp.concatenate([a, jnp.full((LANES - n_heads,), fill, a.dtype)])[None, :]
    row = lambda a: a[None, :].astype(F32)
    return dict(
        w_main=main.astype(BF16), w_kt=k.T.astype(BF16), w_vt=v.T.astype(BF16), w_v=v.astype(BF16),
        norm_pre=row(p["norm_pre"]), norm_post=row(p["norm_post"]),
        lru_conv_w=p["lru_conv_w"], lru_conv_b=row(p["lru_conv_b"]),
        lru_wa=_block_diag(p["lru_wa"]).astype(BF16), lru_ba=row(p["lru_ba"]),
        lru_wx=_block_diag(p["lru_wx"]).astype(BF16), lru_bx=row(p["lru_bx"]),
        lru_lambda=row(p["lru_lambda"]),
        ssd_conv_w=p["ssd_conv_w"], ssd_conv_b=row(p["ssd_conv_b"]),
        ssd_dt_bias=pad_h(p["ssd_dt_bias"], 0.0), ssd_a_log=pad_h(p["ssd_a_log"], 0.0),
        ssd_d=jnp.repeat(p["ssd_d"], SSD_P)[None, :], ssd_norm=row(p["ssd_norm"]),
        cf_conv_w=p["cf_conv_w"], cf_conv_b=row(p["cf_conv_b"]),
        cf_ln_g=row(p["cf_ln_g"]), cf_ln_b=row(p["cf_ln_b"]),
        w_down=p["w_down"].astype(BF16), w_out=p["w_out"].astype(BF16),
    )


def _ssd_state_in(s):
    b = s.shape[0]
    s = s.reshape(b, SSD_G, SSD_R, SSD_P, SSD_N)
    return jnp.transpose(s, (0, 1, 4, 2, 3)).reshape(b, SSD_G, SSD_N, SSD_R * SSD_P)


def _ssd_state_out(s):
    b = s.shape[0]
    s = s.reshape(b, SSD_G, SSD_N, SSD_R, SSD_P)
    return jnp.transpose(s, (0, 1, 3, 4, 2)).reshape(b, SSD_G * SSD_R, SSD_P, SSD_N)


def _layer(x, pw, st, past_k, past_v, k_stack, v_stack, layer, *, seq_tile, ssd_chunk, sb_block):
    b, t, d = x.shape
    x2 = x.reshape(b * t, d)
    p2 = _inproj(x2, pw["norm_pre"], pw["w_main"], BF16, P_TILE_N)
    k_stack, v_stack, kt_new, v_new = _kvproj(x, pw["norm_pre"], pw["w_kt"], pw["w_vt"], pw["w_v"],
                                              k_stack, v_stack, layer)
    p3 = p2.reshape(b, t, P_COLS)

    u_a, lru_h, lru_conv = _lru_branch(p3, st["lru_conv"], st["lru_h"][:, None, :], pw, seq_tile)
    u_b, ssd_new, ssd_conv = _ssd_branch(p3, st["ssd_conv"], _ssd_state_in(st["ssd"]), pw, ssd_chunk)
    u_c, cf_conv = _cf_branch(p3, st["cf_conv"], pw, seq_tile)
    if past_k is None:
        kt_all, v_all, q_pos0 = kt_new, v_new, 0
    else:
        q_pos0 = past_k.shape[1]
        pad = -(q_pos0 + t) % SB_KEY_BLOCK
        past_kt = jnp.transpose(past_k, (0, 2, 3, 1)).reshape(b, -1, q_pos0).astype(BF16)
        past_vb = past_v.reshape(b, q_pos0, -1).astype(BF16)
        kt_all = jnp.pad(jnp.concatenate([past_kt, kt_new], axis=2), ((0, 0), (0, 0), (0, pad)))
        v_all = jnp.pad(jnp.concatenate([past_vb, v_new], axis=1), ((0, 0), (0, pad), (0, 0)))
    u_d = _sb_branch(p3, kt_all, v_all, q_pos0, sb_block)

    w = u_a.shape[-1]
    us = [u.reshape(b * t, w) for u in (u_a, u_b, u_c, u_d)]
    y = _mix(us, p2, x2, pw).reshape(b, t, d)
    new = dict(lru_h=lru_h[:, 0, :], lru_conv=lru_conv, ssd=_ssd_state_out(ssd_new), ssd_conv=ssd_conv,
               cf_conv=cf_conv)
    return y, new, k_stack, v_stack


_STATE_KEYS = ("lru_h", "lru_conv", "ssd", "ssd_conv", "cf_conv")


def kernel(x_prompt, x_sample, state_lru_h, state_lru_conv, state_ssd, state_ssd_conv, state_cf_conv, cache_sb_k, cache_sb_v, norm_pre, norm_post, w_in, lru_conv_w, lru_conv_b, lru_wa, lru_ba, lru_wx, lru_bx, lru_lambda, ssd_conv_w, ssd_conv_b, ssd_dt_bias, ssd_a_log, ssd_d, ssd_norm, cf_conv_w, cf_conv_b, cf_ln_g, cf_ln_b, w_down, w_out):
    raw = dict(norm_pre=norm_pre, norm_post=norm_post, w_in=w_in, lru_conv_w=lru_conv_w, lru_conv_b=lru_conv_b,
               lru_wa=lru_wa, lru_ba=lru_ba, lru_wx=lru_wx, lru_bx=lru_bx, lru_lambda=lru_lambda,
               ssd_conv_w=ssd_conv_w, ssd_conv_b=ssd_conv_b, ssd_dt_bias=ssd_dt_bias, ssd_a_log=ssd_a_log,
               ssd_d=ssd_d, ssd_norm=ssd_norm, cf_conv_w=cf_conv_w, cf_conv_b=cf_conv_b, cf_ln_g=cf_ln_g,
               cf_ln_b=cf_ln_b, w_down=w_down, w_out=w_out)
    pws = jax.vmap(_prep_layer)(raw)

    bp = x_prompt.shape[0]
    dt = x_prompt.dtype
    bw = state_lru_h.shape[-1]
    zero_state = dict(lru_h=jnp.zeros((bp, bw), dt),
                      lru_conv=jnp.zeros((bp,) + state_lru_conv.shape[2:], dt),
                      ssd=jnp.zeros((bp,) + state_ssd.shape[2:], dt),
                      ssd_conv=jnp.zeros((bp,) + state_ssd_conv.shape[2:], dt),
                      cf_conv=jnp.zeros((bp,) + state_cf_conv.shape[2:], dt))

    depth = w_in.shape[0]
    layers = jnp.arange(depth, dtype=jnp.int32)

    def kv_stacks(x):
        shape = (depth, x.shape[0], SB_H * SB_DH, x.shape[1])
        return jnp.zeros(shape, F32), jnp.zeros(shape, F32)

    def kv_out(stack):
        d_, b_, _, t_ = stack.shape
        return jnp.transpose(stack.reshape(d_, b_, SB_H, SB_DH, t_), (0, 1, 4, 2, 3))

    def prompt_step(carry, xs):
        x, k_stack, v_stack = carry
        pw, layer = xs
        y, new, k_stack, v_stack = _layer(x, pw, zero_state, None, None, k_stack, v_stack, layer,
                                          seq_tile=512, ssd_chunk=128, sb_block=128)
        return (y, k_stack, v_stack), tuple(new[k] for k in _STATE_KEYS)

    (y_p, k_p, v_p), p_new = lax.scan(prompt_step, (x_prompt,) + kv_stacks(x_prompt), (pws, layers))

    t_s = x_sample.shape[1]

    def sample_step(carry, xs):
        x, k_stack, v_stack = carry
        pw, st, pk, pv, layer = xs
        y, new, k_stack, v_stack = _layer(x, pw, st, pk, pv, k_stack, v_stack, layer,
                                          seq_tile=t_s, ssd_chunk=t_s, sb_block=t_s)
        return (y, k_stack, v_stack), tuple(new[k] for k in _STATE_KEYS)

    st_s = dict(lru_h=state_lru_h, lru_conv=state_lru_conv, ssd=state_ssd, ssd_conv=state_ssd_conv,
                cf_conv=state_cf_conv)
    (y_s, k_s, v_s), s_new = lax.scan(sample_step, (x_sample,) + kv_stacks(x_sample),
                                      (pws, st_s, cache_sb_k, cache_sb_v, layers))

    outs = [y_p, y_s]
    for i in range(len(_STATE_KEYS)):
        outs += [p_new[i], s_new[i]]
    outs += [kv_out(k_p), kv_out(k_s), kv_out(v_p), kv_out(v_s)]
    return tuple(outs)
```

```python
import functools

import jax
import jax.numpy as jnp
from jax import lax
from jax.experimental import pallas as pl
from jax.experimental.pallas import tpu as pltpu

F32 = jnp.float32
BF16 = jnp.bfloat16

EPS = 1e-6
LRU_C = 8.0
N_BRANCH = 4
LRU_BLOCKS = 8
SSD_P = 64
SSD_N = 64
SSD_G = 4
SSD_R = 2
SB_H = 8
SB_DH = 64
SB_KEY_BLOCK = 128

LANES = 128
SUBLANES = 8
VMEM_LIMIT_BYTES = 56 * 1024 * 1024

F32_EXP_ZERO_BELOW = -104.0
NEG_BIG = -1e30

COL_XBC = 0
COL_CF_IN = 1024
COL_LRU_X = 2048
COL_LRU_G = 2560
COL_SSD_Z = 3072
COL_CF_G = 3584
COL_Q = 4096
COL_SB_G = 4608
COL_MERGE = 5120
COL_DT = 9216
P_COLS = 9472
P_TILE_N = P_COLS // 2
INPROJ_CHUNK = 512


def _cparams(*sem):
    return pltpu.CompilerParams(dimension_semantics=sem, vmem_limit_bytes=VMEM_LIMIT_BYTES)


def _sigmoid(x):
    return 0.5 * jnp.tanh(0.5 * x) + 0.5


def _silu(x):
    return x * _sigmoid(x)


def _softplus(x):
    return jnp.maximum(x, 0.0) + jnp.log(1.0 + jnp.exp(-jnp.abs(x)))


def _split3(x):
    hi = x.astype(BF16)
    r = x - hi.astype(F32)
    mid = r.astype(BF16)
    lo = (r - mid.astype(F32)).astype(BF16)
    return hi, mid, lo


def _dot(a, b):
    return jnp.dot(a, b, preferred_element_type=F32)


def _dot_nt(a, b):
    return lax.dot_general(a, b, (((1,), (1,)), ((), ())), preferred_element_type=F32)


def _sel_left(m01, x):
    hi, mid, lo = _split3(x)
    return _dot(m01, hi) + _dot(m01, mid) + _dot(m01, lo)


def _sel_right(x, m01):
    hi, mid, lo = _split3(x)
    return _dot(hi, m01) + _dot(mid, m01) + _dot(lo, m01)


def _sel_nt(m01, x):
    hi, mid, lo = _split3(x)
    return _dot_nt(m01, hi) + _dot_nt(m01, mid) + _dot_nt(m01, lo)


def _iota(shape, dim):
    return lax.broadcasted_iota(jnp.int32, shape, dim)


def _div_pow2(x, n):
    shift = n.bit_length() - 1
    assert 1 << shift == n
    return lax.shift_right_logical(x, shift)


def _inproj_kernel(x_ref, g_ref, w_ref, o_ref, h_ref):
    @pl.when(pl.program_id(1) == 0)
    def _():
        x = x_ref[...]
        ms = jnp.mean(x * x, axis=-1, keepdims=True)
        h_ref[...] = (x * lax.rsqrt(ms + EPS) * g_ref[...]).astype(BF16)

    h = h_ref[...]
    tn = o_ref.shape[1]
    for c0 in range(0, tn, INPROJ_CHUNK):
        c1 = min(c0 + INPROJ_CHUNK, tn)
        o_ref[:, c0:c1] = _dot(h, w_ref[:, c0:c1]).astype(o_ref.dtype)


def _inproj(x2, g, w, out_dtype, tn):
    n, d = x2.shape
    cols = w.shape[1]
    tm = min(n, 1024)
    return pl.pallas_call(
        _inproj_kernel,
        grid=(n // tm, cols // tn),
        in_specs=[
            pl.BlockSpec((tm, d), lambda i, j: (i, 0)),
            pl.BlockSpec((1, d), lambda i, j: (0, 0)),
            pl.BlockSpec((d, tn), lambda i, j: (0, j)),
        ],
        out_specs=pl.BlockSpec((tm, tn), lambda i, j: (i, j)),
        out_shape=jax.ShapeDtypeStruct((n, cols), out_dtype),
        scratch_shapes=[pltpu.VMEM((tm, d), BF16)],
        compiler_params=_cparams("parallel", "arbitrary"),
        name="inproj_" + jnp.dtype(out_dtype).name,
    )(x2, g, w)


def _kv_kernel(layer_ref, x_ref, g_ref, wkt_ref, wvt_ref, wv_ref, kin_ref, vin_ref, k_ref, v_ref, kb_ref, vb_ref):
    del layer_ref, kin_ref, vin_ref
    x = x_ref[...]
    ms = jnp.mean(x * x, axis=-1, keepdims=True)
    h = (x * lax.rsqrt(ms + EPS) * g_ref[...]).astype(BF16)
    kt = _dot_nt(wkt_ref[...], h)
    k_ref[...] = kt
    v_ref[...] = _dot_nt(wvt_ref[...], h)
    kb_ref[...] = kt.astype(BF16)
    vb_ref[...] = _dot(h, wv_ref[...]).astype(BF16)


def _kvproj(x, g, wkt, wvt, wv, k_stack, v_stack, layer):
    b, t, d = x.shape
    w = wv.shape[1]
    tm = min(t, 1024)
    stack_blk = pl.BlockSpec((None, None, w, tm), lambda bi, ti, l: (l[0], bi, 0, ti))
    full = lambda shape: pl.BlockSpec(shape, lambda bi, ti, l: (0,) * len(shape))
    grid_spec = pltpu.PrefetchScalarGridSpec(
        num_scalar_prefetch=1,
        grid=(b, t // tm),
        in_specs=[
            pl.BlockSpec((None, tm, d), lambda bi, ti, l: (bi, ti, 0)),
            full((1, d)), full((w, d)), full((w, d)), full((d, w)),
            pl.BlockSpec(memory_space=pl.ANY),
            pl.BlockSpec(memory_space=pl.ANY),
        ],
        out_specs=[stack_blk, stack_blk,
                   pl.BlockSpec((None, w, tm), lambda bi, ti, l: (bi, 0, ti)),
                   pl.BlockSpec((None, tm, w), lambda bi, ti, l: (bi, ti, 0))],
    )
    return pl.pallas_call(
        _kv_kernel,
        grid_spec=grid_spec,
        out_shape=[jax.ShapeDtypeStruct(k_stack.shape, F32), jax.ShapeDtypeStruct(v_stack.shape, F32),
                   jax.ShapeDtypeStruct((b, w, t), BF16), jax.ShapeDtypeStruct((b, t, w), BF16)],
        input_output_aliases={6: 0, 7: 1},
        compiler_params=_cparams("arbitrary", "arbitrary"),
        name="kvproj",
    )(jnp.reshape(layer, (1,)).astype(jnp.int32), x, g, wkt, wvt, wv, k_stack, v_stack)


def _conv_taps(xe_ref, w_ref, b_ref, base, rows, hist_off, k):
    acc = None
    for j in range(k):
        term = w_ref[j:j + 1, :] * xe_ref[pl.ds(base + hist_off + j, rows), :]
        acc = term if acc is None else acc + term
    return acc + b_ref[...]


LRU_HIST = 8


def _lru_kernel(x_ref, g_ref, cs_ref, h0_ref, cw_ref, cb_ref, wa_ref, ba_ref, wx_ref, bx_ref, lam_ref,
                u_ref, hl_ref, cn_ref, xe_ref, a_ref, b_ref, h_ref):
    tt, w = x_ref.shape
    kc = cw_ref.shape[0]
    hist = LRU_HIST - (kc - 1)

    @pl.when(pl.program_id(1) == 0)
    def _():
        xe_ref[hist:LRU_HIST, :] = cs_ref[...]
        h_ref[...] = h0_ref[...]

    xe_ref[LRU_HIST:LRU_HIST + tt, :] = x_ref[...].astype(F32)
    xc = _conv_taps(xe_ref, cw_ref, cb_ref, 0, tt, hist, kc)
    tail = xe_ref[tt + hist:tt + LRU_HIST, :]
    xe_ref[hist:LRU_HIST, :] = tail
    cn_ref[...] = tail

    xcb = xc.astype(BF16)
    gate_a = _dot(xcb, wa_ref[...]) + ba_ref[...]
    gate_x = _dot(xcb, wx_ref[...]) + bx_ref[...]
    r = _sigmoid(gate_a)
    i = _sigmoid(gate_x)
    log_a = (-LRU_C) * r * _softplus(-lam_ref[...])
    a = jnp.exp(log_a)
    one_m_a2 = (1.0 - a) * (1.0 + a)
    a_ref[...] = a
    root = jnp.where(one_m_a2 > 0.0, one_m_a2 * lax.rsqrt(one_m_a2), 0.0)
    b_ref[...] = root * (i * xc)

    row = _iota((SUBLANES, w), 0)

    def body(s, h):
        off = pl.multiple_of(s * SUBLANES, SUBLANES)
        a8 = a_ref[pl.ds(off, SUBLANES), :]
        b8 = b_ref[pl.ds(off, SUBLANES), :]
        for d in (1, 2, 4):
            keep = row >= d
            a_sh = pltpu.roll(a8, d, axis=0)
            b_sh = pltpu.roll(b8, d, axis=0)
            b8 = jnp.where(keep, a8 * b_sh + b8, b8)
            a8 = jnp.where(keep, a8 * a_sh, a8)
        h8 = a8 * h + b8
        b_ref[pl.ds(off, SUBLANES), :] = h8
        return h8[SUBLANES - 1:SUBLANES, :]

    h_last = lax.fori_loop(0, tt // SUBLANES, body, h_ref[...])
    h_ref[...] = h_last
    hl_ref[...] = h_last
    u_ref[...] = (b_ref[...] * _silu(g_ref[...].astype(F32))).astype(u_ref.dtype)


def _lru_branch(p3, conv_state, h0, pw, tt):
    b, t, _ = p3.shape
    w = conv_state.shape[-1]
    kc = pw["lru_conv_w"].shape[0]
    col = lambda off: off // w
    full = lambda shape: pl.BlockSpec(shape, lambda bi, ti: (0,) * len(shape))
    return pl.pallas_call(
        _lru_kernel,
        grid=(b, t // tt),
        in_specs=[
            pl.BlockSpec((None, tt, w), lambda bi, ti: (bi, ti, col(COL_LRU_X))),
            pl.BlockSpec((None, tt, w), lambda bi, ti: (bi, ti, col(COL_LRU_G))),
            pl.BlockSpec((None, kc - 1, w), lambda bi, ti: (bi, 0, 0)),
            pl.BlockSpec((None, 1, w), lambda bi, ti: (bi, 0, 0)),
            full((kc, w)), full((1, w)), full((w, w)), full((1, w)), full((w, w)), full((1, w)), full((1, w)),
        ],
        out_specs=[
            pl.BlockSpec((None, tt, w), lambda bi, ti: (bi, ti, 0)),
            pl.BlockSpec((None, 1, w), lambda bi, ti: (bi, 0, 0)),
            pl.BlockSpec((None, kc - 1, w), lambda bi, ti: (bi, 0, 0)),
        ],
        out_shape=[
            jax.ShapeDtypeStruct((b, t, w), BF16),
            jax.ShapeDtypeStruct((b, 1, w), F32),
            jax.ShapeDtypeStruct((b, kc - 1, w), F32),
        ],
        scratch_shapes=[
            pltpu.VMEM((tt + LRU_HIST, w), F32),
            pltpu.VMEM((tt, w), F32),
            pltpu.VMEM((tt, w), F32),
            pltpu.VMEM((1, w), F32),
        ],
        compiler_params=_cparams("parallel", "arbitrary"),
        name="lru_branch",
    )(p3, p3, conv_state, h0, pw["lru_conv_w"], pw["lru_conv_b"], pw["lru_wa"], pw["lru_ba"],
      pw["lru_wx"], pw["lru_bx"], pw["lru_lambda"])


def _ssd_kernel(z_ref, xbc_ref, dt_ref, cs_ref, s0_ref, cw_ref, cb_ref, dtb_ref, alog_ref, dvec_ref, nw_ref,
                u_ref, sn_ref, cn_ref, xe_ref, st_ref):
    L = z_ref.shape[0]
    inner = z_ref.shape[1]
    kc = cw_ref.shape[0]
    hist = LRU_HIST - (kc - 1)
    gw = SSD_R * SSD_P

    @pl.when(pl.program_id(1) == 0)
    def _():
        xe_ref[hist:LRU_HIST, :] = cs_ref[...]
        st_ref[...] = s0_ref[...]

    xe_ref[LRU_HIST:LRU_HIST + L, :] = xbc_ref[...].astype(F32)
    xa = _silu(_conv_taps(xe_ref, cw_ref, cb_ref, 0, L, hist, kc))
    tail = xe_ref[L + hist:L + LRU_HIST, :]
    xe_ref[hist:LRU_HIST, :] = tail
    cn_ref[...] = tail

    x = xa[:, :inner]
    bm = xa[:, inner:inner + SSD_G * SSD_N]
    cm = xa[:, inner + SSD_G * SSD_N:]

    dt = _softplus(dt_ref[...].astype(F32) + dtb_ref[...])
    adt = dt * (-jnp.exp(alog_ref[...]))

    causal = _iota((L, L), 0) >= _iota((L, L), 1)
    tril = causal.astype(BF16)
    eye = (_iota((LANES, LANES), 0) == _iota((LANES, LANES), 1)).astype(BF16)
    eye_n = eye[:SSD_N, :SSD_N]
    expand_c = (_div_pow2(_iota((LANES, inner), 1), SSD_P) == _iota((LANES, inner), 0)).astype(BF16)
    n_heads = inner // SSD_P
    expand_l = (_div_pow2(_iota((LANES, n_heads * L), 1), L) == _iota((LANES, n_heads * L), 0)).astype(BF16)

    acs = _sel_left(tril, adt)
    acs_t = _sel_nt(eye, acs)
    dt_c = _sel_right(dt, expand_c)
    acs_c = _sel_right(acs, expand_c)
    acs_w = _sel_right(acs, expand_l)

    xdt = x * dt_c
    last = acs_c[L - 1:L, :]
    xs = xdt * jnp.exp(last - acs_c)
    e_acs = jnp.exp(acs_c)
    chunk_decay = jnp.exp(last)
    lane = _iota((L, gw), 1)

    ys = []
    for g in range(SSD_G):
        cg = cm[:, g * SSD_N:(g + 1) * SSD_N].astype(BF16)
        bg = bm[:, g * SSD_N:(g + 1) * SSD_N].astype(BF16)
        cb = _dot_nt(cg, bg)
        ms = []
        for r in range(SSD_R):
            h = g * SSD_R + r
            seg = acs_w[:, h * L:(h + 1) * L] - acs_t[h:h + 1, :]
            dec = jnp.exp(jnp.where(causal, seg, NEG_BIG))
            ms.append((cb * dec).astype(BF16))
        mcat = jnp.concatenate(ms, axis=1)
        xg = xdt[:, g * gw:(g + 1) * gw]
        rhs = jnp.concatenate([jnp.where(lane < SSD_P, xg, 0.0), jnp.where(lane >= SSD_P, xg, 0.0)],
                              axis=0).astype(BF16)
        y_diag = _dot(mcat, rhs)
        st = st_ref[g]
        y_off = _dot(cg, st.astype(BF16)) * e_acs[:, g * gw:(g + 1) * gw]
        bg_t = _dot_nt(eye_n, bg).astype(BF16)
        st_ref[g] = st * chunk_decay[:, g * gw:(g + 1) * gw] + _dot(bg_t, xs[:, g * gw:(g + 1) * gw].astype(BF16))
        ys.append(y_diag + y_off)
    y = jnp.concatenate(ys, axis=1) + dvec_ref[...] * x
    sn_ref[...] = st_ref[...]

    gated = y * _silu(z_ref[...].astype(F32))
    ms2 = jnp.mean(gated * gated, axis=-1, keepdims=True)
    u_ref[...] = (gated * lax.rsqrt(ms2 + EPS) * nw_ref[...]).astype(u_ref.dtype)


def _ssd_branch(p3, conv_state, st0, pw, L):
    b, t, _ = p3.shape
    cdim = conv_state.shape[-1]
    inner = pw["ssd_norm"].shape[-1]
    kc = pw["ssd_conv_w"].shape[0]
    gw = SSD_R * SSD_P
    full = lambda shape: pl.BlockSpec(shape, lambda bi, ti: (0,) * len(shape))
    return pl.pallas_call(
        _ssd_kernel,
        grid=(b, t // L),
        in_specs=[
            pl.BlockSpec((None, L, inner), lambda bi, ti: (bi, ti, COL_SSD_Z // inner)),
            pl.BlockSpec((None, L, cdim), lambda bi, ti: (bi, ti, COL_XBC // cdim)),
            pl.BlockSpec((None, L, LANES), lambda bi, ti: (bi, ti, COL_DT // LANES)),
            pl.BlockSpec((None, kc - 1, cdim), lambda bi, ti: (bi, 0, 0)),
            pl.BlockSpec((None, SSD_G, SSD_N, gw), lambda bi, ti: (bi, 0, 0, 0)),
            full((kc, cdim)), full((1, cdim)), full((1, LANES)), full((1, LANES)), full((1, inner)), full((1, inner)),
        ],
        out_specs=[
            pl.BlockSpec((None, L, inner), lambda bi, ti: (bi, ti, 0)),
            pl.BlockSpec((None, SSD_G, SSD_N, gw), lambda bi, ti: (bi, 0, 0, 0)),
            pl.BlockSpec((None, kc - 1, cdim), lambda bi, ti: (bi, 0, 0)),
        ],
        out_shape=[
            jax.ShapeDtypeStruct((b, t, inner), BF16),
            jax.ShapeDtypeStruct((b, SSD_G, SSD_N, gw), F32),
            jax.ShapeDtypeStruct((b, kc - 1, cdim), F32),
        ],
        scratch_shapes=[
            pltpu.VMEM((L + LRU_HIST, cdim), F32),
            pltpu.VMEM((SSD_G, SSD_N, gw), F32),
        ],
        compiler_params=_cparams("parallel", "arbitrary"),
        name="ssd_branch",
    )(p3, p3, p3, conv_state, st0, pw["ssd_conv_w"], pw["ssd_conv_b"], pw["ssd_dt_bias"], pw["ssd_a_log"],
      pw["ssd_d"], pw["ssd_norm"])


CF_HIST = 32
CF_ROWS = 128


def _cf_kernel(in_ref, g_ref, cs_ref, cw_ref, cb_ref, lg_ref, lb_ref, u_ref, cn_ref, xe_ref):
    tt = in_ref.shape[0]
    w = g_ref.shape[1]
    kc = cw_ref.shape[0]
    hist = CF_HIST - (kc - 1)

    @pl.when(pl.program_id(1) == 0)
    def _():
        xe_ref[hist:CF_HIST, :] = cs_ref[...]

    cf = in_ref[...].astype(F32)
    xe_ref[CF_HIST:CF_HIST + tt, :] = cf[:, :w] * _sigmoid(cf[:, w:])

    n_rows = min(CF_ROWS, tt)

    def body(s, carry):
        base = pl.multiple_of(s * n_rows, n_rows)
        c = None
        for r in range(SUBLANES):
            rows = n_rows + (SUBLANES if r else 0)
            part = None
            for o in range(hist + (r - hist) % SUBLANES, hist + kc, SUBLANES):
                term = cw_ref[o - hist:o - hist + 1, :] * xe_ref[pl.ds(base + (o - r), rows), :]
                part = term if part is None else part + term
            piece = part[r:r + n_rows, :]
            c = piece if c is None else c + piece
        c = c + cb_ref[...]
        mu = jnp.mean(c, axis=-1, keepdims=True)
        cc = c - mu
        var = jnp.mean(cc * cc, axis=-1, keepdims=True)
        ln = cc * lax.rsqrt(var + EPS) * lg_ref[...] + lb_ref[...]
        gate = g_ref[pl.ds(base, n_rows), :].astype(F32)
        u_ref[pl.ds(base, n_rows), :] = (_silu(ln) * _silu(gate)).astype(u_ref.dtype)
        return carry

    lax.fori_loop(0, tt // n_rows, body, 0)
    tail = xe_ref[tt + hist:tt + CF_HIST, :]
    xe_ref[hist:CF_HIST, :] = tail
    cn_ref[...] = tail


def _cf_branch(p3, conv_state, pw, tt):
    b, t, _ = p3.shape
    w = conv_state.shape[-1]
    kc = pw["cf_conv_w"].shape[0]
    full = lambda shape: pl.BlockSpec(shape, lambda bi, ti: (0,) * len(shape))
    return pl.pallas_call(
        _cf_kernel,
        grid=(b, t // tt),
        in_specs=[
            pl.BlockSpec((None, tt, 2 * w), lambda bi, ti: (bi, ti, COL_CF_IN // (2 * w))),
            pl.BlockSpec((None, tt, w), lambda bi, ti: (bi, ti, COL_CF_G // w)),
            pl.BlockSpec((None, kc - 1, w), lambda bi, ti: (bi, 0, 0)),
            full((kc, w)), full((1, w)), full((1, w)), full((1, w)),
        ],
        out_specs=[
            pl.BlockSpec((None, tt, w), lambda bi, ti: (bi, ti, 0)),
            pl.BlockSpec((None, kc - 1, w), lambda bi, ti: (bi, 0, 0)),
        ],
        out_shape=[
            jax.ShapeDtypeStruct((b, t, w), BF16),
            jax.ShapeDtypeStruct((b, kc - 1, w), F32),
        ],
        scratch_shapes=[pltpu.VMEM((tt + CF_HIST, w), F32)],
        compiler_params=_cparams("parallel", "arbitrary"),
        name="cf_branch",
    )(p3, p3, conv_state, pw["cf_conv_w"], pw["cf_conv_b"], pw["cf_ln_g"], pw["cf_ln_b"])


def _sb_kernel(q_ref, kt_ref, v_ref, g_ref, sums_ref, u_ref, acc_ref, out_ref, *, q_pos0, tk):
    tq, w = q_ref.shape
    n_pairs = w // LANES
    per_pair = LANES // SB_DH
    q_start = q_pos0 + pl.program_id(1) * tq
    kd = (q_start + (tq - 1)) // tk

    strictly_earlier = (kd * tk + _iota((tq, tk), 1)) < (q_start + _iota((tq, tk), 0))
    sums_rhs = sums_ref[...]

    head_of_lane = _div_pow2(_iota((1, LANES), 1), SB_DH)
    lane_masks = [(head_of_lane == r).astype(BF16) for r in range(per_pair)]
    q_masks = [m * (SB_DH ** -0.5) for m in lane_masks]
    qm = []
    for p in range(n_pairs):
        qp = q_ref[:, p * LANES:(p + 1) * LANES]
        qm.append([qp * m for m in q_masks])
    half = tq // 2

    acc_ref[...] = jnp.zeros_like(acc_ref)
    out_ref[...] = jnp.zeros_like(out_ref)

    def block(kb, masked, nrows):
        stack = lambda parts: jnp.concatenate([x[:nrows] for x in parts], axis=0)
        off = pl.multiple_of(kb * tk, tk)
        earlier = stack([strictly_earlier] * per_pair)
        zs, lsms, hls = [], [], []
        for p in range(n_pairs):
            kpt = kt_ref[p * LANES:(p + 1) * LANES, pl.ds(off, tk)]
            z = _dot(stack(qm[p]), kpt)
            lsm = -_softplus(z)
            if masked:
                lsm = jnp.where(earlier, lsm, 0.0)
            hi = lsm.astype(BF16)
            lo = (lsm - hi.astype(F32)).astype(BF16)
            zs.append(z)
            lsms.append(lsm)
            hls.append(jnp.concatenate([hi, lo], axis=1))
        sums = _dot(jnp.concatenate(hls, axis=0), sums_rhs)
        worst_top, worst_rest = None, None
        for p in range(n_pairs):
            rows = slice(p * per_pair * nrows, (p + 1) * per_pair * nrows)
            acc = stack([acc_ref[p, r] for r in range(per_pair)])
            lw = zs[p] + lsms[p] + sums[rows, :tk] + acc
            if masked:
                lw = jnp.where(earlier, lw, NEG_BIG)
            wgt = jnp.exp(lw).astype(BF16)
            acc = acc + sums[rows, tk:]
            for r in range(per_pair):
                acc_r = acc[r * nrows:(r + 1) * nrows]
                acc_ref[p, r, :nrows] = acc_r
                top = acc_r[:half]
                worst_top = top if worst_top is None else jnp.maximum(worst_top, top)
                if nrows > half:
                    rest = acc_r[half:]
                    worst_rest = rest if worst_rest is None else jnp.maximum(worst_rest, rest)
            vp = v_ref[pl.ds(off, tk), p * LANES:(p + 1) * LANES]
            v2 = jnp.concatenate([vp * m for m in lane_masks], axis=0)
            wcat = jnp.concatenate([wgt[r * nrows:(r + 1) * nrows] for r in range(per_pair)], axis=1)
            out_ref[p, :nrows] = out_ref[p, :nrows] + _dot(wcat, v2)
        if nrows > half:
            return jnp.max(worst_top), jnp.max(worst_rest)
        return jnp.max(worst_top)

    live = lambda kb, worst: jnp.logical_and(kb >= 0, worst >= F32_EXP_ZERO_BELOW)
    top0, rest0 = block(kd, True, tq)

    def full_body(c):
        return (c[0] - 1,) + block(c[0], False, tq)

    kb1, top1, _ = lax.while_loop(lambda c: live(c[0], c[2]), full_body, (kd - 1, top0, rest0))

    def half_body(c):
        return c[0] - 1, block(c[0], False, half)

    lax.while_loop(lambda c: live(c[0], c[1]), half_body, (kb1, top1))

    o = jnp.concatenate([out_ref[p] for p in range(n_pairs)], axis=1)
    u_ref[...] = (o * _silu(g_ref[...].astype(F32))).astype(u_ref.dtype)


def _sb_branch(p3, kt_all, v_all, q_pos0, tq):
    b, t, _ = p3.shape
    w, keys = kt_all.shape[1:]
    tk = SB_KEY_BLOCK
    assert tk % tq == 0 and q_pos0 % tq == 0 and keys % tk == 0 and v_all.shape[1] == keys
    later = jnp.tril(jnp.ones((tk, tk), BF16), k=-1)
    top = jnp.concatenate([later, jnp.ones((tk, tk), BF16)], axis=1)
    sums_rhs = jnp.concatenate([top, top], axis=0)
    return pl.pallas_call(
        functools.partial(_sb_kernel, q_pos0=q_pos0, tk=tk),
        grid=(b, t // tq),
        in_specs=[
            pl.BlockSpec((None, tq, w), lambda bi, qi: (bi, qi, COL_Q // w)),
            pl.BlockSpec((None, w, keys), lambda bi, qi: (bi, 0, 0)),
            pl.BlockSpec((None, keys, w), lambda bi, qi: (bi, 0, 0)),
            pl.BlockSpec((None, tq, w), lambda bi, qi: (bi, qi, COL_SB_G // w)),
            pl.BlockSpec((2 * tk, 2 * tk), lambda bi, qi: (0, 0)),
        ],
        out_specs=pl.BlockSpec((None, tq, w), lambda bi, qi: (bi, qi, 0)),
        out_shape=jax.ShapeDtypeStruct((b, t, w), BF16),
        scratch_shapes=[
            pltpu.VMEM((w // LANES, LANES // SB_DH, tq, tk), F32),
            pltpu.VMEM((w // LANES, tq, LANES), F32),
        ],
        compiler_params=_cparams("parallel", "arbitrary"),
        name="sb_branch",
    )(p3, kt_all, v_all, p3, sums_rhs)


def _mix_kernel(ua_ref, ub_ref, uc_ref, ud_ref, m0_ref, m1_ref, m2_ref, m3_ref, x_ref, wd_ref, wo_ref, g_ref, o_ref):
    us = (ua_ref, ub_ref, uc_ref, ud_ref)
    ms = (m0_ref, m1_ref, m2_ref, m3_ref)
    mixed = None
    for i in range(N_BRANCH):
        y = _dot(us[i][...], wd_ref[i])
        term = _sigmoid(ms[i][...].astype(F32)) * y
        mixed = term if mixed is None else mixed + term
    o = _dot(mixed.astype(BF16), wo_ref[...])
    ms2 = jnp.mean(o * o, axis=-1, keepdims=True)
    o_ref[...] = x_ref[...] + o * lax.rsqrt(ms2 + EPS) * g_ref[...]


def _mix(us, p2, x2, pw):
    n, d = x2.shape
    w = us[0].shape[-1]
    tm = min(n, 512)
    ublk = pl.BlockSpec((tm, w), lambda i: (i, 0))
    mblk = lambda j: pl.BlockSpec((tm, d), lambda i: (i, COL_MERGE // d + j))
    return pl.pallas_call(
        _mix_kernel,
        grid=(n // tm,),
        in_specs=[ublk] * 4 + [mblk(j) for j in range(N_BRANCH)] + [
            pl.BlockSpec((tm, d), lambda i: (i, 0)),
            pl.BlockSpec((N_BRANCH, w, d), lambda i: (0, 0, 0)),
            pl.BlockSpec((d, d), lambda i: (0, 0)),
            pl.BlockSpec((1, d), lambda i: (0, 0)),
        ],
        out_specs=pl.BlockSpec((tm, d), lambda i: (i, 0)),
        out_shape=jax.ShapeDtypeStruct((n, d), F32),
        input_output_aliases={len(us) + N_BRANCH: 0},
        compiler_params=_cparams("parallel"),
        name="mix",
    )(*us, p2, p2, p2, p2, x2, pw["w_down"], pw["w_out"], pw["norm_post"])


def _block_diag(wb):
    n, c, d = wb.shape
    eye = jnp.eye(n, dtype=wb.dtype)
    return (eye[:, None, :, None] * wb[:, :, None, :]).reshape(n * c, n * d)


def _prep_layer(p):
    d_model = p["w_in"].shape[0]
    bw = p["lru_lambda"].shape[-1]
    cdim = p["ssd_conv_b"].shape[-1]
    n_heads = p["ssd_dt_bias"].shape[-1]
    sizes = (bw, bw, bw, cdim, n_heads, 2 * bw, bw, 3 * bw, bw, N_BRANCH * d_model)
    offs = [0]
    for s in sizes:
        offs.append(offs[-1] + s)
    seg = lambda i: p["w_in"][:, offs[i]:offs[i + 1]]
    lru_x, lru_g, ssd_z, xbc, dt, cf_in, cf_g, qkv, sb_g, merge = [seg(i) for i in range(len(sizes))]
    q, k, v = qkv[:, :bw], qkv[:, bw:2 * bw], qkv[:, 2 * bw:]
    zeros = lambda c: jnp.zeros((d_model, c), p["w_in"].dtype)
    main = jnp.concatenate(
        [xbc, cf_in, lru_x, lru_g, ssd_z, cf_g, q, sb_g, merge, dt, zeros(LANES - n_heads),
         zeros(P_COLS - COL_DT - LANES)], axis=1)
    assert main.shape[1] == P_COLS
    pad_h = lambda a, fill: jnp.concatenate([a, jnp.full((LANES - n_heads,), fill, a.dtype)])[None, :]
    row = lambda a: a[None, :].astype(F32)
    return dict(
        w_main=main.astype(BF16), w_kt=k.T.astype(BF16), w_vt=v.T.astype(BF16), w_v=v.astype(BF16),
        norm_pre=row(p["norm_pre"]), norm_post=row(p["norm_post"]),
        lru_conv_w=p["lru_conv_w"], lru_conv_b=row(p["lru_conv_b"]),
        lru_wa=_block_diag(p["lru_wa"]).astype(BF16), lru_ba=row(p["lru_ba"]),
        lru_wx=_block_diag(p["lru_wx"]).astype(BF16), lru_bx=row(p["lru_bx"]),
        lru_lambda=row(p["lru_lambda"]),
        ssd_conv_w=p["ssd_conv_w"], ssd_conv_b=row(p["ssd_conv_b"]),
        ssd_dt_bias=pad_h(p["ssd_dt_bias"], 0.0), ssd_a_log=pad_h(p["ssd_a_log"], 0.0),
        ssd_d=jnp.repeat(p["ssd_d"], SSD_P)[None, :], ssd_norm=row(p["ssd_norm"]),
        cf_conv_w=p["cf_conv_w"], cf_conv_b=row(p["cf_conv_b"]),
        cf_ln_g=row(p["cf_ln_g"]), cf_ln_b=row(p["cf_ln_b"]),
        w_down=p["w_down"].astype(BF16), w_out=p["w_out"].astype(BF16),
    )


def _ssd_state_in(s):
    b = s.shape[0]
    s = s.reshape(b, SSD_G, SSD_R, SSD_P, SSD_N)
    return jnp.transpose(s, (0, 1, 4, 2, 3)).reshape(b, SSD_G, SSD_N, SSD_R * SSD_P)


def _ssd_state_out(s):
    b = s.shape[0]
    s = s.reshape(b, SSD_G, SSD_N, SSD_R, SSD_P)
    return jnp.transpose(s, (0, 1, 3, 4, 2)).reshape(b, SSD_G * SSD_R, SSD_P, SSD_N)


def _layer(x, pw, st, past_k, past_v, k_stack, v_stack, layer, *, seq_tile, ssd_chunk, sb_block):
    b, t, d = x.shape
    x2 = x.reshape(b * t, d)
    p2 = _inproj(x2, pw["norm_pre"], pw["w_main"], BF16, P_TILE_N)
    k_stack, v_stack, kt_new, v_new = _kvproj(x, pw["norm_pre"], pw["w_kt"], pw["w_vt"], pw["w_v"],
                                              k_stack, v_stack, layer)
    p3 = p2.reshape(b, t, P_COLS)

    u_a, lru_h, lru_conv = _lru_branch(p3, st["lru_conv"], st["lru_h"][:, None, :], pw, seq_tile)
    u_b, ssd_new, ssd_conv = _ssd_branch(p3, st["ssd_conv"], _ssd_state_in(st["ssd"]), pw, ssd_chunk)
    u_c, cf_conv = _cf_branch(p3, st["cf_conv"], pw, seq_tile)
    if past_k is None:
        kt_all, v_all, q_pos0 = kt_new, v_new, 0
    else:
        q_pos0 = past_k.shape[1]
        pad = -(q_pos0 + t) % SB_KEY_BLOCK
        past_kt = jnp.transpose(past_k, (0, 2, 3, 1)).reshape(b, -1, q_pos0).astype(BF16)
        past_vb = past_v.reshape(b, q_pos0, -1).astype(BF16)
        kt_all = jnp.pad(jnp.concatenate([past_kt, kt_new], axis=2), ((0, 0), (0, 0), (0, pad)))
        v_all = jnp.pad(jnp.concatenate([past_vb, v_new], axis=1), ((0, 0), (0, pad), (0, 0)))
    u_d = _sb_branch(p3, kt_all, v_all, q_pos0, sb_block)

    w = u_a.shape[-1]
    us = [u.reshape(b * t, w) for u in (u_a, u_b, u_c, u_d)]
    y = _mix(us, p2, x2, pw).reshape(b, t, d)
    new = dict(lru_h=lru_h[:, 0, :], lru_conv=lru_conv, ssd=_ssd_state_out(ssd_new), ssd_conv=ssd_conv,
               cf_conv=cf_conv)
    return y, new, k_stack, v_stack


_STATE_KEYS = ("lru_h", "lru_conv", "ssd", "ssd_conv", "cf_conv")


def kernel(x_prompt, x_sample, state_lru_h, state_lru_conv, state_ssd, state_ssd_conv, state_cf_conv, cache_sb_k, cache_sb_v, norm_pre, norm_post, w_in, lru_conv_w, lru_conv_b, lru_wa, lru_ba, lru_wx, lru_bx, lru_lambda, ssd_conv_w, ssd_conv_b, ssd_dt_bias, ssd_a_log, ssd_d, ssd_norm, cf_conv_w, cf_conv_b, cf_ln_g, cf_ln_b, w_down, w_out):
    raw = dict(norm_pre=norm_pre, norm_post=norm_post, w_in=w_in, lru_conv_w=lru_conv_w, lru_conv_b=lru_conv_b,
               lru_wa=lru_wa, lru_ba=lru_ba, lru_wx=lru_wx, lru_bx=lru_bx, lru_lambda=lru_lambda,
               ssd_conv_w=ssd_conv_w, ssd_conv_b=ssd_conv_b, ssd_dt_bias=ssd_dt_bias, ssd_a_log=ssd_a_log,
               ssd_d=ssd_d, ssd_norm=ssd_norm, cf_conv_w=cf_conv_w, cf_conv_b=cf_conv_b, cf_ln_g=cf_ln_g,
               cf_ln_b=cf_ln_b, w_down=w_down, w_out=w_out)
    pws = jax.vmap(_prep_layer)(raw)

    bp = x_prompt.shape[0]
    dt = x_prompt.dtype
    bw = state_lru_h.shape[-1]
    zero_state = dict(lru_h=jnp.zeros((bp, bw), dt),
                      lru_conv=jnp.zeros((bp,) + state_lru_conv.shape[2:], dt),
                      ssd=jnp.zeros((bp,) + state_ssd.shape[2:], dt),
                      ssd_conv=jnp.zeros((bp,) + state_ssd_conv.shape[2:], dt),
                      cf_conv=jnp.zeros((bp,) + state_cf_conv.shape[2:], dt))

    depth = w_in.shape[0]
    layers = jnp.arange(depth, dtype=jnp.int32)

    def kv_stacks(x):
        shape = (depth, x.shape[0], SB_H * SB_DH, x.shape[1])
        return jnp.zeros(shape, F32), jnp.zeros(shape, F32)

    def kv_out(stack):
        d_, b_, _, t_ = stack.shape
        return jnp.transpose(stack.reshape(d_, b_, SB_H, SB_DH, t_), (0, 1, 4, 2, 3))

    def prompt_step(carry, xs):
        x, k_stack, v_stack = carry
        pw, layer = xs
        y, new, k_stack, v_stack = _layer(x, pw, zero_state, None, None, k_stack, v_stack, layer,
                                          seq_tile=512, ssd_chunk=128, sb_block=128)
        return (y, k_stack, v_stack), tuple(new[k] for k in _STATE_KEYS)

    (y_p, k_p, v_p), p_new = lax.scan(prompt_step, (x_prompt,) + kv_stacks(x_prompt), (pws, layers))

    t_s = x_sample.shape[1]

    def sample_step(carry, xs):
        x, k_stack, v_stack = carry
        pw, st, pk, pv, layer = xs
        y, new, k_stack, v_stack = _layer(x, pw, st, pk, pv, k_stack, v_stack, layer,
                                          seq_tile=t_s, ssd_chunk=t_s, sb_block=t_s)
        return (y, k_stack, v_stack), tuple(new[k] for k in _STATE_KEYS)

    st_s = dict(lru_h=state_lru_h, lru_conv=state_lru_conv, ssd=state_ssd, ssd_conv=state_ssd_conv,
                cf_conv=state_cf_conv)
    (y_s, k_s, v_s), s_new = lax.scan(sample_step, (x_sample,) + kv_stacks(x_sample),
                                      (pws, st_s, cache_sb_k, cache_sb_v, layers))

    outs = [y_p, y_s]
    for i in range(len(_STATE_KEYS)):
        outs += [p_new[i], s_new[i]]
    outs += [kv_out(k_p), kv_out(k_s), kv_out(v_p), kv_out(v_s)]
    return tuple(outs)
```

```python
import functools

import jax
import jax.numpy as jnp
from jax import lax
from jax.experimental import pallas as pl
from jax.experimental.pallas import tpu as pltpu

F32 = jnp.float32
BF16 = jnp.bfloat16

EPS = 1e-6
LRU_C = 8.0
N_BRANCH = 4
LRU_BLOCKS = 8
SSD_P = 64
SSD_N = 64
SSD_G = 4
SSD_R = 2
SB_H = 8
SB_DH = 64
SB_KEY_BLOCK = 128

LANES = 128
SUBLANES = 8
VMEM_LIMIT_BYTES = 56 * 1024 * 1024

F32_EXP_ZERO_BELOW = -104.0
NEG_BIG = -1e30

COL_XBC = 0
COL_CF_IN = 1024
COL_LRU_X = 2048
COL_LRU_G = 2560
COL_SSD_Z = 3072
COL_CF_G = 3584
COL_Q = 4096
COL_SB_G = 4608
COL_MERGE = 5120
COL_DT = 9216
P_COLS = 9472
P_TILE_N = P_COLS // 2
INPROJ_CHUNK = 512


def _cparams(*sem):
    return pltpu.CompilerParams(dimension_semantics=sem, vmem_limit_bytes=VMEM_LIMIT_BYTES)


def _sigmoid(x):
    return 0.5 * jnp.tanh(0.5 * x) + 0.5


def _silu(x):
    return x * _sigmoid(x)


def _softplus(x):
    return jnp.maximum(x, 0.0) + jnp.log(1.0 + jnp.exp(-jnp.abs(x)))


def _split3(x):
    hi = x.astype(BF16)
    r = x - hi.astype(F32)
    mid = r.astype(BF16)
    lo = (r - mid.astype(F32)).astype(BF16)
    return hi, mid, lo


def _dot(a, b):
    return jnp.dot(a, b, preferred_element_type=F32)


def _dot_nt(a, b):
    return lax.dot_general(a, b, (((1,), (1,)), ((), ())), preferred_element_type=F32)


def _sel_left(m01, x):
    hi, mid, lo = _split3(x)
    return _dot(m01, hi) + _dot(m01, mid) + _dot(m01, lo)


def _sel_right(x, m01):
    hi, mid, lo = _split3(x)
    return _dot(hi, m01) + _dot(mid, m01) + _dot(lo, m01)


def _sel_nt(m01, x):
    hi, mid, lo = _split3(x)
    return _dot_nt(m01, hi) + _dot_nt(m01, mid) + _dot_nt(m01, lo)


def _iota(shape, dim):
    return lax.broadcasted_iota(jnp.int32, shape, dim)


def _div_pow2(x, n):
    shift = n.bit_length() - 1
    assert 1 << shift == n
    return lax.shift_right_logical(x, shift)


def _inproj_kernel(x_ref, g_ref, w_ref, o_ref, h_ref):
    @pl.when(pl.program_id(1) == 0)
    def _():
        x = x_ref[...]
        ms = jnp.mean(x * x, axis=-1, keepdims=True)
        h_ref[...] = (x * lax.rsqrt(ms + EPS) * g_ref[...]).astype(BF16)

    h = h_ref[...]
    tn = o_ref.shape[1]
    for c0 in range(0, tn, INPROJ_CHUNK):
        c1 = min(c0 + INPROJ_CHUNK, tn)
        o_ref[:, c0:c1] = _dot(h, w_ref[:, c0:c1]).astype(o_ref.dtype)


def _inproj(x2, g, w, out_dtype, tn):
    n, d = x2.shape
    cols = w.shape[1]
    tm = min(n, 1024)
    return pl.pallas_call(
        _inproj_kernel,
        grid=(n // tm, cols // tn),
        in_specs=[
            pl.BlockSpec((tm, d), lambda i, j: (i, 0)),
            pl.BlockSpec((1, d), lambda i, j: (0, 0)),
            pl.BlockSpec((d, tn), lambda i, j: (0, j)),
        ],
        out_specs=pl.BlockSpec((tm, tn), lambda i, j: (i, j)),
        out_shape=jax.ShapeDtypeStruct((n, cols), out_dtype),
        scratch_shapes=[pltpu.VMEM((tm, d), BF16)],
        compiler_params=_cparams("parallel", "arbitrary"),
        name="inproj_" + jnp.dtype(out_dtype).name,
    )(x2, g, w)


def _kv_kernel(layer_ref, x_ref, g_ref, wkt_ref, wvt_ref, kin_ref, vin_ref, k_ref, v_ref, kb_ref, vb_ref):
    del layer_ref, kin_ref, vin_ref
    x = x_ref[...]
    ms = jnp.mean(x * x, axis=-1, keepdims=True)
    h = (x * lax.rsqrt(ms + EPS) * g_ref[...]).astype(BF16)
    kt = _dot_nt(wkt_ref[...], h)
    vt = _dot_nt(wvt_ref[...], h)
    k_ref[...] = kt
    v_ref[...] = vt
    kb_ref[...] = kt.astype(BF16)
    vb_ref[...] = vt.astype(BF16)


def _kvproj(x, g, wkt, wvt, k_stack, v_stack, layer):
    b, t, d = x.shape
    w = wkt.shape[0]
    tm = min(t, 1024)
    stack_blk = pl.BlockSpec((None, None, w, tm), lambda bi, ti, l: (l[0], bi, 0, ti))
    full = lambda shape: pl.BlockSpec(shape, lambda bi, ti, l: (0,) * len(shape))
    grid_spec = pltpu.PrefetchScalarGridSpec(
        num_scalar_prefetch=1,
        grid=(b, t // tm),
        in_specs=[
            pl.BlockSpec((None, tm, d), lambda bi, ti, l: (bi, ti, 0)),
            full((1, d)), full((w, d)), full((w, d)),
            pl.BlockSpec(memory_space=pl.ANY),
            pl.BlockSpec(memory_space=pl.ANY),
        ],
        out_specs=[stack_blk, stack_blk] + [pl.BlockSpec((None, w, tm), lambda bi, ti, l: (bi, 0, ti))] * 2,
    )
    return pl.pallas_call(
        _kv_kernel,
        grid_spec=grid_spec,
        out_shape=[jax.ShapeDtypeStruct(k_stack.shape, F32), jax.ShapeDtypeStruct(v_stack.shape, F32)]
        + [jax.ShapeDtypeStruct((b, w, t), BF16)] * 2,
        input_output_aliases={5: 0, 6: 1},
        compiler_params=_cparams("arbitrary", "arbitrary"),
        name="kvproj",
    )(jnp.reshape(layer, (1,)).astype(jnp.int32), x, g, wkt, wvt, k_stack, v_stack)


def _conv_taps(xe_ref, w_ref, b_ref, base, rows, hist_off, k):
    acc = None
    for j in range(k):
        term = w_ref[j:j + 1, :] * xe_ref[pl.ds(base + hist_off + j, rows), :]
        acc = term if acc is None else acc + term
    return acc + b_ref[...]


def _conv_short(x_cur, hist_ref, w_ref, b_ref):
    k = w_ref.shape[0]
    first_row = _iota((SUBLANES, x_cur.shape[1]), 0) == 0
    delayed = x_cur
    acc = w_ref[k - 1:k, :] * x_cur
    for j in range(k - 2, -1, -1):
        rolled = pltpu.roll(delayed, 1, axis=0)
        head = jnp.where(first_row, hist_ref[j:j + 1, :], rolled[:SUBLANES])
        delayed = jnp.concatenate([head, rolled[SUBLANES:]], axis=0)
        acc = acc + w_ref[j:j + 1, :] * delayed
    return acc + b_ref[...]


LRU_HIST = 8


def _lru_kernel(x_ref, g_ref, cs_ref, h0_ref, cw_ref, cb_ref, wa_ref, ba_ref, wx_ref, bx_ref, lam_ref,
                u_ref, hl_ref, cn_ref, xe_ref, a_ref, b_ref, h_ref):
    tt, w = x_ref.shape
    kc = cw_ref.shape[0]

    @pl.when(pl.program_id(1) == 0)
    def _():
        xe_ref[...] = cs_ref[...]
        h_ref[...] = h0_ref[...]

    x_cur = x_ref[...].astype(F32)
    xc = _conv_short(x_cur, xe_ref, cw_ref, cb_ref)
    tail = x_cur[tt - (kc - 1):, :]
    xe_ref[...] = tail
    cn_ref[...] = tail

    xcb = xc.astype(BF16)
    gate_a = _dot(xcb, wa_ref[...]) + ba_ref[...]
    gate_x = _dot(xcb, wx_ref[...]) + bx_ref[...]
    r = _sigmoid(gate_a)
    i = _sigmoid(gate_x)
    log_a = (-LRU_C) * r * _softplus(-lam_ref[...])
    a = jnp.exp(log_a)
    one_m_a2 = (1.0 - a) * (1.0 + a)
    a_ref[...] = a
    root = jnp.where(one_m_a2 > 0.0, one_m_a2 * lax.rsqrt(one_m_a2), 0.0)
    b_ref[...] = root * (i * xc)

    row = _iota((SUBLANES, w), 0)

    def body(s, h):
        off = pl.multiple_of(s * SUBLANES, SUBLANES)
        a8 = a_ref[pl.ds(off, SUBLANES), :]
        b8 = b_ref[pl.ds(off, SUBLANES), :]
        for d in (1, 2, 4):
            keep = row >= d
            a_sh = pltpu.roll(a8, d, axis=0)
            b_sh = pltpu.roll(b8, d, axis=0)
            b8 = jnp.where(keep, a8 * b_sh + b8, b8)
            a8 = jnp.where(keep, a8 * a_sh, a8)
        h8 = a8 * h + b8
        b_ref[pl.ds(off, SUBLANES), :] = h8
        return h8[SUBLANES - 1:SUBLANES, :]

    h_last = lax.fori_loop(0, tt // SUBLANES, body, h_ref[...])
    h_ref[...] = h_last
    hl_ref[...] = h_last
    u_ref[...] = (b_ref[...] * _silu(g_ref[...].astype(F32))).astype(u_ref.dtype)


def _lru_branch(p3, conv_state, h0, pw, tt):
    b, t, _ = p3.shape
    w = conv_state.shape[-1]
    kc = pw["lru_conv_w"].shape[0]
    col = lambda off: off // w
    full = lambda shape: pl.BlockSpec(shape, lambda bi, ti: (0,) * len(shape))
    return pl.pallas_call(
        _lru_kernel,
        grid=(b, t // tt),
        in_specs=[
            pl.BlockSpec((None, tt, w), lambda bi, ti: (bi, ti, col(COL_LRU_X))),
            pl.BlockSpec((None, tt, w), lambda bi, ti: (bi, ti, col(COL_LRU_G))),
            pl.BlockSpec((None, kc - 1, w), lambda bi, ti: (bi, 0, 0)),
            pl.BlockSpec((None, 1, w), lambda bi, ti: (bi, 0, 0)),
            full((kc, w)), full((1, w)), full((w, w)), full((1, w)), full((w, w)), full((1, w)), full((1, w)),
        ],
        out_specs=[
            pl.BlockSpec((None, tt, w), lambda bi, ti: (bi, ti, 0)),
            pl.BlockSpec((None, 1, w), lambda bi, ti: (bi, 0, 0)),
            pl.BlockSpec((None, kc - 1, w), lambda bi, ti: (bi, 0, 0)),
        ],
        out_shape=[
            jax.ShapeDtypeStruct((b, t, w), BF16),
            jax.ShapeDtypeStruct((b, 1, w), F32),
            jax.ShapeDtypeStruct((b, kc - 1, w), F32),
        ],
        scratch_shapes=[
            pltpu.VMEM((kc - 1, w), F32),
            pltpu.VMEM((tt, w), F32),
            pltpu.VMEM((tt, w), F32),
            pltpu.VMEM((1, w), F32),
        ],
        compiler_params=_cparams("parallel", "arbitrary"),
        name="lru_branch",
    )(p3, p3, conv_state, h0, pw["lru_conv_w"], pw["lru_conv_b"], pw["lru_wa"], pw["lru_ba"],
      pw["lru_wx"], pw["lru_bx"], pw["lru_lambda"])


def _ssd_kernel(z_ref, xbc_ref, dt_ref, cs_ref, s0_ref, cw_ref, cb_ref, dtb_ref, alog_ref, dvec_ref, nw_ref,
                u_ref, sn_ref, cn_ref, xe_ref, st_ref, *, L):
    tile, inner = z_ref.shape
    kc = cw_ref.shape[0]

    @pl.when(pl.program_id(1) == 0)
    def _():
        xe_ref[...] = cs_ref[...]
        st_ref[...] = s0_ref[...]

    x_cur = xbc_ref[...].astype(F32)
    xa_all = _silu(_conv_short(x_cur, xe_ref, cw_ref, cb_ref))
    tail = x_cur[tile - (kc - 1):, :]
    xe_ref[...] = tail
    cn_ref[...] = tail

    dt_all = _softplus(dt_ref[...].astype(F32) + dtb_ref[...])
    adt_all = dt_all * (-jnp.exp(alog_ref[...]))

    for c in range(tile // L):
        rows = slice(c * L, (c + 1) * L)
        y = _ssd_chunk(xa_all[rows], dt_all[rows], adt_all[rows], st_ref, dvec_ref, inner, L)
        gated = y * _silu(z_ref[rows, :].astype(F32))
        ms2 = jnp.mean(gated * gated, axis=-1, keepdims=True)
        u_ref[rows, :] = (gated * lax.rsqrt(ms2 + EPS) * nw_ref[...]).astype(u_ref.dtype)
    sn_ref[...] = st_ref[...]


def _ssd_chunk(xa, dt, adt, st_ref, dvec_ref, inner, L):
    gw = SSD_R * SSD_P
    x = xa[:, :inner]
    bm = xa[:, inner:inner + SSD_G * SSD_N]
    cm = xa[:, inner + SSD_G * SSD_N:]

    causal = _iota((L, L), 0) >= _iota((L, L), 1)
    tril = causal.astype(BF16)
    eye = (_iota((LANES, LANES), 0) == _iota((LANES, LANES), 1)).astype(BF16)
    eye_n = eye[:SSD_N, :SSD_N]
    expand_c = (_div_pow2(_iota((LANES, inner), 1), SSD_P) == _iota((LANES, inner), 0)).astype(BF16)
    n_heads = inner // SSD_P
    expand_l = (_div_pow2(_iota((LANES, n_heads * L), 1), L) == _iota((LANES, n_heads * L), 0)).astype(BF16)

    acs = _sel_left(tril, adt)
    acs_t = _sel_nt(eye, acs)
    dt_c = _sel_right(dt, expand_c)
    acs_c = _sel_right(acs, expand_c)
    acs_w = _sel_right(acs, expand_l)

    xdt = x * dt_c
    last = acs_c[L - 1:L, :]
    xs = xdt * jnp.exp(last - acs_c)
    e_acs = jnp.exp(acs_c)
    chunk_decay = jnp.exp(last)
    lane = _iota((L, gw), 1)

    ys = []
    for g in range(SSD_G):
        cg = cm[:, g * SSD_N:(g + 1) * SSD_N].astype(BF16)
        bg = bm[:, g * SSD_N:(g + 1) * SSD_N].astype(BF16)
        cb = _dot_nt(cg, bg)
        ms = []
        for r in range(SSD_R):
            h = g * SSD_R + r
            seg = acs_w[:, h * L:(h + 1) * L] - acs_t[h:h + 1, :]
            dec = jnp.exp(jnp.where(causal, seg, NEG_BIG))
            ms.append((cb * dec).astype(BF16))
        mcat = jnp.concatenate(ms, axis=1)
        xg = xdt[:, g * gw:(g + 1) * gw]
        rhs = jnp.concatenate([jnp.where(lane < SSD_P, xg, 0.0), jnp.where(lane >= SSD_P, xg, 0.0)],
                              axis=0).astype(BF16)
        y_diag = _dot(mcat, rhs)
        st = st_ref[g]
        y_off = _dot(cg, st.astype(BF16)) * e_acs[:, g * gw:(g + 1) * gw]
        bg_t = _dot_nt(eye_n, bg).astype(BF16)
        st_ref[g] = st * chunk_decay[:, g * gw:(g + 1) * gw] + _dot(bg_t, xs[:, g * gw:(g + 1) * gw].astype(BF16))
        ys.append(y_diag + y_off)
    return jnp.concatenate(ys, axis=1) + dvec_ref[...] * x


SSD_CHUNKS_PER_STEP = 2


def _ssd_branch(p3, conv_state, st0, pw, L):
    b, t, _ = p3.shape
    tile = min(t, SSD_CHUNKS_PER_STEP * L)
    cdim = conv_state.shape[-1]
    inner = pw["ssd_norm"].shape[-1]
    kc = pw["ssd_conv_w"].shape[0]
    gw = SSD_R * SSD_P
    full = lambda shape: pl.BlockSpec(shape, lambda bi, ti: (0,) * len(shape))
    return pl.pallas_call(
        functools.partial(_ssd_kernel, L=L),
        grid=(b, t // tile),
        in_specs=[
            pl.BlockSpec((None, tile, inner), lambda bi, ti: (bi, ti, COL_SSD_Z // inner)),
            pl.BlockSpec((None, tile, cdim), lambda bi, ti: (bi, ti, COL_XBC // cdim)),
            pl.BlockSpec((None, tile, LANES), lambda bi, ti: (bi, ti, COL_DT // LANES)),
            pl.BlockSpec((None, kc - 1, cdim), lambda bi, ti: (bi, 0, 0)),
            pl.BlockSpec((None, SSD_G, SSD_N, gw), lambda bi, ti: (bi, 0, 0, 0)),
            full((kc, cdim)), full((1, cdim)), full((1, LANES)), full((1, LANES)), full((1, inner)), full((1, inner)),
        ],
        out_specs=[
            pl.BlockSpec((None, tile, inner), lambda bi, ti: (bi, ti, 0)),
            pl.BlockSpec((None, SSD_G, SSD_N, gw), lambda bi, ti: (bi, 0, 0, 0)),
            pl.BlockSpec((None, kc - 1, cdim), lambda bi, ti: (bi, 0, 0)),
        ],
        out_shape=[
            jax.ShapeDtypeStruct((b, t, inner), BF16),
            jax.ShapeDtypeStruct((b, SSD_G, SSD_N, gw), F32),
            jax.ShapeDtypeStruct((b, kc - 1, cdim), F32),
        ],
        scratch_shapes=[
            pltpu.VMEM((kc - 1, cdim), F32),
            pltpu.VMEM((SSD_G, SSD_N, gw), F32),
        ],
        compiler_params=_cparams("parallel", "arbitrary"),
        name="ssd_branch",
    )(p3, p3, p3, conv_state, st0, pw["ssd_conv_w"], pw["ssd_conv_b"], pw["ssd_dt_bias"], pw["ssd_a_log"],
      pw["ssd_d"], pw["ssd_norm"])


CF_HIST = 32
CF_ROWS = 128


def _cf_kernel(in_ref, g_ref, cs_ref, cw_ref, cb_ref, lg_ref, lb_ref, u_ref, cn_ref, xe_ref):
    tt = in_ref.shape[0]
    w = g_ref.shape[1]
    kc = cw_ref.shape[0]
    hist = CF_HIST - (kc - 1)

    @pl.when(pl.program_id(1) == 0)
    def _():
        xe_ref[hist:CF_HIST, :] = cs_ref[...]

    cf = in_ref[...].astype(F32)
    xe_ref[CF_HIST:CF_HIST + tt, :] = cf[:, :w] * _sigmoid(cf[:, w:])

    n_rows = min(CF_ROWS, tt)

    def body(s, carry):
        base = pl.multiple_of(s * n_rows, n_rows)
        c = None
        for r in range(SUBLANES):
            rows = n_rows + (SUBLANES if r else 0)
            part = None
            for o in range(hist + (r - hist) % SUBLANES, hist + kc, SUBLANES):
                term = cw_ref[o - hist:o - hist + 1, :] * xe_ref[pl.ds(base + (o - r), rows), :]
                part = term if part is None else part + term
            piece = part[r:r + n_rows, :]
            c = piece if c is None else c + piece
        c = c + cb_ref[...]
        mu = jnp.mean(c, axis=-1, keepdims=True)
        cc = c - mu
        var = jnp.mean(cc * cc, axis=-1, keepdims=True)
        ln = cc * lax.rsqrt(var + EPS) * lg_ref[...] + lb_ref[...]
        gate = g_ref[pl.ds(base, n_rows), :].astype(F32)
        u_ref[pl.ds(base, n_rows), :] = (_silu(ln) * _silu(gate)).astype(u_ref.dtype)
        return carry

    lax.fori_loop(0, tt // n_rows, body, 0)
    tail = xe_ref[tt + hist:tt + CF_HIST, :]
    xe_ref[hist:CF_HIST, :] = tail
    cn_ref[...] = tail


def _cf_branch(p3, conv_state, pw, tt):
    b, t, _ = p3.shape
    w = conv_state.shape[-1]
    kc = pw["cf_conv_w"].shape[0]
    full = lambda shape: pl.BlockSpec(shape, lambda bi, ti: (0,) * len(shape))
    return pl.pallas_call(
        _cf_kernel,
        grid=(b, t // tt),
        in_specs=[
            pl.BlockSpec((None, tt, 2 * w), lambda bi, ti: (bi, ti, COL_CF_IN // (2 * w))),
            pl.BlockSpec((None, tt, w), lambda bi, ti: (bi, ti, COL_CF_G // w)),
            pl.BlockSpec((None, kc - 1, w), lambda bi, ti: (bi, 0, 0)),
            full((kc, w)), full((1, w)), full((1, w)), full((1, w)),
        ],
        out_specs=[
            pl.BlockSpec((None, tt, w), lambda bi, ti: (bi, ti, 0)),
            pl.BlockSpec((None, kc - 1, w), lambda bi, ti: (bi, 0, 0)),
        ],
        out_shape=[
            jax.ShapeDtypeStruct((b, t, w), BF16),
            jax.ShapeDtypeStruct((b, kc - 1, w), F32),
        ],
        scratch_shapes=[pltpu.VMEM((tt + CF_HIST, w), F32)],
        compiler_params=_cparams("parallel", "arbitrary"),
        name="cf_branch",
    )(p3, p3, conv_state, pw["cf_conv_w"], pw["cf_conv_b"], pw["cf_ln_g"], pw["cf_ln_b"])


def _sb_kernel(q_ref, kt_ref, vt_ref, g_ref, sums_ref, u_ref, acc_ref, out_ref, *, q_pos0, tk):
    tq, w = q_ref.shape
    n_pairs = w // LANES
    per_pair = LANES // SB_DH
    q_start = q_pos0 + pl.program_id(1) * tq
    kd = (q_start + (tq - 1)) // tk

    strictly_earlier = (kd * tk + _iota((tq, tk), 1)) < (q_start + _iota((tq, tk), 0))
    sums_rhs = sums_ref[...]

    head_of_lane = _div_pow2(_iota((1, LANES), 1), SB_DH)
    lane_masks = [(head_of_lane == r).astype(BF16) for r in range(per_pair)]
    q_masks = [m * (SB_DH ** -0.5) for m in lane_masks]
    head_of_channel = _div_pow2(_iota((LANES, tk), 0), SB_DH)
    channel_masks = [(head_of_channel == r).astype(BF16) for r in range(per_pair)]
    qm = []
    for p in range(n_pairs):
        qp = q_ref[:, p * LANES:(p + 1) * LANES]
        qm.append([qp * m for m in q_masks])
    half = tq // 2

    acc_ref[...] = jnp.zeros_like(acc_ref)
    out_ref[...] = jnp.zeros_like(out_ref)

    def block(kb, masked, nrows):
        stack = lambda parts: jnp.concatenate([x[:nrows] for x in parts], axis=0)
        off = pl.multiple_of(kb * tk, tk)
        earlier = stack([strictly_earlier] * per_pair)
        zs, lsms, hls = [], [], []
        for p in range(n_pairs):
            kpt = kt_ref[p * LANES:(p + 1) * LANES, pl.ds(off, tk)]
            z = _dot(stack(qm[p]), kpt)
            lsm = -_softplus(z)
            if masked:
                lsm = jnp.where(earlier, lsm, 0.0)
            hi = lsm.astype(BF16)
            lo = (lsm - hi.astype(F32)).astype(BF16)
            zs.append(z)
            lsms.append(lsm)
            hls.append(jnp.concatenate([hi, lo], axis=1))
        sums = _dot(jnp.concatenate(hls, axis=0), sums_rhs)
        worst_top, worst_rest = None, None
        for p in range(n_pairs):
            rows = slice(p * per_pair * nrows, (p + 1) * per_pair * nrows)
            acc = stack([acc_ref[p, r] for r in range(per_pair)])
            lw = zs[p] + lsms[p] + sums[rows, :tk] + acc
            if masked:
                lw = jnp.where(earlier, lw, NEG_BIG)
            wgt = jnp.exp(lw).astype(BF16)
            acc = acc + sums[rows, tk:]
            for r in range(per_pair):
                acc_r = acc[r * nrows:(r + 1) * nrows]
                acc_ref[p, r, :nrows] = acc_r
                top = acc_r[:half]
                worst_top = top if worst_top is None else jnp.maximum(worst_top, top)
                if nrows > half:
                    rest = acc_r[half:]
                    worst_rest = rest if worst_rest is None else jnp.maximum(worst_rest, rest)
            vpt = vt_ref[p * LANES:(p + 1) * LANES, pl.ds(off, tk)]
            v2t = jnp.concatenate([vpt * m for m in channel_masks], axis=1)
            wcat = jnp.concatenate([wgt[r * nrows:(r + 1) * nrows] for r in range(per_pair)], axis=1)
            out_ref[p, :nrows] = out_ref[p, :nrows] + _dot_nt(wcat, v2t)
        if nrows > half:
            return jnp.max(worst_top), jnp.max(worst_rest)
        return jnp.max(worst_top)

    live = lambda kb, worst: jnp.logical_and(kb >= 0, worst >= F32_EXP_ZERO_BELOW)
    top0, rest0 = block(kd, True, tq)

    def full_body(c):
        return (c[0] - 1,) + block(c[0], False, tq)

    kb1, top1, _ = lax.while_loop(lambda c: live(c[0], c[2]), full_body, (kd - 1, top0, rest0))

    def half_body(c):
        return c[0] - 1, block(c[0], False, half)

    lax.while_loop(lambda c: live(c[0], c[1]), half_body, (kb1, top1))

    o = jnp.concatenate([out_ref[p] for p in range(n_pairs)], axis=1)
    u_ref[...] = (o * _silu(g_ref[...].astype(F32))).astype(u_ref.dtype)


def _sb_branch(p3, kt_all, vt_all, q_pos0, tq):
    b, t, _ = p3.shape
    w, keys = kt_all.shape[1:]
    tk = SB_KEY_BLOCK
    assert tk % tq == 0 and q_pos0 % tq == 0 and keys % tk == 0 and vt_all.shape == kt_all.shape
    later = jnp.tril(jnp.ones((tk, tk), BF16), k=-1)
    top = jnp.concatenate([later, jnp.ones((tk, tk), BF16)], axis=1)
    sums_rhs = jnp.concatenate([top, top], axis=0)
    return pl.pallas_call(
        functools.partial(_sb_kernel, q_pos0=q_pos0, tk=tk),
        grid=(b, t // tq),
        in_specs=[
            pl.BlockSpec((None, tq, w), lambda bi, qi: (bi, qi, COL_Q // w)),
            pl.BlockSpec((None, w, keys), lambda bi, qi: (bi, 0, 0)),
            pl.BlockSpec((None, w, keys), lambda bi, qi: (bi, 0, 0)),
            pl.BlockSpec((None, tq, w), lambda bi, qi: (bi, qi, COL_SB_G // w)),
            pl.BlockSpec((2 * tk, 2 * tk), lambda bi, qi: (0, 0)),
        ],
        out_specs=pl.BlockSpec((None, tq, w), lambda bi, qi: (bi, qi, 0)),
        out_shape=jax.ShapeDtypeStruct((b, t, w), BF16),
        scratch_shapes=[
            pltpu.VMEM((w // LANES, LANES // SB_DH, tq, tk), F32),
            pltpu.VMEM((w // LANES, tq, LANES), F32),
        ],
        compiler_params=_cparams("parallel", "arbitrary"),
        name="sb_branch",
    )(p3, kt_all, vt_all, p3, sums_rhs)


def _mix_kernel(ua_ref, ub_ref, uc_ref, ud_ref, m0_ref, m1_ref, m2_ref, m3_ref, x_ref, wd_ref, wo_ref, g_ref, o_ref):
    us = (ua_ref, ub_ref, uc_ref, ud_ref)
    ms = (m0_ref, m1_ref, m2_ref, m3_ref)
    mixed = None
    for i in range(N_BRANCH):
        y = _dot(us[i][...], wd_ref[i])
        term = _sigmoid(ms[i][...].astype(F32)) * y
        mixed = term if mixed is None else mixed + term
    o = _dot(mixed.astype(BF16), wo_ref[...])
    ms2 = jnp.mean(o * o, axis=-1, keepdims=True)
    o_ref[...] = x_ref[...] + o * lax.rsqrt(ms2 + EPS) * g_ref[...]


def _mix(us, p2, x2, pw):
    n, d = x2.shape
    w = us[0].shape[-1]
    tm = min(n, 512)
    ublk = pl.BlockSpec((tm, w), lambda i: (i, 0))
    mblk = lambda j: pl.BlockSpec((tm, d), lambda i: (i, COL_MERGE // d + j))
    return pl.pallas_call(
        _mix_kernel,
        grid=(n // tm,),
        in_specs=[ublk] * 4 + [mblk(j) for j in range(N_BRANCH)] + [
            pl.BlockSpec((tm, d), lambda i: (i, 0)),
            pl.BlockSpec((N_BRANCH, w, d), lambda i: (0, 0, 0)),
            pl.BlockSpec((d, d), lambda i: (0, 0)),
            pl.BlockSpec((1, d), lambda i: (0, 0)),
        ],
        out_specs=pl.BlockSpec((tm, d), lambda i: (i, 0)),
        out_shape=jax.ShapeDtypeStruct((n, d), F32),
        input_output_aliases={len(us) + N_BRANCH: 0},
        compiler_params=_cparams("parallel"),
        name="mix",
    )(*us, p2, p2, p2, p2, x2, pw["w_down"], pw["w_out"], pw["norm_post"])


def _block_diag(wb):
    n, c, d = wb.shape
    eye = jnp.eye(n, dtype=wb.dtype)
    return (eye[:, None, :, None] * wb[:, :, None, :]).reshape(n * c, n * d)


def _prep_layer(p):
    d_model = p["w_in"].shape[0]
    bw = p["lru_lambda"].shape[-1]
    cdim = p["ssd_conv_b"].shape[-1]
    n_heads = p["ssd_dt_bias"].shape[-1]
    sizes = (bw, bw, bw, cdim, n_heads, 2 * bw, bw, 3 * bw, bw, N_BRANCH * d_model)
    offs = [0]
    for s in sizes:
        offs.append(offs[-1] + s)
    seg = lambda i: p["w_in"][:, offs[i]:offs[i + 1]]
    lru_x, lru_g, ssd_z, xbc, dt, cf_in, cf_g, qkv, sb_g, merge = [seg(i) for i in range(len(sizes))]
    q, k, v = qkv[:, :bw], qkv[:, bw:2 * bw], qkv[:, 2 * bw:]
    zeros = lambda c: jnp.zeros((d_model, c), p["w_in"].dtype)
    main = jnp.concatenate(
        [xbc, cf_in, lru_x, lru_g, ssd_z, cf_g, q, sb_g, merge, dt, zeros(LANES - n_heads),
         zeros(P_COLS - COL_DT - LANES)], axis=1)
    assert main.shape[1] == P_COLS
    pad_h = lambda a, fill: jnp.concatenate([a, jnp.full((LANES - n_heads,), fill, a.dtype)])[None, :]
    row = lambda a: a[None, :].astype(F32)
    return dict(
        w_main=main.astype(BF16), w_kt=k.T.astype(BF16), w_vt=v.T.astype(BF16),
        norm_pre=row(p["norm_pre"]), norm_post=row(p["norm_post"]),
        lru_conv_w=p["lru_conv_w"], lru_conv_b=row(p["lru_conv_b"]),
        lru_wa=_block_diag(p["lru_wa"]).astype(BF16), lru_ba=row(p["lru_ba"]),
        lru_wx=_block_diag(p["lru_wx"]).astype(BF16), lru_bx=row(p["lru_bx"]),
        lru_lambda=row(p["lru_lambda"]),
        ssd_conv_w=p["ssd_conv_w"], ssd_conv_b=row(p["ssd_conv_b"]),
        ssd_dt_bias=pad_h(p["ssd_dt_bias"], 0.0), ssd_a_log=pad_h(p["ssd_a_log"], 0.0),
        ssd_d=jnp.repeat(p["ssd_d"], SSD_P)[None, :], ssd_norm=row(p["ssd_norm"]),
        cf_conv_w=p["cf_conv_w"], cf_conv_b=row(p["cf_conv_b"]),
        cf_ln_g=row(p["cf_ln_g"]), cf_ln_b=row(p["cf_ln_b"]),
        w_down=p["w_down"].astype(BF16), w_out=p["w_out"].astype(BF16),
    )


def _ssd_state_in(s):
    b = s.shape[0]
    s = s.reshape(b, SSD_G, SSD_R, SSD_P, SSD_N)
    return jnp.transpose(s, (0, 1, 4, 2, 3)).reshape(b, SSD_G, SSD_N, SSD_R * SSD_P)


def _ssd_state_out(s):
    b = s.shape[0]
    s = s.reshape(b, SSD_G, SSD_N, SSD_R, SSD_P)
    return jnp.transpose(s, (0, 1, 3, 4, 2)).reshape(b, SSD_G * SSD_R, SSD_P, SSD_N)


def _layer(x, pw, st, past_k, past_v, k_stack, v_stack, layer, *, seq_tile, ssd_chunk, sb_block):
    b, t, d = x.shape
    x2 = x.reshape(b * t, d)
    p2 = _inproj(x2, pw["norm_pre"], pw["w_main"], BF16, P_TILE_N)
    k_stack, v_stack, kt_new, vt_new = _kvproj(x, pw["norm_pre"], pw["w_kt"], pw["w_vt"], k_stack, v_stack, layer)
    p3 = p2.reshape(b, t, P_COLS)

    u_a, lru_h, lru_conv = _lru_branch(p3, st["lru_conv"], st["lru_h"][:, None, :], pw, seq_tile)
    u_b, ssd_new, ssd_conv = _ssd_branch(p3, st["ssd_conv"], _ssd_state_in(st["ssd"]), pw, ssd_chunk)
    u_c, cf_conv = _cf_branch(p3, st["cf_conv"], pw, seq_tile)
    if past_k is None:
        kt_all, vt_all, q_pos0 = kt_new, vt_new, 0
    else:
        q_pos0 = past_k.shape[1]
        pad = -(q_pos0 + t) % SB_KEY_BLOCK

        def with_cache(past, new):
            past_t = jnp.transpose(past, (0, 2, 3, 1)).reshape(b, -1, q_pos0).astype(BF16)
            return jnp.pad(jnp.concatenate([past_t, new], axis=2), ((0, 0), (0, 0), (0, pad)))

        kt_all, vt_all = with_cache(past_k, kt_new), with_cache(past_v, vt_new)
    u_d = _sb_branch(p3, kt_all, vt_all, q_pos0, sb_block)

    w = u_a.shape[-1]
    us = [u.reshape(b * t, w) for u in (u_a, u_b, u_c, u_d)]
    y = _mix(us, p2, x2, pw).reshape(b, t, d)
    new = dict(lru_h=lru_h[:, 0, :], lru_conv=lru_conv, ssd=_ssd_state_out(ssd_new), ssd_conv=ssd_conv,
               cf_conv=cf_conv)
    return y, new, k_stack, v_stack


_STATE_KEYS = ("lru_h", "lru_conv", "ssd", "ssd_conv", "cf_conv")


def kernel(x_prompt, x_sample, state_lru_h, state_lru_conv, state_ssd, state_ssd_conv, state_cf_conv, cache_sb_k, cache_sb_v, norm_pre, norm_post, w_in, lru_conv_w, lru_conv_b, lru_wa, lru_ba, lru_wx, lru_bx, lru_lambda, ssd_conv_w, ssd_conv_b, ssd_dt_bias, ssd_a_log, ssd_d, ssd_norm, cf_conv_w, cf_conv_b, cf_ln_g, cf_ln_b, w_down, w_out):
    raw = dict(norm_pre=norm_pre, norm_post=norm_post, w_in=w_in, lru_conv_w=lru_conv_w, lru_conv_b=lru_conv_b,
               lru_wa=lru_wa, lru_ba=lru_ba, lru_wx=lru_wx, lru_bx=lru_bx, lru_lambda=lru_lambda,
               ssd_conv_w=ssd_conv_w, ssd_conv_b=ssd_conv_b, ssd_dt_bias=ssd_dt_bias, ssd_a_log=ssd_a_log,
               ssd_d=ssd_d, ssd_norm=ssd_norm, cf_conv_w=cf_conv_w, cf_conv_b=cf_conv_b, cf_ln_g=cf_ln_g,
               cf_ln_b=cf_ln_b, w_down=w_down, w_out=w_out)
    pws = jax.vmap(_prep_layer)(raw)

    bp = x_prompt.shape[0]
    dt = x_prompt.dtype
    bw = state_lru_h.shape[-1]
    zero_state = dict(lru_h=jnp.zeros((bp, bw), dt),
                      lru_conv=jnp.zeros((bp,) + state_lru_conv.shape[2:], dt),
                      ssd=jnp.zeros((bp,) + state_ssd.shape[2:], dt),
                      ssd_conv=jnp.zeros((bp,) + state_ssd_conv.shape[2:], dt),
                      cf_conv=jnp.zeros((bp,) + state_cf_conv.shape[2:], dt))

    depth = w_in.shape[0]
    layers = jnp.arange(depth, dtype=jnp.int32)

    def kv_stacks(x):
        shape = (depth, x.shape[0], SB_H * SB_DH, x.shape[1])
        return jnp.zeros(shape, F32), jnp.zeros(shape, F32)

    def kv_out(stack):
        d_, b_, _, t_ = stack.shape
        return jnp.transpose(stack.reshape(d_, b_, SB_H, SB_DH, t_), (0, 1, 4, 2, 3))

    def prompt_step(carry, xs):
        x, k_stack, v_stack = carry
        pw, layer = xs
        y, new, k_stack, v_stack = _layer(x, pw, zero_state, None, None, k_stack, v_stack, layer,
                                          seq_tile=512, ssd_chunk=128, sb_block=128)
        return (y, k_stack, v_stack), tuple(new[k] for k in _STATE_KEYS)

    (y_p, k_p, v_p), p_new = lax.scan(prompt_step, (x_prompt,) + kv_stacks(x_prompt), (pws, layers))

    t_s = x_sample.shape[1]

    def sample_step(carry, xs):
        x, k_stack, v_stack = carry
        pw, st, pk, pv, layer = xs
        y, new, k_stack, v_stack = _layer(x, pw, st, pk, pv, k_stack, v_stack, layer,
                                          seq_tile=t_s, ssd_chunk=t_s, sb_block=t_s)
        return (y, k_stack, v_stack), tuple(new[k] for k in _STATE_KEYS)

    st_s = dict(lru_h=state_lru_h, lru_conv=state_lru_conv, ssd=state_ssd, ssd_conv=state_ssd_conv,
                cf_conv=state_cf_conv)
    (y_s, k_s, v_s), s_new = lax.scan(sample_step, (x_sample,) + kv_stacks(x_sample),
                                      (pws, st_s, cache_sb_k, cache_sb_v, layers))

    outs = [y_p, y_s]
    for i in range(len(_STATE_KEYS)):
        outs += [p_new[i], s_new[i]]
    outs += [kv_out(k_p), kv_out(k_s), kv_out(v_p), kv_out(v_s)]
    return tuple(outs)
```

```python
import functools

import jax
import jax.numpy as jnp
from jax import lax
from jax.experimental import pallas as pl
from jax.experimental.pallas import tpu as pltpu

F32 = jnp.float32
BF16 = jnp.bfloat16

EPS = 1e-6
LRU_C = 8.0
N_BRANCH = 4
LRU_BLOCKS = 8
SSD_P = 64
SSD_N = 64
SSD_G = 4
SSD_R = 2
SB_H = 8
SB_DH = 64
SB_KEY_BLOCK = 128
SB_QBLOCKS_PER_STEP = 2

LANES = 128
SUBLANES = 8
VMEM_LIMIT_BYTES = 56 * 1024 * 1024

F32_EXP_ZERO_BELOW = -104.0
NEG_BIG = -1e30

COL_XBC = 0
COL_CF_IN = 1024
COL_LRU_X = 2048
COL_LRU_G = 2560
COL_SSD_Z = 3072
COL_CF_G = 3584
COL_Q = 4096
COL_SB_G = 4608
COL_MERGE = 5120
COL_DT = 9216
P_COLS = 9472
P_TILE_N = P_COLS // 2
INPROJ_CHUNK = 512


def _cparams(*sem):
    return pltpu.CompilerParams(dimension_semantics=sem, vmem_limit_bytes=VMEM_LIMIT_BYTES)


def _sigmoid(x):
    return 0.5 * jnp.tanh(0.5 * x) + 0.5


def _silu(x):
    return x * _sigmoid(x)


def _softplus(x):
    return jnp.maximum(x, 0.0) + jnp.log(1.0 + jnp.exp(-jnp.abs(x)))


def _split3(x):
    hi = x.astype(BF16)
    r = x - hi.astype(F32)
    mid = r.astype(BF16)
    lo = (r - mid.astype(F32)).astype(BF16)
    return hi, mid, lo


def _dot(a, b):
    return jnp.dot(a, b, preferred_element_type=F32)


def _dot_nt(a, b):
    return lax.dot_general(a, b, (((1,), (1,)), ((), ())), preferred_element_type=F32)


def _sel_left(m01, x):
    hi, mid, lo = _split3(x)
    return _dot(m01, hi) + _dot(m01, mid) + _dot(m01, lo)


def _sel_right(x, m01):
    hi, mid, lo = _split3(x)
    return _dot(hi, m01) + _dot(mid, m01) + _dot(lo, m01)


def _sel_nt(m01, x):
    hi, mid, lo = _split3(x)
    return _dot_nt(m01, hi) + _dot_nt(m01, mid) + _dot_nt(m01, lo)


def _iota(shape, dim):
    return lax.broadcasted_iota(jnp.int32, shape, dim)


def _div_pow2(x, n):
    shift = n.bit_length() - 1
    assert 1 << shift == n
    return lax.shift_right_logical(x, shift)


def _inproj_kernel(x_ref, g_ref, w_ref, o_ref, h_ref):
    @pl.when(pl.program_id(1) == 0)
    def _():
        x = x_ref[...]
        ms = jnp.mean(x * x, axis=-1, keepdims=True)
        h_ref[...] = (x * lax.rsqrt(ms + EPS) * g_ref[...]).astype(BF16)

    h = h_ref[...]
    tn = o_ref.shape[1]
    for c0 in range(0, tn, INPROJ_CHUNK):
        c1 = min(c0 + INPROJ_CHUNK, tn)
        o_ref[:, c0:c1] = _dot(h, w_ref[:, c0:c1]).astype(o_ref.dtype)


def _inproj(x2, g, w, out_dtype, tn):
    n, d = x2.shape
    cols = w.shape[1]
    tm = min(n, 1024)
    return pl.pallas_call(
        _inproj_kernel,
        grid=(n // tm, cols // tn),
        in_specs=[
            pl.BlockSpec((tm, d), lambda i, j: (i, 0)),
            pl.BlockSpec((1, d), lambda i, j: (0, 0)),
            pl.BlockSpec((d, tn), lambda i, j: (0, j)),
        ],
        out_specs=pl.BlockSpec((tm, tn), lambda i, j: (i, j)),
        out_shape=jax.ShapeDtypeStruct((n, cols), out_dtype),
        scratch_shapes=[pltpu.VMEM((tm, d), BF16)],
        compiler_params=_cparams("parallel", "arbitrary"),
        name="inproj_" + jnp.dtype(out_dtype).name,
    )(x2, g, w)


def _kv_kernel(layer_ref, x_ref, g_ref, wkt_ref, wvt_ref, kin_ref, vin_ref, k_ref, v_ref, kb_ref, vb_ref):
    del layer_ref, kin_ref, vin_ref
    x = x_ref[...]
    ms = jnp.mean(x * x, axis=-1, keepdims=True)
    h = (x * lax.rsqrt(ms + EPS) * g_ref[...]).astype(BF16)
    kt = _dot_nt(wkt_ref[...], h)
    vt = _dot_nt(wvt_ref[...], h)
    k_ref[...] = kt
    v_ref[...] = vt
    kb_ref[...] = kt.astype(BF16)
    vb_ref[...] = vt.astype(BF16)


def _kvproj(x, g, wkt, wvt, k_stack, v_stack, layer):
    b, t, d = x.shape
    w = wkt.shape[0]
    tm = min(t, 1024)
    stack_blk = pl.BlockSpec((None, None, w, tm), lambda bi, ti, l: (l[0], bi, 0, ti))
    full = lambda shape: pl.BlockSpec(shape, lambda bi, ti, l: (0,) * len(shape))
    grid_spec = pltpu.PrefetchScalarGridSpec(
        num_scalar_prefetch=1,
        grid=(b, t // tm),
        in_specs=[
            pl.BlockSpec((None, tm, d), lambda bi, ti, l: (bi, ti, 0)),
            full((1, d)), full((w, d)), full((w, d)),
            pl.BlockSpec(memory_space=pl.ANY),
            pl.BlockSpec(memory_space=pl.ANY),
        ],
        out_specs=[stack_blk, stack_blk] + [pl.BlockSpec((None, w, tm), lambda bi, ti, l: (bi, 0, ti))] * 2,
    )
    return pl.pallas_call(
        _kv_kernel,
        grid_spec=grid_spec,
        out_shape=[jax.ShapeDtypeStruct(k_stack.shape, F32), jax.ShapeDtypeStruct(v_stack.shape, F32)]
        + [jax.ShapeDtypeStruct((b, w, t), BF16)] * 2,
        input_output_aliases={5: 0, 6: 1},
        compiler_params=_cparams("arbitrary", "arbitrary"),
        name="kvproj",
    )(jnp.reshape(layer, (1,)).astype(jnp.int32), x, g, wkt, wvt, k_stack, v_stack)


def _conv_taps(xe_ref, w_ref, b_ref, base, rows, hist_off, k):
    acc = None
    for j in range(k):
        term = w_ref[j:j + 1, :] * xe_ref[pl.ds(base + hist_off + j, rows), :]
        acc = term if acc is None else acc + term
    return acc + b_ref[...]


def _conv_short(x_cur, hist_ref, w_ref, b_ref):
    k = w_ref.shape[0]
    first_row = _iota((SUBLANES, x_cur.shape[1]), 0) == 0
    delayed = x_cur
    acc = w_ref[k - 1:k, :] * x_cur
    for j in range(k - 2, -1, -1):
        rolled = pltpu.roll(delayed, 1, axis=0)
        head = jnp.where(first_row, hist_ref[j:j + 1, :], rolled[:SUBLANES])
        delayed = jnp.concatenate([head, rolled[SUBLANES:]], axis=0)
        acc = acc + w_ref[j:j + 1, :] * delayed
    return acc + b_ref[...]


LRU_HIST = 8


def _lru_kernel(x_ref, g_ref, cs_ref, h0_ref, cw_ref, cb_ref, wa_ref, ba_ref, wx_ref, bx_ref, lam_ref,
                u_ref, hl_ref, cn_ref, xe_ref, a_ref, b_ref, h_ref):
    tt, w = x_ref.shape
    kc = cw_ref.shape[0]

    @pl.when(pl.program_id(1) == 0)
    def _():
        xe_ref[...] = cs_ref[...]
        h_ref[...] = h0_ref[...]

    x_cur = x_ref[...].astype(F32)
    xc = _conv_short(x_cur, xe_ref, cw_ref, cb_ref)
    tail = x_cur[tt - (kc - 1):, :]
    xe_ref[...] = tail
    cn_ref[...] = tail

    xcb = xc.astype(BF16)
    gate_a = _dot(xcb, wa_ref[...]) + ba_ref[...]
    gate_x = _dot(xcb, wx_ref[...]) + bx_ref[...]
    r = _sigmoid(gate_a)
    i = _sigmoid(gate_x)
    log_a = (-LRU_C) * r * _softplus(-lam_ref[...])
    a = jnp.exp(log_a)
    one_m_a2 = (1.0 - a) * (1.0 + a)
    a_ref[...] = a
    root = jnp.where(one_m_a2 > 0.0, one_m_a2 * lax.rsqrt(one_m_a2), 0.0)
    b_ref[...] = root * (i * xc)

    row = _iota((SUBLANES, w), 0)

    def body(s, h):
        off = pl.multiple_of(s * SUBLANES, SUBLANES)
        a8 = a_ref[pl.ds(off, SUBLANES), :]
        b8 = b_ref[pl.ds(off, SUBLANES), :]
        for d in (1, 2, 4):
            keep = row >= d
            a_sh = pltpu.roll(a8, d, axis=0)
            b_sh = pltpu.roll(b8, d, axis=0)
            b8 = jnp.where(keep, a8 * b_sh + b8, b8)
            a8 = jnp.where(keep, a8 * a_sh, a8)
        h8 = a8 * h + b8
        b_ref[pl.ds(off, SUBLANES), :] = h8
        return h8[SUBLANES - 1:SUBLANES, :]

    h_last = lax.fori_loop(0, tt // SUBLANES, body, h_ref[...])
    h_ref[...] = h_last
    hl_ref[...] = h_last
    u_ref[...] = (b_ref[...] * _silu(g_ref[...].astype(F32))).astype(u_ref.dtype)


def _lru_branch(p3, conv_state, h0, pw, tt):
    b, t, _ = p3.shape
    w = conv_state.shape[-1]
    kc = pw["lru_conv_w"].shape[0]
    col = lambda off: off // w
    full = lambda shape: pl.BlockSpec(shape, lambda bi, ti: (0,) * len(shape))
    return pl.pallas_call(
        _lru_kernel,
        grid=(b, t // tt),
        in_specs=[
            pl.BlockSpec((None, tt, w), lambda bi, ti: (bi, ti, col(COL_LRU_X))),
            pl.BlockSpec((None, tt, w), lambda bi, ti: (bi, ti, col(COL_LRU_G))),
            pl.BlockSpec((None, kc - 1, w), lambda bi, ti: (bi, 0, 0)),
            pl.BlockSpec((None, 1, w), lambda bi, ti: (bi, 0, 0)),
            full((kc, w)), full((1, w)), full((w, w)), full((1, w)), full((w, w)), full((1, w)), full((1, w)),
        ],
        out_specs=[
            pl.BlockSpec((None, tt, w), lambda bi, ti: (bi, ti, 0)),
            pl.BlockSpec((None, 1, w), lambda bi, ti: (bi, 0, 0)),
            pl.BlockSpec((None, kc - 1, w), lambda bi, ti: (bi, 0, 0)),
        ],
        out_shape=[
            jax.ShapeDtypeStruct((b, t, w), BF16),
            jax.ShapeDtypeStruct((b, 1, w), F32),
            jax.ShapeDtypeStruct((b, kc - 1, w), F32),
        ],
        scratch_shapes=[
            pltpu.VMEM((kc - 1, w), F32),
            pltpu.VMEM((tt, w), F32),
            pltpu.VMEM((tt, w), F32),
            pltpu.VMEM((1, w), F32),
        ],
        compiler_params=_cparams("parallel", "arbitrary"),
        name="lru_branch",
    )(p3, p3, conv_state, h0, pw["lru_conv_w"], pw["lru_conv_b"], pw["lru_wa"], pw["lru_ba"],
      pw["lru_wx"], pw["lru_bx"], pw["lru_lambda"])


def _ssd_kernel(z_ref, xbc_ref, dt_ref, cs_ref, s0_ref, cw_ref, cb_ref, dtb_ref, alog_ref, dvec_ref, nw_ref,
                u_ref, sn_ref, cn_ref, xe_ref, st_ref, *, L):
    tile, inner = z_ref.shape
    kc = cw_ref.shape[0]

    @pl.when(pl.program_id(1) == 0)
    def _():
        xe_ref[...] = cs_ref[...]
        st_ref[...] = s0_ref[...]

    x_cur = xbc_ref[...].astype(F32)
    xa_all = _silu(_conv_short(x_cur, xe_ref, cw_ref, cb_ref))
    tail = x_cur[tile - (kc - 1):, :]
    xe_ref[...] = tail
    cn_ref[...] = tail

    dt_all = _softplus(dt_ref[...].astype(F32) + dtb_ref[...])
    adt_all = dt_all * (-jnp.exp(alog_ref[...]))

    for c in range(tile // L):
        rows = slice(c * L, (c + 1) * L)
        y = _ssd_chunk(xa_all[rows], dt_all[rows], adt_all[rows], st_ref, dvec_ref, inner, L)
        gated = y * _silu(z_ref[rows, :].astype(F32))
        ms2 = jnp.mean(gated * gated, axis=-1, keepdims=True)
        u_ref[rows, :] = (gated * lax.rsqrt(ms2 + EPS) * nw_ref[...]).astype(u_ref.dtype)
    sn_ref[...] = st_ref[...]


def _ssd_chunk(xa, dt, adt, st_ref, dvec_ref, inner, L):
    gw = SSD_R * SSD_P
    x = xa[:, :inner]
    bm = xa[:, inner:inner + SSD_G * SSD_N]
    cm = xa[:, inner + SSD_G * SSD_N:]

    causal = _iota((L, L), 0) >= _iota((L, L), 1)
    tril = causal.astype(BF16)
    eye = (_iota((LANES, LANES), 0) == _iota((LANES, LANES), 1)).astype(BF16)
    eye_n = eye[:SSD_N, :SSD_N]
    expand_c = (_div_pow2(_iota((LANES, inner), 1), SSD_P) == _iota((LANES, inner), 0)).astype(BF16)
    n_heads = inner // SSD_P
    expand_l = (_div_pow2(_iota((LANES, n_heads * L), 1), L) == _iota((LANES, n_heads * L), 0)).astype(BF16)

    acs = _sel_left(tril, adt)
    acs_t = _sel_nt(eye, acs)
    dt_c = _sel_right(dt, expand_c)
    acs_c = _sel_right(acs, expand_c)
    acs_w = _sel_right(acs, expand_l)

    xdt = x * dt_c
    last = acs_c[L - 1:L, :]
    xs = xdt * jnp.exp(last - acs_c)
    e_acs = jnp.exp(acs_c)
    chunk_decay = jnp.exp(last)
    lane = _iota((L, gw), 1)

    ys = []
    for g in range(SSD_G):
        cg = cm[:, g * SSD_N:(g + 1) * SSD_N].astype(BF16)
        bg = bm[:, g * SSD_N:(g + 1) * SSD_N].astype(BF16)
        cb = _dot_nt(cg, bg)
        ms = []
        for r in range(SSD_R):
            h = g * SSD_R + r
            seg = acs_w[:, h * L:(h + 1) * L] - acs_t[h:h + 1, :]
            dec = jnp.exp(jnp.where(causal, seg, NEG_BIG))
            ms.append((cb * dec).astype(BF16))
        mcat = jnp.concatenate(ms, axis=1)
        xg = xdt[:, g * gw:(g + 1) * gw]
        rhs = jnp.concatenate([jnp.where(lane < SSD_P, xg, 0.0), jnp.where(lane >= SSD_P, xg, 0.0)],
                              axis=0).astype(BF16)
        y_diag = _dot(mcat, rhs)
        st = st_ref[g]
        y_off = _dot(cg, st.astype(BF16)) * e_acs[:, g * gw:(g + 1) * gw]
        bg_t = _dot_nt(eye_n, bg).astype(BF16)
        st_ref[g] = st * chunk_decay[:, g * gw:(g + 1) * gw] + _dot(bg_t, xs[:, g * gw:(g + 1) * gw].astype(BF16))
        ys.append(y_diag + y_off)
    return jnp.concatenate(ys, axis=1) + dvec_ref[...] * x


SSD_CHUNKS_PER_STEP = 2


def _ssd_branch(p3, conv_state, st0, pw, L):
    b, t, _ = p3.shape
    tile = min(t, SSD_CHUNKS_PER_STEP * L)
    cdim = conv_state.shape[-1]
    inner = pw["ssd_norm"].shape[-1]
    kc = pw["ssd_conv_w"].shape[0]
    gw = SSD_R * SSD_P
    full = lambda shape: pl.BlockSpec(shape, lambda bi, ti: (0,) * len(shape))
    return pl.pallas_call(
        functools.partial(_ssd_kernel, L=L),
        grid=(b, t // tile),
        in_specs=[
            pl.BlockSpec((None, tile, inner), lambda bi, ti: (bi, ti, COL_SSD_Z // inner)),
            pl.BlockSpec((None, tile, cdim), lambda bi, ti: (bi, ti, COL_XBC // cdim)),
            pl.BlockSpec((None, tile, LANES), lambda bi, ti: (bi, ti, COL_DT // LANES)),
            pl.BlockSpec((None, kc - 1, cdim), lambda bi, ti: (bi, 0, 0)),
            pl.BlockSpec((None, SSD_G, SSD_N, gw), lambda bi, ti: (bi, 0, 0, 0)),
            full((kc, cdim)), full((1, cdim)), full((1, LANES)), full((1, LANES)), full((1, inner)), full((1, inner)),
        ],
        out_specs=[
            pl.BlockSpec((None, tile, inner), lambda bi, ti: (bi, ti, 0)),
            pl.BlockSpec((None, SSD_G, SSD_N, gw), lambda bi, ti: (bi, 0, 0, 0)),
            pl.BlockSpec((None, kc - 1, cdim), lambda bi, ti: (bi, 0, 0)),
        ],
        out_shape=[
            jax.ShapeDtypeStruct((b, t, inner), BF16),
            jax.ShapeDtypeStruct((b, SSD_G, SSD_N, gw), F32),
            jax.ShapeDtypeStruct((b, kc - 1, cdim), F32),
        ],
        scratch_shapes=[
            pltpu.VMEM((kc - 1, cdim), F32),
            pltpu.VMEM((SSD_G, SSD_N, gw), F32),
        ],
        compiler_params=_cparams("parallel", "arbitrary"),
        name="ssd_branch",
    )(p3, p3, p3, conv_state, st0, pw["ssd_conv_w"], pw["ssd_conv_b"], pw["ssd_dt_bias"], pw["ssd_a_log"],
      pw["ssd_d"], pw["ssd_norm"])


CF_HIST = 32
CF_ROWS = 128


def _cf_kernel(in_ref, g_ref, cs_ref, cw_ref, cb_ref, lg_ref, lb_ref, u_ref, cn_ref, xe_ref):
    tt = in_ref.shape[0]
    w = g_ref.shape[1]
    kc = cw_ref.shape[0]
    hist = CF_HIST - (kc - 1)

    @pl.when(pl.program_id(1) == 0)
    def _():
        xe_ref[hist:CF_HIST, :] = cs_ref[...]

    cf = in_ref[...].astype(F32)
    xe_ref[CF_HIST:CF_HIST + tt, :] = cf[:, :w] * _sigmoid(cf[:, w:])

    n_rows = min(CF_ROWS, tt)

    def body(s, carry):
        base = pl.multiple_of(s * n_rows, n_rows)
        c = None
        for r in range(SUBLANES):
            rows = n_rows + (SUBLANES if r else 0)
            part = None
            for o in range(hist + (r - hist) % SUBLANES, hist + kc, SUBLANES):
                term = cw_ref[o - hist:o - hist + 1, :] * xe_ref[pl.ds(base + (o - r), rows), :]
                part = term if part is None else part + term
            piece = part[r:r + n_rows, :]
            c = piece if c is None else c + piece
        c = c + cb_ref[...]
        mu = jnp.mean(c, axis=-1, keepdims=True)
        cc = c - mu
        var = jnp.mean(cc * cc, axis=-1, keepdims=True)
        ln = cc * lax.rsqrt(var + EPS) * lg_ref[...] + lb_ref[...]
        gate = g_ref[pl.ds(base, n_rows), :].astype(F32)
        u_ref[pl.ds(base, n_rows), :] = (_silu(ln) * _silu(gate)).astype(u_ref.dtype)
        return carry

    lax.fori_loop(0, tt // n_rows, body, 0)
    tail = xe_ref[tt + hist:tt + CF_HIST, :]
    xe_ref[hist:CF_HIST, :] = tail
    cn_ref[...] = tail


def _cf_branch(p3, conv_state, pw, tt):
    b, t, _ = p3.shape
    w = conv_state.shape[-1]
    kc = pw["cf_conv_w"].shape[0]
    full = lambda shape: pl.BlockSpec(shape, lambda bi, ti: (0,) * len(shape))
    return pl.pallas_call(
        _cf_kernel,
        grid=(b, t // tt),
        in_specs=[
            pl.BlockSpec((None, tt, 2 * w), lambda bi, ti: (bi, ti, COL_CF_IN // (2 * w))),
            pl.BlockSpec((None, tt, w), lambda bi, ti: (bi, ti, COL_CF_G // w)),
            pl.BlockSpec((None, kc - 1, w), lambda bi, ti: (bi, 0, 0)),
            full((kc, w)), full((1, w)), full((1, w)), full((1, w)),
        ],
        out_specs=[
            pl.BlockSpec((None, tt, w), lambda bi, ti: (bi, ti, 0)),
            pl.BlockSpec((None, kc - 1, w), lambda bi, ti: (bi, 0, 0)),
        ],
        out_shape=[
            jax.ShapeDtypeStruct((b, t, w), BF16),
            jax.ShapeDtypeStruct((b, kc - 1, w), F32),
        ],
        scratch_shapes=[pltpu.VMEM((tt + CF_HIST, w), F32)],
        compiler_params=_cparams("parallel", "arbitrary"),
        name="cf_branch",
    )(p3, p3, conv_state, pw["cf_conv_w"], pw["cf_conv_b"], pw["cf_ln_g"], pw["cf_ln_b"])


def _sb_kernel(q_ref, kt_ref, vt_ref, g_ref, sums_ref, u_ref, acc_ref, out_ref, *, q_pos0, tk, tq):
    rows_total, w = q_ref.shape
    n_q = rows_total // tq
    n_pairs = w // LANES
    per_pair = LANES // SB_DH
    q_starts = [q_pos0 + (pl.program_id(1) * n_q + i) * tq for i in range(n_q)]
    kds = [(qs + (tq - 1)) // tk for qs in q_starts]
    earliers = [(kd * tk + _iota((tq, tk), 1)) < (qs + _iota((tq, tk), 0)) for kd, qs in zip(kds, q_starts)]
    sums_rhs = sums_ref[...]

    head_of_lane = _div_pow2(_iota((1, LANES), 1), SB_DH)
    lane_masks = [(head_of_lane == r).astype(BF16) for r in range(per_pair)]
    q_masks = [m * (SB_DH ** -0.5) for m in lane_masks]
    head_of_channel = _div_pow2(_iota((LANES, tk), 0), SB_DH)
    channel_masks = [(head_of_channel == r).astype(BF16) for r in range(per_pair)]
    qm = []
    for i in range(n_q):
        per_q = []
        for p in range(n_pairs):
            qp = q_ref[i * tq:(i + 1) * tq, p * LANES:(p + 1) * LANES]
            per_q.append([qp * m for m in q_masks])
        qm.append(per_q)
    half = tq // 2

    acc_ref[...] = jnp.zeros_like(acc_ref)
    out_ref[...] = jnp.zeros_like(out_ref)

    def block(step, masked, nrows):
        stack = lambda parts: jnp.concatenate([x[:nrows] for x in parts], axis=0)
        offs, valids, zs, lsms, hls = [], [], [], [], []
        for i in range(n_q):
            kb = kds[i] - step
            valids.append(kb >= 0)
            offs.append(pl.multiple_of(jnp.maximum(kb, 0) * tk, tk))
            earlier = stack([earliers[i]] * per_pair)
            for p in range(n_pairs):
                kpt = kt_ref[p * LANES:(p + 1) * LANES, pl.ds(offs[i], tk)]
                z = _dot(stack(qm[i][p]), kpt)
                lsm = -_softplus(z)
                if masked:
                    lsm = jnp.where(earlier, lsm, 0.0)
                hi = lsm.astype(BF16)
                lo = (lsm - hi.astype(F32)).astype(BF16)
                zs.append(z)
                lsms.append(lsm)
                hls.append(jnp.concatenate([hi, lo], axis=1))
        sums = _dot(jnp.concatenate(hls, axis=0), sums_rhs)
        tops, rests = [], []
        for i in range(n_q):
            earlier = stack([earliers[i]] * per_pair)
            keep = jnp.where(valids[i], 1.0, 0.0)
            worst_top, worst_rest = None, None
            for p in range(n_pairs):
                n = i * n_pairs + p
                rows = slice(n * per_pair * nrows, (n + 1) * per_pair * nrows)
                acc = stack([acc_ref[i, p, r] for r in range(per_pair)])
                lw = zs[n] + lsms[n] + sums[rows, :tk] + acc
                if masked:
                    lw = jnp.where(earlier, lw, NEG_BIG)
                wgt = jnp.exp(lw)
                if not masked:
                    wgt = wgt * keep
                wgt = wgt.astype(BF16)
                acc = acc + sums[rows, tk:]
                for r in range(per_pair):
                    acc_r = acc[r * nrows:(r + 1) * nrows]
                    acc_ref[i, p, r, :nrows] = acc_r
                    top = acc_r[:half]
                    worst_top = top if worst_top is None else jnp.maximum(worst_top, top)
                    if nrows > half:
                        rest = acc_r[half:]
                        worst_rest = rest if worst_rest is None else jnp.maximum(worst_rest, rest)
                vpt = vt_ref[p * LANES:(p + 1) * LANES, pl.ds(offs[i], tk)]
                v2t = jnp.concatenate([vpt * m for m in channel_masks], axis=1)
                wcat = jnp.concatenate([wgt[r * nrows:(r + 1) * nrows] for r in range(per_pair)], axis=1)
                out_ref[i, p, :nrows] = out_ref[i, p, :nrows] + _dot_nt(wcat, v2t)
            tops.append(jnp.where(valids[i], jnp.max(worst_top), -jnp.inf))
            if nrows > half:
                rests.append(jnp.where(valids[i], jnp.max(worst_rest), -jnp.inf))
        return tuple(tops), tuple(rests)

    def any_live(step, worsts):
        live = [jnp.logical_and(kds[i] - step >= 0, worsts[i] >= F32_EXP_ZERO_BELOW) for i in range(n_q)]
        return functools.reduce(jnp.logical_or, live)

    tops0, rests0 = block(0, True, tq)

    def full_body(c):
        tops, rests = block(c[0], False, tq)
        return c[0] + 1, tops, rests

    step1, tops1, _ = lax.while_loop(lambda c: any_live(c[0], c[2]), full_body, (1, tops0, rests0))

    def half_body(c):
        tops, _ = block(c[0], False, half)
        return c[0] + 1, tops

    lax.while_loop(lambda c: any_live(c[0], c[1]), half_body, (step1, tops1))

    for i in range(n_q):
        o = jnp.concatenate([out_ref[i, p] for p in range(n_pairs)], axis=1)
        gate = g_ref[i * tq:(i + 1) * tq, :].astype(F32)
        u_ref[i * tq:(i + 1) * tq, :] = (o * _silu(gate)).astype(u_ref.dtype)


def _sb_branch(p3, kt_all, vt_all, q_pos0, tq):
    b, t, _ = p3.shape
    w, keys = kt_all.shape[1:]
    tk = SB_KEY_BLOCK
    assert tk % tq == 0 and q_pos0 % tq == 0 and keys % tk == 0 and vt_all.shape == kt_all.shape
    later = jnp.tril(jnp.ones((tk, tk), BF16), k=-1)
    top = jnp.concatenate([later, jnp.ones((tk, tk), BF16)], axis=1)
    sums_rhs = jnp.concatenate([top, top], axis=0)
    n_q = min(SB_QBLOCKS_PER_STEP, t // tq)
    rows = n_q * tq
    return pl.pallas_call(
        functools.partial(_sb_kernel, q_pos0=q_pos0, tk=tk, tq=tq),
        grid=(b, t // rows),
        in_specs=[
            pl.BlockSpec((None, rows, w), lambda bi, qi: (bi, qi, COL_Q // w)),
            pl.BlockSpec((None, w, keys), lambda bi, qi: (bi, 0, 0)),
            pl.BlockSpec((None, w, keys), lambda bi, qi: (bi, 0, 0)),
            pl.BlockSpec((None, rows, w), lambda bi, qi: (bi, qi, COL_SB_G // w)),
            pl.BlockSpec((2 * tk, 2 * tk), lambda bi, qi: (0, 0)),
        ],
        out_specs=pl.BlockSpec((None, rows, w), lambda bi, qi: (bi, qi, 0)),
        out_shape=jax.ShapeDtypeStruct((b, t, w), BF16),
        scratch_shapes=[
            pltpu.VMEM((n_q, w // LANES, LANES // SB_DH, tq, tk), F32),
            pltpu.VMEM((n_q, w // LANES, tq, LANES), F32),
        ],
        compiler_params=_cparams("parallel", "arbitrary"),
        name="sb_branch",
    )(p3, kt_all, vt_all, p3, sums_rhs)


def _mix_kernel(ua_ref, ub_ref, uc_ref, ud_ref, m0_ref, m1_ref, m2_ref, m3_ref, x_ref, wd_ref, wo_ref, g_ref, o_ref):
    us = (ua_ref, ub_ref, uc_ref, ud_ref)
    ms = (m0_ref, m1_ref, m2_ref, m3_ref)
    mixed = None
    for i in range(N_BRANCH):
        y = _dot(us[i][...], wd_ref[i])
        term = _sigmoid(ms[i][...].astype(F32)) * y
        mixed = term if mixed is None else mixed + term
    o = _dot(mixed.astype(BF16), wo_ref[...])
    ms2 = jnp.mean(o * o, axis=-1, keepdims=True)
    o_ref[...] = x_ref[...] + o * lax.rsqrt(ms2 + EPS) * g_ref[...]


def _mix(us, p2, x2, pw):
    n, d = x2.shape
    w = us[0].shape[-1]
    tm = min(n, 512)
    ublk = pl.BlockSpec((tm, w), lambda i: (i, 0))
    mblk = lambda j: pl.BlockSpec((tm, d), lambda i: (i, COL_MERGE // d + j))
    return pl.pallas_call(
        _mix_kernel,
        grid=(n // tm,),
        in_specs=[ublk] * 4 + [mblk(j) for j in range(N_BRANCH)] + [
            pl.BlockSpec((tm, d), lambda i: (i, 0)),
            pl.BlockSpec((N_BRANCH, w, d), lambda i: (0, 0, 0)),
            pl.BlockSpec((d, d), lambda i: (0, 0)),
            pl.BlockSpec((1, d), lambda i: (0, 0)),
        ],
        out_specs=pl.BlockSpec((tm, d), lambda i: (i, 0)),
        out_shape=jax.ShapeDtypeStruct((n, d), F32),
        input_output_aliases={len(us) + N_BRANCH: 0},
        compiler_params=_cparams("parallel"),
        name="mix",
    )(*us, p2, p2, p2, p2, x2, pw["w_down"], pw["w_out"], pw["norm_post"])


def _block_diag(wb):
    n, c, d = wb.shape
    eye = jnp.eye(n, dtype=wb.dtype)
    return (eye[:, None, :, None] * wb[:, :, None, :]).reshape(n * c, n * d)


def _prep_layer(p):
    d_model = p["w_in"].shape[0]
    bw = p["lru_lambda"].shape[-1]
    cdim = p["ssd_conv_b"].shape[-1]
    n_heads = p["ssd_dt_bias"].shape[-1]
    sizes = (bw, bw, bw, cdim, n_heads, 2 * bw, bw, 3 * bw, bw, N_BRANCH * d_model)
    offs = [0]
    for s in sizes:
        offs.append(offs[-1] + s)
    seg = lambda i: p["w_in"][:, offs[i]:offs[i + 1]]
    lru_x, lru_g, ssd_z, xbc, dt, cf_in, cf_g, qkv, sb_g, merge = [seg(i) for i in range(len(sizes))]
    q, k, v = qkv[:, :bw], qkv[:, bw:2 * bw], qkv[:, 2 * bw:]
    zeros = lambda c: jnp.zeros((d_model, c), p["w_in"].dtype)
    main = jnp.concatenate(
        [xbc, cf_in, lru_x, lru_g, ssd_z, cf_g, q, sb_g, merge, dt, zeros(LANES - n_heads),
         zeros(P_COLS - COL_DT - LANES)], axis=1)
    assert main.shape[1] == P_COLS
    pad_h = lambda a, fill: jnp.concatenate([a, jnp.full((LANES - n_heads,), fill, a.dtype)])[None, :]
    row = lambda a: a[None, :].astype(F32)
    return dict(
        w_main=main.astype(BF16), w_kt=k.T.astype(BF16), w_vt=v.T.astype(BF16),
        norm_pre=row(p["norm_pre"]), norm_post=row(p["norm_post"]),
        lru_conv_w=p["lru_conv_w"], lru_conv_b=row(p["lru_conv_b"]),
        lru_wa=_block_diag(p["lru_wa"]).astype(BF16), lru_ba=row(p["lru_ba"]),
        lru_wx=_block_diag(p["lru_wx"]).astype(BF16), lru_bx=row(p["lru_bx"]),
        lru_lambda=row(p["lru_lambda"]),
        ssd_conv_w=p["ssd_conv_w"], ssd_conv_b=row(p["ssd_conv_b"]),
        ssd_dt_bias=pad_h(p["ssd_dt_bias"], 0.0), ssd_a_log=pad_h(p["ssd_a_log"], 0.0),
        ssd_d=jnp.repeat(p["ssd_d"], SSD_P)[None, :], ssd_norm=row(p["ssd_norm"]),
        cf_conv_w=p["cf_conv_w"], cf_conv_b=row(p["cf_conv_b"]),
        cf_ln_g=row(p["cf_ln_g"]), cf_ln_b=row(p["cf_ln_b"]),
        w_down=p["w_down"].astype(BF16), w_out=p["w_out"].astype(BF16),
    )


def _ssd_state_in(s):
    b = s.shape[0]
    s = s.reshape(b, SSD_G, SSD_R, SSD_P, SSD_N)
    return jnp.transpose(s, (0, 1, 4, 2, 3)).reshape(b, SSD_G, SSD_N, SSD_R * SSD_P)


def _ssd_state_out(s):
    b = s.shape[0]
    s = s.reshape(b, SSD_G, SSD_N, SSD_R, SSD_P)
    return jnp.transpose(s, (0, 1, 3, 4, 2)).reshape(b, SSD_G * SSD_R, SSD_P, SSD_N)


def _layer(x, pw, st, past_k, past_v, k_stack, v_stack, layer, *, seq_tile, ssd_chunk, sb_block):
    b, t, d = x.shape
    x2 = x.reshape(b * t, d)
    p2 = _inproj(x2, pw["norm_pre"], pw["w_main"], BF16, P_TILE_N)
    k_stack, v_stack, kt_new, vt_new = _kvproj(x, pw["norm_pre"], pw["w_kt"], pw["w_vt"], k_stack, v_stack, layer)
    p3 = p2.reshape(b, t, P_COLS)

    u_a, lru_h, lru_conv = _lru_branch(p3, st["lru_conv"], st["lru_h"][:, None, :], pw, seq_tile)
    u_b, ssd_new, ssd_conv = _ssd_branch(p3, st["ssd_conv"], _ssd_state_in(st["ssd"]), pw, ssd_chunk)
    u_c, cf_conv = _cf_branch(p3, st["cf_conv"], pw, seq_tile)
    if past_k is None:
        kt_all, vt_all, q_pos0 = kt_new, vt_new, 0
    else:
        q_pos0 = past_k.shape[1]
        pad = -(q_pos0 + t) % SB_KEY_BLOCK

        def with_cache(past, new):
            past_t = jnp.transpose(past, (0, 2, 3, 1)).reshape(b, -1, q_pos0).astype(BF16)
            return jnp.pad(jnp.concatenate([past_t, new], axis=2), ((0, 0), (0, 0), (0, pad)))

        kt_all, vt_all = with_cache(past_k, kt_new), with_cache(past_v, vt_new)
    u_d = _sb_branch(p3, kt_all, vt_all, q_pos0, sb_block)

    w = u_a.shape[-1]
    us = [u.reshape(b * t, w) for u in (u_a, u_b, u_c, u_d)]
    y = _mix(us, p2, x2, pw).reshape(b, t, d)
    new = dict(lru_h=lru_h[:, 0, :], lru_conv=lru_conv, ssd=_ssd_state_out(ssd_new), ssd_conv=ssd_conv,
               cf_conv=cf_conv)
    return y, new, k_stack, v_stack


_STATE_KEYS = ("lru_h", "lru_conv", "ssd", "ssd_conv", "cf_conv")


def kernel(x_prompt, x_sample, state_lru_h, state_lru_conv, state_ssd, state_ssd_conv, state_cf_conv, cache_sb_k, cache_sb_v, norm_pre, norm_post, w_in, lru_conv_w, lru_conv_b, lru_wa, lru_ba, lru_wx, lru_bx, lru_lambda, ssd_conv_w, ssd_conv_b, ssd_dt_bias, ssd_a_log, ssd_d, ssd_norm, cf_conv_w, cf_conv_b, cf_ln_g, cf_ln_b, w_down, w_out):
    raw = dict(norm_pre=norm_pre, norm_post=norm_post, w_in=w_in, lru_conv_w=lru_conv_w, lru_conv_b=lru_conv_b,
               lru_wa=lru_wa, lru_ba=lru_ba, lru_wx=lru_wx, lru_bx=lru_bx, lru_lambda=lru_lambda,
               ssd_conv_w=ssd_conv_w, ssd_conv_b=ssd_conv_b, ssd_dt_bias=ssd_dt_bias, ssd_a_log=ssd_a_log,
               ssd_d=ssd_d, ssd_norm=ssd_norm, cf_conv_w=cf_conv_w, cf_conv_b=cf_conv_b, cf_ln_g=cf_ln_g,
               cf_ln_b=cf_ln_b, w_down=w_down, w_out=w_out)
    pws = jax.vmap(_prep_layer)(raw)

    bp = x_prompt.shape[0]
    dt = x_prompt.dtype
    bw = state_lru_h.shape[-1]
    zero_state = dict(lru_h=jnp.zeros((bp, bw), dt),
                      lru_conv=jnp.zeros((bp,) + state_lru_conv.shape[2:], dt),
                      ssd=jnp.zeros((bp,) + state_ssd.shape[2:], dt),
                      ssd_conv=jnp.zeros((bp,) + state_ssd_conv.shape[2:], dt),
                      cf_conv=jnp.zeros((bp,) + state_cf_conv.shape[2:], dt))

    depth = w_in.shape[0]
    layers = jnp.arange(depth, dtype=jnp.int32)

    def kv_stacks(x):
        shape = (depth, x.shape[0], SB_H * SB_DH, x.shape[1])
        return jnp.zeros(shape, F32), jnp.zeros(shape, F32)

    def kv_out(stack):
        d_, b_, _, t_ = stack.shape
        return jnp.transpose(stack.reshape(d_, b_, SB_H, SB_DH, t_), (0, 1, 4, 2, 3))

    def prompt_step(carry, xs):
        x, k_stack, v_stack = carry
        pw, layer = xs
        y, new, k_stack, v_stack = _layer(x, pw, zero_state, None, None, k_stack, v_stack, layer,
                                          seq_tile=512, ssd_chunk=128, sb_block=128)
        return (y, k_stack, v_stack), tuple(new[k] for k in _STATE_KEYS)

    (y_p, k_p, v_p), p_new = lax.scan(prompt_step, (x_prompt,) + kv_stacks(x_prompt), (pws, layers))

    t_s = x_sample.shape[1]

    def sample_step(carry, xs):
        x, k_stack, v_stack = carry
        pw, st, pk, pv, layer = xs
        y, new, k_stack, v_stack = _layer(x, pw, st, pk, pv, k_stack, v_stack, layer,
                                          seq_tile=t_s, ssd_chunk=t_s, sb_block=t_s)
        return (y, k_stack, v_stack), tuple(new[k] for k in _STATE_KEYS)

    st_s = dict(lru_h=state_lru_h, lru_conv=state_lru_conv, ssd=state_ssd, ssd_conv=state_ssd_conv,
                cf_conv=state_cf_conv)
    (y_s, k_s, v_s), s_new = lax.scan(sample_step, (x_sample,) + kv_stacks(x_sample),
                                      (pws, st_s, cache_sb_k, cache_sb_v, layers))

    outs = [y_p, y_s]
    for i in range(len(_STATE_KEYS)):
        outs += [p_new[i], s_new[i]]
    outs += [kv_out(k_p), kv_out(k_s), kv_out(v_p), kv_out(v_s)]
    return tuple(outs)
```

```python
import functools

import jax
import jax.numpy as jnp
from jax import lax
from jax.experimental import pallas as pl
from jax.experimental.pallas import tpu as pltpu

F32 = jnp.float32
BF16 = jnp.bfloat16

EPS = 1e-6
LRU_C = 8.0
N_BRANCH = 4
SSD_P = 64
SSD_N = 64
SSD_G = 4
SSD_R = 2
SB_H = 8
SB_DH = 64
SB_KEY_BLOCK = 128
SB_QBLOCKS_PER_STEP = 4

LANES = 128
SUBLANES = 8
VMEM_LIMIT_BYTES = 56 * 1024 * 1024

F32_EXP_ZERO_BELOW = -104.0
NEG_BIG = -1e30

COL_XBC = 0
COL_CF_IN = 1024
COL_LRU_X = 2048
COL_LRU_G = 2560
COL_SSD_Z = 3072
COL_CF_G = 3584
COL_Q = 4096
COL_SB_G = 4608
COL_MERGE = 5120
COL_DT = 9216
P_COLS = 9472
P_TILE_N = P_COLS // 2
INPROJ_CHUNK = 512


def _cparams(*sem):
    return pltpu.CompilerParams(dimension_semantics=sem, vmem_limit_bytes=VMEM_LIMIT_BYTES)


def _sigmoid(x):
    return 0.5 * jnp.tanh(0.5 * x) + 0.5


def _silu(x):
    return x * _sigmoid(x)


def _softplus(x):
    return jnp.maximum(x, 0.0) + jnp.log(1.0 + jnp.exp(-jnp.abs(x)))


def _split3(x):
    hi = x.astype(BF16)
    r = x - hi.astype(F32)
    mid = r.astype(BF16)
    lo = (r - mid.astype(F32)).astype(BF16)
    return hi, mid, lo


def _dot(a, b):
    return jnp.dot(a, b, preferred_element_type=F32)


def _dot_nt(a, b):
    return lax.dot_general(a, b, (((1,), (1,)), ((), ())), preferred_element_type=F32)


def _sel_left(m01, x):
    hi, mid, lo = _split3(x)
    return _dot(m01, hi) + _dot(m01, mid) + _dot(m01, lo)


def _sel_right(x, m01):
    hi, mid, lo = _split3(x)
    return _dot(hi, m01) + _dot(mid, m01) + _dot(lo, m01)


def _sel_nt(m01, x):
    hi, mid, lo = _split3(x)
    return _dot_nt(m01, hi) + _dot_nt(m01, mid) + _dot_nt(m01, lo)


def _iota(shape, dim):
    return lax.broadcasted_iota(jnp.int32, shape, dim)


def _div_pow2(x, n):
    shift = n.bit_length() - 1
    assert 1 << shift == n
    return lax.shift_right_logical(x, shift)


def _inproj_kernel(x_ref, g_ref, w_ref, o_ref, h_ref):
    @pl.when(pl.program_id(1) == 0)
    def _():
        x = x_ref[...]
        ms = jnp.mean(x * x, axis=-1, keepdims=True)
        h_ref[...] = (x * lax.rsqrt(ms + EPS) * g_ref[...]).astype(BF16)

    h = h_ref[...]
    tn = o_ref.shape[1]
    for c0 in range(0, tn, INPROJ_CHUNK):
        c1 = min(c0 + INPROJ_CHUNK, tn)
        o_ref[:, c0:c1] = _dot(h, w_ref[:, c0:c1]).astype(o_ref.dtype)


def _inproj(x2, g, w, out_dtype, tn):
    n, d = x2.shape
    cols = w.shape[1]
    tm = min(n, 1024)
    return pl.pallas_call(
        _inproj_kernel,
        grid=(n // tm, cols // tn),
        in_specs=[
            pl.BlockSpec((tm, d), lambda i, j: (i, 0)),
            pl.BlockSpec((1, d), lambda i, j: (0, 0)),
            pl.BlockSpec((d, tn), lambda i, j: (0, j)),
        ],
        out_specs=pl.BlockSpec((tm, tn), lambda i, j: (i, j)),
        out_shape=jax.ShapeDtypeStruct((n, cols), out_dtype),
        scratch_shapes=[pltpu.VMEM((tm, d), BF16)],
        compiler_params=_cparams("parallel", "arbitrary"),
        name="inproj_" + jnp.dtype(out_dtype).name,
    )(x2, g, w)


def _kv_kernel(layer_ref, x_ref, g_ref, wkt_ref, wvt_ref, kin_ref, vin_ref, k_ref, v_ref, kb_ref, vb_ref):
    del layer_ref, kin_ref, vin_ref
    x = x_ref[...]
    ms = jnp.mean(x * x, axis=-1, keepdims=True)
    h = (x * lax.rsqrt(ms + EPS) * g_ref[...]).astype(BF16)
    kt = _dot_nt(wkt_ref[...], h)
    vt = _dot_nt(wvt_ref[...], h)
    k_ref[...] = kt
    v_ref[...] = vt
    kb_ref[...] = kt.astype(BF16)
    vb_ref[...] = vt.astype(BF16)


def _kvproj(x, g, wkt, wvt, k_stack, v_stack, layer):
    b, t, d = x.shape
    w = wkt.shape[0]
    tm = min(t, 1024)
    stack_blk = pl.BlockSpec((None, None, w, tm), lambda bi, ti, l: (l[0], bi, 0, ti))
    full = lambda shape: pl.BlockSpec(shape, lambda bi, ti, l: (0,) * len(shape))
    grid_spec = pltpu.PrefetchScalarGridSpec(
        num_scalar_prefetch=1,
        grid=(b, t // tm),
        in_specs=[
            pl.BlockSpec((None, tm, d), lambda bi, ti, l: (bi, ti, 0)),
            full((1, d)), full((w, d)), full((w, d)),
            pl.BlockSpec(memory_space=pl.ANY),
            pl.BlockSpec(memory_space=pl.ANY),
        ],
        out_specs=[stack_blk, stack_blk] + [pl.BlockSpec((None, w, tm), lambda bi, ti, l: (bi, 0, ti))] * 2,
    )
    return pl.pallas_call(
        _kv_kernel,
        grid_spec=grid_spec,
        out_shape=[jax.ShapeDtypeStruct(k_stack.shape, F32), jax.ShapeDtypeStruct(v_stack.shape, F32)]
        + [jax.ShapeDtypeStruct((b, w, t), BF16)] * 2,
        input_output_aliases={5: 0, 6: 1},
        compiler_params=_cparams("arbitrary", "arbitrary"),
        name="kvproj",
    )(jnp.reshape(layer, (1,)).astype(jnp.int32), x, g, wkt, wvt, k_stack, v_stack)


def _conv_short(x_cur, hist_ref, w_ref, b_ref):
    k = w_ref.shape[0]
    first_row = _iota((SUBLANES, x_cur.shape[1]), 0) == 0
    delayed = x_cur
    acc = w_ref[k - 1:k, :] * x_cur
    for j in range(k - 2, -1, -1):
        rolled = pltpu.roll(delayed, 1, axis=0)
        head = jnp.where(first_row, hist_ref[j:j + 1, :], rolled[:SUBLANES])
        delayed = jnp.concatenate([head, rolled[SUBLANES:]], axis=0)
        acc = acc + w_ref[j:j + 1, :] * delayed
    return acc + b_ref[...]


def _lru_kernel(x_ref, g_ref, cs_ref, h0_ref, cw_ref, cb_ref, wa_ref, ba_ref, wx_ref, bx_ref, lam_ref,
                u_ref, hl_ref, cn_ref, xe_ref, a_ref, b_ref, h_ref):
    tt, w = x_ref.shape
    kc = cw_ref.shape[0]

    @pl.when(pl.program_id(1) == 0)
    def _():
        xe_ref[...] = cs_ref[...]
        h_ref[...] = h0_ref[...]

    x_cur = x_ref[...].astype(F32)
    xc = _conv_short(x_cur, xe_ref, cw_ref, cb_ref)
    tail = x_cur[tt - (kc - 1):, :]
    xe_ref[...] = tail
    cn_ref[...] = tail

    xcb = xc.astype(BF16)
    gate_a = _dot(xcb, wa_ref[...]) + ba_ref[...]
    gate_x = _dot(xcb, wx_ref[...]) + bx_ref[...]
    r = _sigmoid(gate_a)
    i = _sigmoid(gate_x)
    log_a = (-LRU_C) * r * _softplus(-lam_ref[...])
    a = jnp.exp(log_a)
    one_m_a2 = (1.0 - a) * (1.0 + a)
    a_ref[...] = a
    root = jnp.where(one_m_a2 > 0.0, one_m_a2 * lax.rsqrt(one_m_a2), 0.0)
    b_ref[...] = root * (i * xc)

    row = _iota((SUBLANES, w), 0)

    def body(s, h):
        off = pl.multiple_of(s * SUBLANES, SUBLANES)
        a8 = a_ref[pl.ds(off, SUBLANES), :]
        b8 = b_ref[pl.ds(off, SUBLANES), :]
        for d in (1, 2, 4):
            keep = row >= d
            a_sh = pltpu.roll(a8, d, axis=0)
            b_sh = pltpu.roll(b8, d, axis=0)
            b8 = jnp.where(keep, a8 * b_sh + b8, b8)
            a8 = jnp.where(keep, a8 * a_sh, a8)
        h8 = a8 * h + b8
        b_ref[pl.ds(off, SUBLANES), :] = h8
        return h8[SUBLANES - 1:SUBLANES, :]

    h_last = lax.fori_loop(0, tt // SUBLANES, body, h_ref[...])
    h_ref[...] = h_last
    hl_ref[...] = h_last
    u_ref[...] = (b_ref[...] * _silu(g_ref[...].astype(F32))).astype(u_ref.dtype)


def _lru_branch(p3, conv_state, h0, pw, tt):
    b, t, _ = p3.shape
    w = conv_state.shape[-1]
    kc = pw["lru_conv_w"].shape[0]
    col = lambda off: off // w
    full = lambda shape: pl.BlockSpec(shape, lambda bi, ti: (0,) * len(shape))
    return pl.pallas_call(
        _lru_kernel,
        grid=(b, t // tt),
        in_specs=[
            pl.BlockSpec((None, tt, w), lambda bi, ti: (bi, ti, col(COL_LRU_X))),
            pl.BlockSpec((None, tt, w), lambda bi, ti: (bi, ti, col(COL_LRU_G))),
            pl.BlockSpec((None, kc - 1, w), lambda bi, ti: (bi, 0, 0)),
            pl.BlockSpec((None, 1, w), lambda bi, ti: (bi, 0, 0)),
            full((kc, w)), full((1, w)), full((w, w)), full((1, w)), full((w, w)), full((1, w)), full((1, w)),
        ],
        out_specs=[
            pl.BlockSpec((None, tt, w), lambda bi, ti: (bi, ti, 0)),
            pl.BlockSpec((None, 1, w), lambda bi, ti: (bi, 0, 0)),
            pl.BlockSpec((None, kc - 1, w), lambda bi, ti: (bi, 0, 0)),
        ],
        out_shape=[
            jax.ShapeDtypeStruct((b, t, w), BF16),
            jax.ShapeDtypeStruct((b, 1, w), F32),
            jax.ShapeDtypeStruct((b, kc - 1, w), F32),
        ],
        scratch_shapes=[
            pltpu.VMEM((kc - 1, w), F32),
            pltpu.VMEM((tt, w), F32),
            pltpu.VMEM((tt, w), F32),
            pltpu.VMEM((1, w), F32),
        ],
        compiler_params=_cparams("parallel", "arbitrary"),
        name="lru_branch",
    )(p3, p3, conv_state, h0, pw["lru_conv_w"], pw["lru_conv_b"], pw["lru_wa"], pw["lru_ba"],
      pw["lru_wx"], pw["lru_bx"], pw["lru_lambda"])


def _ssd_kernel(z_ref, xbc_ref, dt_ref, cs_ref, s0_ref, cw_ref, cb_ref, dtb_ref, alog_ref, dvec_ref, nw_ref,
                u_ref, sn_ref, cn_ref, xe_ref, st_ref, *, L):
    tile, inner = z_ref.shape
    kc = cw_ref.shape[0]

    @pl.when(pl.program_id(1) == 0)
    def _():
        xe_ref[...] = cs_ref[...]
        st_ref[...] = s0_ref[...]

    x_cur = xbc_ref[...].astype(F32)
    xa_all = _silu(_conv_short(x_cur, xe_ref, cw_ref, cb_ref))
    tail = x_cur[tile - (kc - 1):, :]
    xe_ref[...] = tail
    cn_ref[...] = tail

    dt_all = _softplus(dt_ref[...].astype(F32) + dtb_ref[...])
    adt_all = dt_all * (-jnp.exp(alog_ref[...]))

    for c in range(tile // L):
        rows = slice(c * L, (c + 1) * L)
        y = _ssd_chunk(xa_all[rows], dt_all[rows], adt_all[rows], st_ref, dvec_ref, inner, L)
        gated = y * _silu(z_ref[rows, :].astype(F32))
        ms2 = jnp.mean(gated * gated, axis=-1, keepdims=True)
        u_ref[rows, :] = (gated * lax.rsqrt(ms2 + EPS) * nw_ref[...]).astype(u_ref.dtype)
    sn_ref[...] = st_ref[...]


def _ssd_chunk(xa, dt, adt, st_ref, dvec_ref, inner, L):
    gw = SSD_R * SSD_P
    x = xa[:, :inner]
    bm = xa[:, inner:inner + SSD_G * SSD_N]
    cm = xa[:, inner + SSD_G * SSD_N:]

    causal = _iota((L, L), 0) >= _iota((L, L), 1)
    tril = causal.astype(BF16)
    eye = (_iota((LANES, LANES), 0) == _iota((LANES, LANES), 1)).astype(BF16)
    eye_n = eye[:SSD_N, :SSD_N]
    expand_c = (_div_pow2(_iota((LANES, inner), 1), SSD_P) == _iota((LANES, inner), 0)).astype(BF16)
    n_heads = inner // SSD_P
    expand_l = (_div_pow2(_iota((LANES, n_heads * L), 1), L) == _iota((LANES, n_heads * L), 0)).astype(BF16)

    acs = _sel_left(tril, adt)
    acs_t = _sel_nt(eye, acs)
    dt_c = _sel_right(dt, expand_c)
    acs_c = _sel_right(acs, expand_c)
    acs_w = _sel_right(acs, expand_l)

    xdt = x * dt_c
    last = acs_c[L - 1:L, :]
    xs = xdt * jnp.exp(last - acs_c)
    e_acs = jnp.exp(acs_c)
    chunk_decay = jnp.exp(last)
    lane = _iota((L, gw), 1)

    ys = []
    for g in range(SSD_G):
        cg = cm[:, g * SSD_N:(g + 1) * SSD_N].astype(BF16)
        bg = bm[:, g * SSD_N:(g + 1) * SSD_N].astype(BF16)
        cb = _dot_nt(cg, bg)
        ms = []
        for r in range(SSD_R):
            h = g * SSD_R + r
            seg = acs_w[:, h * L:(h + 1) * L] - acs_t[h:h + 1, :]
            dec = jnp.exp(jnp.where(causal, seg, NEG_BIG))
            ms.append((cb * dec).astype(BF16))
        mcat = jnp.concatenate(ms, axis=1)
        xg = xdt[:, g * gw:(g + 1) * gw]
        rhs = jnp.concatenate([jnp.where(lane < SSD_P, xg, 0.0), jnp.where(lane >= SSD_P, xg, 0.0)],
                              axis=0).astype(BF16)
        y_diag = _dot(mcat, rhs)
        st = st_ref[g]
        y_off = _dot(cg, st.astype(BF16)) * e_acs[:, g * gw:(g + 1) * gw]
        bg_t = _dot_nt(eye_n, bg).astype(BF16)
        st_ref[g] = st * chunk_decay[:, g * gw:(g + 1) * gw] + _dot(bg_t, xs[:, g * gw:(g + 1) * gw].astype(BF16))
        ys.append(y_diag + y_off)
    return jnp.concatenate(ys, axis=1) + dvec_ref[...] * x


SSD_CHUNKS_PER_STEP = 2


def _ssd_branch(p3, conv_state, st0, pw, L):
    b, t, _ = p3.shape
    tile = min(t, SSD_CHUNKS_PER_STEP * L)
    cdim = conv_state.shape[-1]
    inner = pw["ssd_norm"].shape[-1]
    kc = pw["ssd_conv_w"].shape[0]
    gw = SSD_R * SSD_P
    full = lambda shape: pl.BlockSpec(shape, lambda bi, ti: (0,) * len(shape))
    return pl.pallas_call(
        functools.partial(_ssd_kernel, L=L),
        grid=(b, t // tile),
        in_specs=[
            pl.BlockSpec((None, tile, inner), lambda bi, ti: (bi, ti, COL_SSD_Z // inner)),
            pl.BlockSpec((None, tile, cdim), lambda bi, ti: (bi, ti, COL_XBC // cdim)),
            pl.BlockSpec((None, tile, LANES), lambda bi, ti: (bi, ti, COL_DT // LANES)),
            pl.BlockSpec((None, kc - 1, cdim), lambda bi, ti: (bi, 0, 0)),
            pl.BlockSpec((None, SSD_G, SSD_N, gw), lambda bi, ti: (bi, 0, 0, 0)),
            full((kc, cdim)), full((1, cdim)), full((1, LANES)), full((1, LANES)), full((1, inner)), full((1, inner)),
        ],
        out_specs=[
            pl.BlockSpec((None, tile, inner), lambda bi, ti: (bi, ti, 0)),
            pl.BlockSpec((None, SSD_G, SSD_N, gw), lambda bi, ti: (bi, 0, 0, 0)),
            pl.BlockSpec((None, kc - 1, cdim), lambda bi, ti: (bi, 0, 0)),
        ],
        out_shape=[
            jax.ShapeDtypeStruct((b, t, inner), BF16),
            jax.ShapeDtypeStruct((b, SSD_G, SSD_N, gw), F32),
            jax.ShapeDtypeStruct((b, kc - 1, cdim), F32),
        ],
        scratch_shapes=[
            pltpu.VMEM((kc - 1, cdim), F32),
            pltpu.VMEM((SSD_G, SSD_N, gw), F32),
        ],
        compiler_params=_cparams("parallel", "arbitrary"),
        name="ssd_branch",
    )(p3, p3, p3, conv_state, st0, pw["ssd_conv_w"], pw["ssd_conv_b"], pw["ssd_dt_bias"], pw["ssd_a_log"],
      pw["ssd_d"], pw["ssd_norm"])


CF_HIST = 32
CF_ROWS = 128


def _cf_kernel(in_ref, g_ref, cs_ref, cw_ref, cb_ref, lg_ref, lb_ref, u_ref, cn_ref, xe_ref):
    tt = in_ref.shape[0]
    w = g_ref.shape[1]
    kc = cw_ref.shape[0]
    hist = CF_HIST - (kc - 1)

    @pl.when(pl.program_id(1) == 0)
    def _():
        xe_ref[hist:CF_HIST, :] = cs_ref[...]

    cf = in_ref[...].astype(F32)
    xe_ref[CF_HIST:CF_HIST + tt, :] = cf[:, :w] * _sigmoid(cf[:, w:])

    n_rows = min(CF_ROWS, tt)

    def body(s, carry):
        base = pl.multiple_of(s * n_rows, n_rows)
        c = None
        for r in range(SUBLANES):
            rows = n_rows + (SUBLANES if r else 0)
            part = None
            for o in range(hist + (r - hist) % SUBLANES, hist + kc, SUBLANES):
                term = cw_ref[o - hist:o - hist + 1, :] * xe_ref[pl.ds(base + (o - r), rows), :]
                part = term if part is None else part + term
            piece = part[r:r + n_rows, :]
            c = piece if c is None else c + piece
        c = c + cb_ref[...]
        mu = jnp.mean(c, axis=-1, keepdims=True)
        cc = c - mu
        var = jnp.mean(cc * cc, axis=-1, keepdims=True)
        ln = cc * lax.rsqrt(var + EPS) * lg_ref[...] + lb_ref[...]
        gate = g_ref[pl.ds(base, n_rows), :].astype(F32)
        u_ref[pl.ds(base, n_rows), :] = (_silu(ln) * _silu(gate)).astype(u_ref.dtype)
        return carry

    lax.fori_loop(0, tt // n_rows, body, 0)
    tail = xe_ref[tt + hist:tt + CF_HIST, :]
    xe_ref[hist:CF_HIST, :] = tail
    cn_ref[...] = tail


def _cf_branch(p3, conv_state, pw, tt):
    b, t, _ = p3.shape
    w = conv_state.shape[-1]
    kc = pw["cf_conv_w"].shape[0]
    full = lambda shape: pl.BlockSpec(shape, lambda bi, ti: (0,) * len(shape))
    return pl.pallas_call(
        _cf_kernel,
        grid=(b, t // tt),
        in_specs=[
            pl.BlockSpec((None, tt, 2 * w), lambda bi, ti: (bi, ti, COL_CF_IN // (2 * w))),
            pl.BlockSpec((None, tt, w), lambda bi, ti: (bi, ti, COL_CF_G // w)),
            pl.BlockSpec((None, kc - 1, w), lambda bi, ti: (bi, 0, 0)),
            full((kc, w)), full((1, w)), full((1, w)), full((1, w)),
        ],
        out_specs=[
            pl.BlockSpec((None, tt, w), lambda bi, ti: (bi, ti, 0)),
            pl.BlockSpec((None, kc - 1, w), lambda bi, ti: (bi, 0, 0)),
        ],
        out_shape=[
            jax.ShapeDtypeStruct((b, t, w), BF16),
            jax.ShapeDtypeStruct((b, kc - 1, w), F32),
        ],
        scratch_shapes=[pltpu.VMEM((tt + CF_HIST, w), F32)],
        compiler_params=_cparams("parallel", "arbitrary"),
        name="cf_branch",
    )(p3, p3, conv_state, pw["cf_conv_w"], pw["cf_conv_b"], pw["cf_ln_g"], pw["cf_ln_b"])


def _sb_kernel(q_ref, kt_ref, vt_ref, g_ref, sums_ref, u_ref, acc_ref, out_ref, *, q_pos0, tk, tq):
    rows_total, w = q_ref.shape
    n_q = rows_total // tq
    n_pairs = w // LANES
    per_pair = LANES // SB_DH
    q_starts = [q_pos0 + (pl.program_id(1) * n_q + i) * tq for i in range(n_q)]
    kds = [(qs + (tq - 1)) // tk for qs in q_starts]
    earliers = [(kd * tk + _iota((tq, tk), 1)) < (qs + _iota((tq, tk), 0)) for kd, qs in zip(kds, q_starts)]
    sums_rhs = sums_ref[...]

    head_of_lane = _div_pow2(_iota((1, LANES), 1), SB_DH)
    lane_masks = [(head_of_lane == r).astype(BF16) for r in range(per_pair)]
    q_masks = [m * (SB_DH ** -0.5) for m in lane_masks]
    head_of_channel = _div_pow2(_iota((LANES, tk), 0), SB_DH)
    channel_masks = [(head_of_channel == r).astype(BF16) for r in range(per_pair)]
    qm = []
    for i in range(n_q):
        per_q = []
        for p in range(n_pairs):
            qp = q_ref[i * tq:(i + 1) * tq, p * LANES:(p + 1) * LANES]
            per_q.append([qp * m for m in q_masks])
        qm.append(per_q)
    half = tq // 2

    acc_ref[...] = jnp.zeros_like(acc_ref)
    out_ref[...] = jnp.zeros_like(out_ref)

    def block(step, masked, nrows):
        stack = lambda parts: jnp.concatenate([x[:nrows] for x in parts], axis=0)
        offs, valids, zs, lsms, hls = [], [], [], [], []
        for i in range(n_q):
            kb = kds[i] - step
            valids.append(kb >= 0)
            offs.append(pl.multiple_of(jnp.maximum(kb, 0) * tk, tk))
            earlier = stack([earliers[i]] * per_pair)
            for p in range(n_pairs):
                kpt = kt_ref[p * LANES:(p + 1) * LANES, pl.ds(offs[i], tk)]
                z = _dot(stack(qm[i][p]), kpt)
                lsm = -_softplus(z)
                if masked:
                    lsm = jnp.where(earlier, lsm, 0.0)
                hi = lsm.astype(BF16)
                lo = (lsm - hi.astype(F32)).astype(BF16)
                zs.append(z)
                lsms.append(lsm)
                hls.append(jnp.concatenate([hi, lo], axis=1))
        sums = _dot(jnp.concatenate(hls, axis=0), sums_rhs)
        tops, rests = [], []
        for i in range(n_q):
            earlier = stack([earliers[i]] * per_pair)
            keep = jnp.where(valids[i], 1.0, 0.0)
            worst_top, worst_rest = None, None
            for p in range(n_pairs):
                n = i * n_pairs + p
                rows = slice(n * per_pair * nrows, (n + 1) * per_pair * nrows)
                acc = stack([acc_ref[i, p, r] for r in range(per_pair)])
                lw = zs[n] + lsms[n] + sums[rows, :tk] + acc
                if masked:
                    lw = jnp.where(earlier, lw, NEG_BIG)
                wgt = jnp.exp(lw)
                if not masked:
                    wgt = wgt * keep
                wgt = wgt.astype(BF16)
                acc = acc + sums[rows, tk:]
                for r in range(per_pair):
                    acc_r = acc[r * nrows:(r + 1) * nrows]
                    acc_ref[i, p, r, :nrows] = acc_r
                    top = acc_r[:half]
                    worst_top = top if worst_top is None else jnp.maximum(worst_top, top)
                    if nrows > half:
                        rest = acc_r[half:]
                        worst_rest = rest if worst_rest is None else jnp.maximum(worst_rest, rest)
                vpt = vt_ref[p * LANES:(p + 1) * LANES, pl.ds(offs[i], tk)]
                v2t = jnp.concatenate([vpt * m for m in channel_masks], axis=1)
                wcat = jnp.concatenate([wgt[r * nrows:(r + 1) * nrows] for r in range(per_pair)], axis=1)
                out_ref[i, p, :nrows] = out_ref[i, p, :nrows] + _dot_nt(wcat, v2t)
            tops.append(jnp.where(valids[i], jnp.max(worst_top), -jnp.inf))
            if nrows > half:
                rests.append(jnp.where(valids[i], jnp.max(worst_rest), -jnp.inf))
        return tuple(tops), tuple(rests)

    def any_live(step, worsts):
        live = [jnp.logical_and(kds[i] - step >= 0, worsts[i] >= F32_EXP_ZERO_BELOW) for i in range(n_q)]
        return functools.reduce(jnp.logical_or, live)

    tops0, rests0 = block(0, True, tq)

    def full_body(c):
        tops, rests = block(c[0], False, tq)
        return c[0] + 1, tops, rests

    step1, tops1, _ = lax.while_loop(lambda c: any_live(c[0], c[2]), full_body, (1, tops0, rests0))

    def half_body(c):
        tops, _ = block(c[0], False, half)
        return c[0] + 1, tops

    lax.while_loop(lambda c: any_live(c[0], c[1]), half_body, (step1, tops1))

    for i in range(n_q):
        o = jnp.concatenate([out_ref[i, p] for p in range(n_pairs)], axis=1)
        gate = g_ref[i * tq:(i + 1) * tq, :].astype(F32)
        u_ref[i * tq:(i + 1) * tq, :] = (o * _silu(gate)).astype(u_ref.dtype)


def _sb_branch(p3, kt_all, vt_all, q_pos0, tq):
    b, t, _ = p3.shape
    w, keys = kt_all.shape[1:]
    tk = SB_KEY_BLOCK
    assert tk % tq == 0 and q_pos0 % tq == 0 and keys % tk == 0 and vt_all.shape == kt_all.shape
    later = jnp.tril(jnp.ones((tk, tk), BF16), k=-1)
    top = jnp.concatenate([later, jnp.ones((tk, tk), BF16)], axis=1)
    sums_rhs = jnp.concatenate([top, top], axis=0)
    n_q = min(SB_QBLOCKS_PER_STEP, t // tq)
    rows = n_q * tq
    return pl.pallas_call(
        functools.partial(_sb_kernel, q_pos0=q_pos0, tk=tk, tq=tq),
        grid=(b, t // rows),
        in_specs=[
            pl.BlockSpec((None, rows, w), lambda bi, qi: (bi, qi, COL_Q // w)),
            pl.BlockSpec((None, w, keys), lambda bi, qi: (bi, 0, 0)),
            pl.BlockSpec((None, w, keys), lambda bi, qi: (bi, 0, 0)),
            pl.BlockSpec((None, rows, w), lambda bi, qi: (bi, qi, COL_SB_G // w)),
            pl.BlockSpec((2 * tk, 2 * tk), lambda bi, qi: (0, 0)),
        ],
        out_specs=pl.BlockSpec((None, rows, w), lambda bi, qi: (bi, qi, 0)),
        out_shape=jax.ShapeDtypeStruct((b, t, w), BF16),
        scratch_shapes=[
            pltpu.VMEM((n_q, w // LANES, LANES // SB_DH, tq, tk), F32),
            pltpu.VMEM((n_q, w // LANES, tq, LANES), F32),
        ],
        compiler_params=_cparams("parallel", "arbitrary"),
        name="sb_branch",
    )(p3, kt_all, vt_all, p3, sums_rhs)


def _mix_kernel(ua_ref, ub_ref, uc_ref, ud_ref, m0_ref, m1_ref, m2_ref, m3_ref, x_ref, wd_ref, wo_ref, g_ref, o_ref):
    us = (ua_ref, ub_ref, uc_ref, ud_ref)
    ms = (m0_ref, m1_ref, m2_ref, m3_ref)
    mixed = None
    for i in range(N_BRANCH):
        y = _dot(us[i][...], wd_ref[i])
        term = _sigmoid(ms[i][...].astype(F32)) * y
        mixed = term if mixed is None else mixed + term
    o = _dot(mixed.astype(BF16), wo_ref[...])
    ms2 = jnp.mean(o * o, axis=-1, keepdims=True)
    o_ref[...] = x_ref[...] + o * lax.rsqrt(ms2 + EPS) * g_ref[...]


def _mix(us, p2, x2, pw):
    n, d = x2.shape
    w = us[0].shape[-1]
    tm = min(n, 512)
    ublk = pl.BlockSpec((tm, w), lambda i: (i, 0))
    mblk = lambda j: pl.BlockSpec((tm, d), lambda i: (i, COL_MERGE // d + j))
    return pl.pallas_call(
        _mix_kernel,
        grid=(n // tm,),
        in_specs=[ublk] * 4 + [mblk(j) for j in range(N_BRANCH)] + [
            pl.BlockSpec((tm, d), lambda i: (i, 0)),
            pl.BlockSpec((N_BRANCH, w, d), lambda i: (0, 0, 0)),
            pl.BlockSpec((d, d), lambda i: (0, 0)),
            pl.BlockSpec((1, d), lambda i: (0, 0)),
        ],
        out_specs=pl.BlockSpec((tm, d), lambda i: (i, 0)),
        out_shape=jax.ShapeDtypeStruct((n, d), F32),
        input_output_aliases={len(us) + N_BRANCH: 0},
        compiler_params=_cparams("parallel"),
        name="mix",
    )(*us, p2, p2, p2, p2, x2, pw["w_down"], pw["w_out"], pw["norm_post"])


def _block_diag(wb):
    n, c, d = wb.shape
    eye = jnp.eye(n, dtype=wb.dtype)
    return (eye[:, None, :, None] * wb[:, :, None, :]).reshape(n * c, n * d)


def _prep_layer(p):
    d_model = p["w_in"].shape[0]
    bw = p["lru_lambda"].shape[-1]
    cdim = p["ssd_conv_b"].shape[-1]
    n_heads = p["ssd_dt_bias"].shape[-1]
    sizes = (bw, bw, bw, cdim, n_heads, 2 * bw, bw, 3 * bw, bw, N_BRANCH * d_model)
    offs = [0]
    for s in sizes:
        offs.append(offs[-1] + s)
    seg = lambda i: p["w_in"][:, offs[i]:offs[i + 1]]
    lru_x, lru_g, ssd_z, xbc, dt, cf_in, cf_g, qkv, sb_g, merge = [seg(i) for i in range(len(sizes))]
    q, k, v = qkv[:, :bw], qkv[:, bw:2 * bw], qkv[:, 2 * bw:]
    zeros = lambda c: jnp.zeros((d_model, c), p["w_in"].dtype)
    main = jnp.concatenate(
        [xbc, cf_in, lru_x, lru_g, ssd_z, cf_g, q, sb_g, merge, dt, zeros(LANES - n_heads),
         zeros(P_COLS - COL_DT - LANES)], axis=1)
    assert main.shape[1] == P_COLS
    pad_h = lambda a, fill: jnp.concatenate([a, jnp.full((LANES - n_heads,), fill, a.dtype)])[None, :]
    row = lambda a: a[None, :].astype(F32)
    return dict(
        w_main=main.astype(BF16), w_kt=k.T.astype(BF16), w_vt=v.T.astype(BF16),
        norm_pre=row(p["norm_pre"]), norm_post=row(p["norm_post"]),
        lru_conv_w=p["lru_conv_w"], lru_conv_b=row(p["lru_conv_b"]),
        lru_wa=_block_diag(p["lru_wa"]).astype(BF16), lru_ba=row(p["lru_ba"]),
        lru_wx=_block_diag(p["lru_wx"]).astype(BF16), lru_bx=row(p["lru_bx"]),
        lru_lambda=row(p["lru_lambda"]),
        ssd_conv_w=p["ssd_conv_w"], ssd_conv_b=row(p["ssd_conv_b"]),
        ssd_dt_bias=pad_h(p["ssd_dt_bias"], 0.0), ssd_a_log=pad_h(p["ssd_a_log"], 0.0),
        ssd_d=jnp.repeat(p["ssd_d"], SSD_P)[None, :], ssd_norm=row(p["ssd_norm"]),
        cf_conv_w=p["cf_conv_w"], cf_conv_b=row(p["cf_conv_b"]),
        cf_ln_g=row(p["cf_ln_g"]), cf_ln_b=row(p["cf_ln_b"]),
        w_down=p["w_down"].astype(BF16), w_out=p["w_out"].astype(BF16),
    )


def _ssd_state_in(s):
    b = s.shape[0]
    s = s.reshape(b, SSD_G, SSD_R, SSD_P, SSD_N)
    return jnp.transpose(s, (0, 1, 4, 2, 3)).reshape(b, SSD_G, SSD_N, SSD_R * SSD_P)


def _ssd_state_out(s):
    b = s.shape[0]
    s = s.reshape(b, SSD_G, SSD_N, SSD_R, SSD_P)
    return jnp.transpose(s, (0, 1, 3, 4, 2)).reshape(b, SSD_G * SSD_R, SSD_P, SSD_N)


def _layer(x, pw, st, past_k, past_v, k_stack, v_stack, layer, *, seq_tile, ssd_chunk, sb_block):
    b, t, d = x.shape
    x2 = x.reshape(b * t, d)
    p2 = _inproj(x2, pw["norm_pre"], pw["w_main"], BF16, P_TILE_N)
    k_stack, v_stack, kt_new, vt_new = _kvproj(x, pw["norm_pre"], pw["w_kt"], pw["w_vt"], k_stack, v_stack, layer)
    p3 = p2.reshape(b, t, P_COLS)

    u_a, lru_h, lru_conv = _lru_branch(p3, st["lru_conv"], st["lru_h"][:, None, :], pw, seq_tile)
    u_b, ssd_new, ssd_conv = _ssd_branch(p3, st["ssd_conv"], _ssd_state_in(st["ssd"]), pw, ssd_chunk)
    u_c, cf_conv = _cf_branch(p3, st["cf_conv"], pw, seq_tile)
    if past_k is None:
        kt_all, vt_all, q_pos0 = kt_new, vt_new, 0
    else:
        q_pos0 = past_k.shape[1]
        pad = -(q_pos0 + t) % SB_KEY_BLOCK

        def with_cache(past, new):
            past_t = jnp.transpose(past, (0, 2, 3, 1)).reshape(b, -1, q_pos0).astype(BF16)
            return jnp.pad(jnp.concatenate([past_t, new], axis=2), ((0, 0), (0, 0), (0, pad)))

        kt_all, vt_all = with_cache(past_k, kt_new), with_cache(past_v, vt_new)
    u_d = _sb_branch(p3, kt_all, vt_all, q_pos0, sb_block)

    w = u_a.shape[-1]
    us = [u.reshape(b * t, w) for u in (u_a, u_b, u_c, u_d)]
    y = _mix(us, p2, x2, pw).reshape(b, t, d)
    new = dict(lru_h=lru_h[:, 0, :], lru_conv=lru_conv, ssd=_ssd_state_out(ssd_new), ssd_conv=ssd_conv,
               cf_conv=cf_conv)
    return y, new, k_stack, v_stack


_STATE_KEYS = ("lru_h", "lru_conv", "ssd", "ssd_conv", "cf_conv")


def kernel(x_prompt, x_sample, state_lru_h, state_lru_conv, state_ssd, state_ssd_conv, state_cf_conv, cache_sb_k, cache_sb_v, norm_pre, norm_post, w_in, lru_conv_w, lru_conv_b, lru_wa, lru_ba, lru_wx, lru_bx, lru_lambda, ssd_conv_w, ssd_conv_b, ssd_dt_bias, ssd_a_log, ssd_d, ssd_norm, cf_conv_w, cf_conv_b, cf_ln_g, cf_ln_b, w_down, w_out):
    raw = dict(norm_pre=norm_pre, norm_post=norm_post, w_in=w_in, lru_conv_w=lru_conv_w, lru_conv_b=lru_conv_b,
               lru_wa=lru_wa, lru_ba=lru_ba, lru_wx=lru_wx, lru_bx=lru_bx, lru_lambda=lru_lambda,
               ssd_conv_w=ssd_conv_w, ssd_conv_b=ssd_conv_b, ssd_dt_bias=ssd_dt_bias, ssd_a_log=ssd_a_log,
               ssd_d=ssd_d, ssd_norm=ssd_norm, cf_conv_w=cf_conv_w, cf_conv_b=cf_conv_b, cf_ln_g=cf_ln_g,
               cf_ln_b=cf_ln_b, w_down=w_down, w_out=w_out)
    pws = jax.vmap(_prep_layer)(raw)

    bp = x_prompt.shape[0]
    dt = x_prompt.dtype
    bw = state_lru_h.shape[-1]
    zero_state = dict(lru_h=jnp.zeros((bp, bw), dt),
                      lru_conv=jnp.zeros((bp,) + state_lru_conv.shape[2:], dt),
                      ssd=jnp.zeros((bp,) + state_ssd.shape[2:], dt),
                      ssd_conv=jnp.zeros((bp,) + state_ssd_conv.shape[2:], dt),
                      cf_conv=jnp.zeros((bp,) + state_cf_conv.shape[2:], dt))

    depth = w_in.shape[0]
    layers = jnp.arange(depth, dtype=jnp.int32)

    def kv_stacks(x):
        shape = (depth, x.shape[0], SB_H * SB_DH, x.shape[1])
        return jnp.zeros(shape, F32), jnp.zeros(shape, F32)

    def kv_out(stack):
        d_, b_, _, t_ = stack.shape
        return jnp.transpose(stack.reshape(d_, b_, SB_H, SB_DH, t_), (0, 1, 4, 2, 3))

    def prompt_step(carry, xs):
        x, k_stack, v_stack = carry
        pw, layer = xs
        y, new, k_stack, v_stack = _layer(x, pw, zero_state, None, None, k_stack, v_stack, layer,
                                          seq_tile=512, ssd_chunk=128, sb_block=128)
        return (y, k_stack, v_stack), tuple(new[k] for k in _STATE_KEYS)

    (y_p, k_p, v_p), p_new = lax.scan(prompt_step, (x_prompt,) + kv_stacks(x_prompt), (pws, layers))

    t_s = x_sample.shape[1]

    def sample_step(carry, xs):
        x, k_stack, v_stack = carry
        pw, st, pk, pv, layer = xs
        y, new, k_stack, v_stack = _layer(x, pw, st, pk, pv, k_stack, v_stack, layer,
                                          seq_tile=t_s, ssd_chunk=t_s, sb_block=t_s)
        return (y, k_stack, v_stack), tuple(new[k] for k in _STATE_KEYS)

    st_s = dict(lru_h=state_lru_h, lru_conv=state_lru_conv, ssd=state_ssd, ssd_conv=state_ssd_conv,
                cf_conv=state_cf_conv)
    (y_s, k_s, v_s), s_new = lax.scan(sample_step, (x_sample,) + kv_stacks(x_sample),
                                      (pws, st_s, cache_sb_k, cache_sb_v, layers))

    outs = [y_p, y_s]
    for i in range(len(_STATE_KEYS)):
        outs += [p_new[i], s_new[i]]
    outs += [kv_out(k_p), kv_out(k_s), kv_out(v_p), kv_out(v_s)]
    return tuple(outs)
```

```python
import functools

import jax
import jax.numpy as jnp
from jax import lax
from jax.experimental import pallas as pl
from jax.experimental.pallas import tpu as pltpu

F32 = jnp.float32
BF16 = jnp.bfloat16

EPS = 1e-6
LRU_C = 8.0
N_BRANCH = 4
SSD_P = 64
SSD_N = 64
SSD_G = 4
SSD_R = 2
SB_H = 8
SB_DH = 64
SB_KEY_BLOCK = 128
SB_QBLOCKS_PER_STEP = 4

LANES = 128
SUBLANES = 8
VMEM_LIMIT_BYTES = 56 * 1024 * 1024

F32_EXP_ZERO_BELOW = -104.0
NEG_BIG = -1e30

COL_XBC = 0
COL_CF_IN = 1024
COL_LRU_X = 2048
COL_LRU_G = 2560
COL_SSD_Z = 3072
COL_CF_G = 3584
COL_Q = 4096
COL_SB_G = 4608
COL_MERGE = 5120
COL_DT = 9216
P_COLS = 9472
P_TILE_N = P_COLS // 2
INPROJ_CHUNK = 512


def _cparams(*sem):
    return pltpu.CompilerParams(dimension_semantics=sem, vmem_limit_bytes=VMEM_LIMIT_BYTES)


def _sigmoid(x):
    return 0.5 * jnp.tanh(0.5 * x) + 0.5


def _silu(x):
    return x * _sigmoid(x)


def _softplus(x):
    return jnp.maximum(x, 0.0) + jnp.log(1.0 + jnp.exp(-jnp.abs(x)))


def _split3(x):
    hi = x.astype(BF16)
    r = x - hi.astype(F32)
    mid = r.astype(BF16)
    lo = (r - mid.astype(F32)).astype(BF16)
    return hi, mid, lo


def _dot(a, b):
    return jnp.dot(a, b, preferred_element_type=F32)


def _dot_nt(a, b):
    return lax.dot_general(a, b, (((1,), (1,)), ((), ())), preferred_element_type=F32)


def _sel_left(m01, x):
    hi, mid, lo = _split3(x)
    return _dot(m01, hi) + _dot(m01, mid) + _dot(m01, lo)


def _sel_right(x, m01):
    hi, mid, lo = _split3(x)
    return _dot(hi, m01) + _dot(mid, m01) + _dot(lo, m01)


def _sel_nt(m01, x):
    hi, mid, lo = _split3(x)
    return _dot_nt(m01, hi) + _dot_nt(m01, mid) + _dot_nt(m01, lo)


def _iota(shape, dim):
    return lax.broadcasted_iota(jnp.int32, shape, dim)


def _div_pow2(x, n):
    shift = n.bit_length() - 1
    assert 1 << shift == n
    return lax.shift_right_logical(x, shift)


def _inproj_kernel(x_ref, g_ref, w_ref, o_ref, h_ref):
    @pl.when(pl.program_id(1) == 0)
    def _():
        x = x_ref[...]
        ms = jnp.mean(x * x, axis=-1, keepdims=True)
        h_ref[...] = (x * lax.rsqrt(ms + EPS) * g_ref[...]).astype(BF16)

    h = h_ref[...]
    tn = o_ref.shape[1]
    for c0 in range(0, tn, INPROJ_CHUNK):
        c1 = min(c0 + INPROJ_CHUNK, tn)
        o_ref[:, c0:c1] = _dot(h, w_ref[:, c0:c1]).astype(o_ref.dtype)


def _inproj(x2, g, w, out_dtype, tn):
    n, d = x2.shape
    cols = w.shape[1]
    tm = min(n, 1024)
    return pl.pallas_call(
        _inproj_kernel,
        grid=(n // tm, cols // tn),
        in_specs=[
            pl.BlockSpec((tm, d), lambda i, j: (i, 0)),
            pl.BlockSpec((1, d), lambda i, j: (0, 0)),
            pl.BlockSpec((d, tn), lambda i, j: (0, j)),
        ],
        out_specs=pl.BlockSpec((tm, tn), lambda i, j: (i, j)),
        out_shape=jax.ShapeDtypeStruct((n, cols), out_dtype),
        scratch_shapes=[pltpu.VMEM((tm, d), BF16)],
        compiler_params=_cparams("parallel", "arbitrary"),
        name="inproj_" + jnp.dtype(out_dtype).name,
    )(x2, g, w)


def _kv_kernel(layer_ref, x_ref, g_ref, wkt_ref, wvt_ref, kin_ref, vin_ref, k_ref, v_ref, kb_ref, vb_ref):
    del layer_ref, kin_ref, vin_ref
    x = x_ref[...]
    ms = jnp.mean(x * x, axis=-1, keepdims=True)
    h = (x * lax.rsqrt(ms + EPS) * g_ref[...]).astype(BF16)
    kt = _dot_nt(wkt_ref[...], h)
    vt = _dot_nt(wvt_ref[...], h)
    k_ref[...] = kt
    v_ref[...] = vt
    kb_ref[...] = kt.astype(BF16)
    vb_ref[...] = vt.astype(BF16)


def _kvproj(x, g, wkt, wvt, k_stack, v_stack, layer):
    b, t, d = x.shape
    w = wkt.shape[0]
    tm = min(t, 1024)
    stack_blk = pl.BlockSpec((None, None, w, tm), lambda bi, ti, l: (l[0], bi, 0, ti))
    full = lambda shape: pl.BlockSpec(shape, lambda bi, ti, l: (0,) * len(shape))
    grid_spec = pltpu.PrefetchScalarGridSpec(
        num_scalar_prefetch=1,
        grid=(b, t // tm),
        in_specs=[
            pl.BlockSpec((None, tm, d), lambda bi, ti, l: (bi, ti, 0)),
            full((1, d)), full((w, d)), full((w, d)),
            pl.BlockSpec(memory_space=pl.ANY),
            pl.BlockSpec(memory_space=pl.ANY),
        ],
        out_specs=[stack_blk, stack_blk] + [pl.BlockSpec((None, w, tm), lambda bi, ti, l: (bi, 0, ti))] * 2,
    )
    return pl.pallas_call(
        _kv_kernel,
        grid_spec=grid_spec,
        out_shape=[jax.ShapeDtypeStruct(k_stack.shape, F32), jax.ShapeDtypeStruct(v_stack.shape, F32)]
        + [jax.ShapeDtypeStruct((b, w, t), BF16)] * 2,
        input_output_aliases={5: 0, 6: 1},
        compiler_params=_cparams("arbitrary", "arbitrary"),
        name="kvproj",
    )(jnp.reshape(layer, (1,)).astype(jnp.int32), x, g, wkt, wvt, k_stack, v_stack)


def _conv_short(x_cur, hist_ref, w_ref, b_ref):
    k = w_ref.shape[0]
    first_row = _iota((SUBLANES, x_cur.shape[1]), 0) == 0
    delayed = x_cur
    acc = w_ref[k - 1:k, :] * x_cur
    for j in range(k - 2, -1, -1):
        rolled = pltpu.roll(delayed, 1, axis=0)
        head = jnp.where(first_row, hist_ref[j:j + 1, :], rolled[:SUBLANES])
        delayed = jnp.concatenate([head, rolled[SUBLANES:]], axis=0)
        acc = acc + w_ref[j:j + 1, :] * delayed
    return acc + b_ref[...]


def _lru_kernel(x_ref, g_ref, cs_ref, h0_ref, cw_ref, cb_ref, wa_ref, ba_ref, wx_ref, bx_ref, lam_ref,
                u_ref, hl_ref, cn_ref, xe_ref, a_ref, b_ref, h_ref):
    tt, w = x_ref.shape
    kc = cw_ref.shape[0]

    @pl.when(pl.program_id(1) == 0)
    def _():
        xe_ref[...] = cs_ref[...]
        h_ref[...] = h0_ref[...]

    x_cur = x_ref[...].astype(F32)
    xc = _conv_short(x_cur, xe_ref, cw_ref, cb_ref)
    tail = x_cur[tt - (kc - 1):, :]
    xe_ref[...] = tail
    cn_ref[...] = tail

    xcb = xc.astype(BF16)
    gate_a = _dot(xcb, wa_ref[...]) + ba_ref[...]
    gate_x = _dot(xcb, wx_ref[...]) + bx_ref[...]
    r = _sigmoid(gate_a)
    i = _sigmoid(gate_x)
    log_a = (-LRU_C) * r * _softplus(-lam_ref[...])
    a = jnp.exp(log_a)
    one_m_a2 = (1.0 - a) * (1.0 + a)
    a_ref[...] = a
    root = jnp.where(one_m_a2 > 0.0, one_m_a2 * lax.rsqrt(one_m_a2), 0.0)
    b_ref[...] = root * (i * xc)

    row = _iota((SUBLANES, w), 0)

    def body(s, h):
        off = pl.multiple_of(s * SUBLANES, SUBLANES)
        a8 = a_ref[pl.ds(off, SUBLANES), :]
        b8 = b_ref[pl.ds(off, SUBLANES), :]
        for d in (1, 2, 4):
            keep = row >= d
            a_sh = pltpu.roll(a8, d, axis=0)
            b_sh = pltpu.roll(b8, d, axis=0)
            b8 = jnp.where(keep, a8 * b_sh + b8, b8)
            a8 = jnp.where(keep, a8 * a_sh, a8)
        h8 = a8 * h + b8
        b_ref[pl.ds(off, SUBLANES), :] = h8
        return h8[SUBLANES - 1:SUBLANES, :]

    h_last = lax.fori_loop(0, tt // SUBLANES, body, h_ref[...])
    h_ref[...] = h_last
    hl_ref[...] = h_last
    u_ref[...] = (b_ref[...] * _silu(g_ref[...].astype(F32))).astype(u_ref.dtype)


def _lru_branch(p3, conv_state, h0, pw, tt):
    b, t, _ = p3.shape
    w = conv_state.shape[-1]
    kc = pw["lru_conv_w"].shape[0]
    col = lambda off: off // w
    full = lambda shape: pl.BlockSpec(shape, lambda bi, ti: (0,) * len(shape))
    return pl.pallas_call(
        _lru_kernel,
        grid=(b, t // tt),
        in_specs=[
            pl.BlockSpec((None, tt, w), lambda bi, ti: (bi, ti, col(COL_LRU_X))),
            pl.BlockSpec((None, tt, w), lambda bi, ti: (bi, ti, col(COL_LRU_G))),
            pl.BlockSpec((None, kc - 1, w), lambda bi, ti: (bi, 0, 0)),
            pl.BlockSpec((None, 1, w), lambda bi, ti: (bi, 0, 0)),
            full((kc, w)), full((1, w)), full((w, w)), full((1, w)), full((w, w)), full((1, w)), full((1, w)),
        ],
        out_specs=[
            pl.BlockSpec((None, tt, w), lambda bi, ti: (bi, ti, 0)),
            pl.BlockSpec((None, 1, w), lambda bi, ti: (bi, 0, 0)),
            pl.BlockSpec((None, kc - 1, w), lambda bi, ti: (bi, 0, 0)),
        ],
        out_shape=[
            jax.ShapeDtypeStruct((b, t, w), BF16),
            jax.ShapeDtypeStruct((b, 1, w), F32),
            jax.ShapeDtypeStruct((b, kc - 1, w), F32),
        ],
        scratch_shapes=[
            pltpu.VMEM((kc - 1, w), F32),
            pltpu.VMEM((tt, w), F32),
            pltpu.VMEM((tt, w), F32),
            pltpu.VMEM((1, w), F32),
        ],
        compiler_params=_cparams("parallel", "arbitrary"),
        name="lru_branch",
    )(p3, p3, conv_state, h0, pw["lru_conv_w"], pw["lru_conv_b"], pw["lru_wa"], pw["lru_ba"],
      pw["lru_wx"], pw["lru_bx"], pw["lru_lambda"])


def _ssd_kernel(z_ref, xbc_ref, dt_ref, cs_ref, s0_ref, cw_ref, cb_ref, dtb_ref, alog_ref, dvec_ref, nw_ref,
                u_ref, sn_ref, cn_ref, xe_ref, st_ref, *, L):
    tile, inner = z_ref.shape
    kc = cw_ref.shape[0]

    @pl.when(pl.program_id(1) == 0)
    def _():
        xe_ref[...] = cs_ref[...]
        st_ref[...] = s0_ref[...]

    x_cur = xbc_ref[...].astype(F32)
    xa_all = _silu(_conv_short(x_cur, xe_ref, cw_ref, cb_ref))
    tail = x_cur[tile - (kc - 1):, :]
    xe_ref[...] = tail
    cn_ref[...] = tail

    dt_all = _softplus(dt_ref[...].astype(F32) + dtb_ref[...])
    adt_all = dt_all * (-jnp.exp(alog_ref[...]))

    for c in range(tile // L):
        rows = slice(c * L, (c + 1) * L)
        y = _ssd_chunk(xa_all[rows], dt_all[rows], adt_all[rows], st_ref, dvec_ref, inner, L)
        gated = y * _silu(z_ref[rows, :].astype(F32))
        ms2 = jnp.mean(gated * gated, axis=-1, keepdims=True)
        u_ref[rows, :] = (gated * lax.rsqrt(ms2 + EPS) * nw_ref[...]).astype(u_ref.dtype)
    sn_ref[...] = st_ref[...]


def _ssd_chunk(xa, dt, adt, st_ref, dvec_ref, inner, L):
    gw = SSD_R * SSD_P
    x = xa[:, :inner]
    bm = xa[:, inner:inner + SSD_G * SSD_N]
    cm = xa[:, inner + SSD_G * SSD_N:]

    causal = _iota((L, L), 0) >= _iota((L, L), 1)
    tril = causal.astype(BF16)
    eye = (_iota((LANES, LANES), 0) == _iota((LANES, LANES), 1)).astype(BF16)
    eye_n = eye[:SSD_N, :SSD_N]
    expand_c = (_div_pow2(_iota((LANES, inner), 1), SSD_P) == _iota((LANES, inner), 0)).astype(BF16)
    n_heads = inner // SSD_P
    expand_l = (_div_pow2(_iota((LANES, n_heads * L), 1), L) == _iota((LANES, n_heads * L), 0)).astype(BF16)

    acs = _sel_left(tril, adt)
    acs_t = _sel_nt(eye, acs)
    dt_c = _sel_right(dt, expand_c)
    acs_c = _sel_right(acs, expand_c)
    acs_w = _sel_right(acs, expand_l)

    xdt = x * dt_c
    last = acs_c[L - 1:L, :]
    xs = xdt * jnp.exp(last - acs_c)
    e_acs = jnp.exp(acs_c)
    chunk_decay = jnp.exp(last)
    lane = _iota((L, gw), 1)

    ys = []
    for g in range(SSD_G):
        cg = cm[:, g * SSD_N:(g + 1) * SSD_N].astype(BF16)
        bg = bm[:, g * SSD_N:(g + 1) * SSD_N].astype(BF16)
        cb = _dot_nt(cg, bg)
        ms = []
        for r in range(SSD_R):
            h = g * SSD_R + r
            seg = acs_w[:, h * L:(h + 1) * L] - acs_t[h:h + 1, :]
            dec = jnp.exp(jnp.where(causal, seg, NEG_BIG))
            ms.append((cb * dec).astype(BF16))
        mcat = jnp.concatenate(ms, axis=1)
        xg = xdt[:, g * gw:(g + 1) * gw]
        rhs = jnp.concatenate([jnp.where(lane < SSD_P, xg, 0.0), jnp.where(lane >= SSD_P, xg, 0.0)],
                              axis=0).astype(BF16)
        y_diag = _dot(mcat, rhs)
        st = st_ref[g]
        y_off = _dot(cg, st.astype(BF16)) * e_acs[:, g * gw:(g + 1) * gw]
        bg_t = _dot_nt(eye_n, bg).astype(BF16)
        st_ref[g] = st * chunk_decay[:, g * gw:(g + 1) * gw] + _dot(bg_t, xs[:, g * gw:(g + 1) * gw].astype(BF16))
        ys.append(y_diag + y_off)
    return jnp.concatenate(ys, axis=1) + dvec_ref[...] * x


SSD_CHUNKS_PER_STEP = 2


def _ssd_branch(p3, conv_state, st0, pw, L):
    b, t, _ = p3.shape
    tile = min(t, SSD_CHUNKS_PER_STEP * L)
    cdim = conv_state.shape[-1]
    inner = pw["ssd_norm"].shape[-1]
    kc = pw["ssd_conv_w"].shape[0]
    gw = SSD_R * SSD_P
    full = lambda shape: pl.BlockSpec(shape, lambda bi, ti: (0,) * len(shape))
    return pl.pallas_call(
        functools.partial(_ssd_kernel, L=L),
        grid=(b, t // tile),
        in_specs=[
            pl.BlockSpec((None, tile, inner), lambda bi, ti: (bi, ti, COL_SSD_Z // inner)),
            pl.BlockSpec((None, tile, cdim), lambda bi, ti: (bi, ti, COL_XBC // cdim)),
            pl.BlockSpec((None, tile, LANES), lambda bi, ti: (bi, ti, COL_DT // LANES)),
            pl.BlockSpec((None, kc - 1, cdim), lambda bi, ti: (bi, 0, 0)),
            pl.BlockSpec((None, SSD_G, SSD_N, gw), lambda bi, ti: (bi, 0, 0, 0)),
            full((kc, cdim)), full((1, cdim)), full((1, LANES)), full((1, LANES)), full((1, inner)), full((1, inner)),
        ],
        out_specs=[
            pl.BlockSpec((None, tile, inner), lambda bi, ti: (bi, ti, 0)),
            pl.BlockSpec((None, SSD_G, SSD_N, gw), lambda bi, ti: (bi, 0, 0, 0)),
            pl.BlockSpec((None, kc - 1, cdim), lambda bi, ti: (bi, 0, 0)),
        ],
        out_shape=[
            jax.ShapeDtypeStruct((b, t, inner), BF16),
            jax.ShapeDtypeStruct((b, SSD_G, SSD_N, gw), F32),
            jax.ShapeDtypeStruct((b, kc - 1, cdim), F32),
        ],
        scratch_shapes=[
            pltpu.VMEM((kc - 1, cdim), F32),
            pltpu.VMEM((SSD_G, SSD_N, gw), F32),
        ],
        compiler_params=_cparams("parallel", "arbitrary"),
        name="ssd_branch",
    )(p3, p3, p3, conv_state, st0, pw["ssd_conv_w"], pw["ssd_conv_b"], pw["ssd_dt_bias"], pw["ssd_a_log"],
      pw["ssd_d"], pw["ssd_norm"])


CF_HIST = 32
CF_ROWS = 128


def _cf_kernel(in_ref, g_ref, cs_ref, cw_ref, cb_ref, lg_ref, lb_ref, u_ref, cn_ref, xe_ref):
    tt = in_ref.shape[0]
    w = g_ref.shape[1]
    kc = cw_ref.shape[0]
    hist = CF_HIST - (kc - 1)

    @pl.when(pl.program_id(1) == 0)
    def _():
        xe_ref[hist:CF_HIST, :] = cs_ref[...]

    cf = in_ref[...].astype(F32)
    xe_ref[CF_HIST:CF_HIST + tt, :] = cf[:, :w] * _sigmoid(cf[:, w:])

    n_rows = min(CF_ROWS, tt)

    def body(s, carry):
        base = pl.multiple_of(s * n_rows, n_rows)
        c = None
        for r in range(SUBLANES):
            rows = n_rows + (SUBLANES if r else 0)
            part = None
            for o in range(hist + (r - hist) % SUBLANES, hist + kc, SUBLANES):
                term = cw_ref[o - hist:o - hist + 1, :] * xe_ref[pl.ds(base + (o - r), rows), :]
                part = term if part is None else part + term
            piece = part[r:r + n_rows, :]
            c = piece if c is None else c + piece
        c = c + cb_ref[...]
        mu = jnp.mean(c, axis=-1, keepdims=True)
        cc = c - mu
        var = jnp.mean(cc * cc, axis=-1, keepdims=True)
        ln = cc * lax.rsqrt(var + EPS) * lg_ref[...] + lb_ref[...]
        gate = g_ref[pl.ds(base, n_rows), :].astype(F32)
        u_ref[pl.ds(base, n_rows), :] = (_silu(ln) * _silu(gate)).astype(u_ref.dtype)
        return carry

    lax.fori_loop(0, tt // n_rows, body, 0)
    tail = xe_ref[tt + hist:tt + CF_HIST, :]
    xe_ref[hist:CF_HIST, :] = tail
    cn_ref[...] = tail


def _cf_branch(p3, conv_state, pw, tt):
    b, t, _ = p3.shape
    w = conv_state.shape[-1]
    kc = pw["cf_conv_w"].shape[0]
    full = lambda shape: pl.BlockSpec(shape, lambda bi, ti: (0,) * len(shape))
    return pl.pallas_call(
        _cf_kernel,
        grid=(b, t // tt),
        in_specs=[
            pl.BlockSpec((None, tt, 2 * w), lambda bi, ti: (bi, ti, COL_CF_IN // (2 * w))),
            pl.BlockSpec((None, tt, w), lambda bi, ti: (bi, ti, COL_CF_G // w)),
            pl.BlockSpec((None, kc - 1, w), lambda bi, ti: (bi, 0, 0)),
            full((kc, w)), full((1, w)), full((1, w)), full((1, w)),
        ],
        out_specs=[
            pl.BlockSpec((None, tt, w), lambda bi, ti: (bi, ti, 0)),
            pl.BlockSpec((None, kc - 1, w), lambda bi, ti: (bi, 0, 0)),
        ],
        out_shape=[
            jax.ShapeDtypeStruct((b, t, w), BF16),
            jax.ShapeDtypeStruct((b, kc - 1, w), F32),
        ],
        scratch_shapes=[pltpu.VMEM((tt + CF_HIST, w), F32)],
        compiler_params=_cparams("parallel", "arbitrary"),
        name="cf_branch",
    )(p3, p3, conv_state, pw["cf_conv_w"], pw["cf_conv_b"], pw["cf_ln_g"], pw["cf_ln_b"])


N_LRU_IN, N_LRU_OUT, N_LRU_SCRATCH = 11, 3, 4
N_CF_IN, N_CF_OUT = 7, 2


def _lru_cf_kernel(*refs):
    ins, rest = refs[:N_LRU_IN + N_CF_IN], refs[N_LRU_IN + N_CF_IN:]
    outs, scratch = rest[:N_LRU_OUT + N_CF_OUT], rest[N_LRU_OUT + N_CF_OUT:]
    _lru_kernel(*ins[:N_LRU_IN], *outs[:N_LRU_OUT], *scratch[:N_LRU_SCRATCH])
    _cf_kernel(*ins[N_LRU_IN:], *outs[N_LRU_OUT:], *scratch[N_LRU_SCRATCH:])


def _lru_cf_branch(p3, lru_conv_state, h0, cf_conv_state, pw, tt):
    b, t, _ = p3.shape
    w = lru_conv_state.shape[-1]
    kl = pw["lru_conv_w"].shape[0]
    kc = pw["cf_conv_w"].shape[0]
    full = lambda shape: pl.BlockSpec(shape, lambda bi, ti: (0,) * len(shape))
    tile = lambda width, off: pl.BlockSpec((None, tt, width), lambda bi, ti: (bi, ti, off // width))
    per_batch = lambda rows: pl.BlockSpec((None, rows, w), lambda bi, ti: (bi, 0, 0))
    in_specs = [
        tile(w, COL_LRU_X), tile(w, COL_LRU_G), per_batch(kl - 1), per_batch(1),
        full((kl, w)), full((1, w)), full((w, w)), full((1, w)), full((w, w)), full((1, w)), full((1, w)),
        tile(2 * w, COL_CF_IN), tile(w, COL_CF_G), per_batch(kc - 1),
        full((kc, w)), full((1, w)), full((1, w)), full((1, w)),
    ]
    assert len(in_specs) == N_LRU_IN + N_CF_IN
    out_tile = pl.BlockSpec((None, tt, w), lambda bi, ti: (bi, ti, 0))
    return pl.pallas_call(
        _lru_cf_kernel,
        grid=(b, t // tt),
        in_specs=in_specs,
        out_specs=[out_tile, per_batch(1), per_batch(kl - 1), out_tile, per_batch(kc - 1)],
        out_shape=[
            jax.ShapeDtypeStruct((b, t, w), BF16),
            jax.ShapeDtypeStruct((b, 1, w), F32),
            jax.ShapeDtypeStruct((b, kl - 1, w), F32),
            jax.ShapeDtypeStruct((b, t, w), BF16),
            jax.ShapeDtypeStruct((b, kc - 1, w), F32),
        ],
        scratch_shapes=[
            pltpu.VMEM((kl - 1, w), F32), pltpu.VMEM((tt, w), F32), pltpu.VMEM((tt, w), F32), pltpu.VMEM((1, w), F32),
            pltpu.VMEM((tt + CF_HIST, w), F32),
        ],
        compiler_params=_cparams("parallel", "arbitrary"),
        name="lru_cf_branch",
    )(p3, p3, lru_conv_state, h0, pw["lru_conv_w"], pw["lru_conv_b"], pw["lru_wa"], pw["lru_ba"],
      pw["lru_wx"], pw["lru_bx"], pw["lru_lambda"],
      p3, p3, cf_conv_state, pw["cf_conv_w"], pw["cf_conv_b"], pw["cf_ln_g"], pw["cf_ln_b"])


def _sb_kernel(q_ref, kt_ref, vt_ref, g_ref, sums_ref, u_ref, acc_ref, out_ref, *, q_pos0, tk, tq):
    rows_total, w = q_ref.shape
    n_q = rows_total // tq
    n_pairs = w // LANES
    per_pair = LANES // SB_DH
    q_starts = [q_pos0 + (pl.program_id(1) * n_q + i) * tq for i in range(n_q)]
    kds = [(qs + (tq - 1)) // tk for qs in q_starts]
    earliers = [(kd * tk + _iota((tq, tk), 1)) < (qs + _iota((tq, tk), 0)) for kd, qs in zip(kds, q_starts)]
    sums_rhs = sums_ref[...]

    head_of_lane = _div_pow2(_iota((1, LANES), 1), SB_DH)
    lane_masks = [(head_of_lane == r).astype(BF16) for r in range(per_pair)]
    q_masks = [m * (SB_DH ** -0.5) for m in lane_masks]
    head_of_channel = _div_pow2(_iota((LANES, tk), 0), SB_DH)
    channel_masks = [(head_of_channel == r).astype(BF16) for r in range(per_pair)]
    qm = []
    for i in range(n_q):
        per_q = []
        for p in range(n_pairs):
            qp = q_ref[i * tq:(i + 1) * tq, p * LANES:(p + 1) * LANES]
            per_q.append([qp * m for m in q_masks])
        qm.append(per_q)
    half = tq // 2

    acc_ref[...] = jnp.zeros_like(acc_ref)
    out_ref[...] = jnp.zeros_like(out_ref)

    def block(step, masked, nrows):
        stack = lambda parts: jnp.concatenate([x[:nrows] for x in parts], axis=0)
        offs, valids, zs, lsms, hls = [], [], [], [], []
        for i in range(n_q):
            kb = kds[i] - step
            valids.append(kb >= 0)
            offs.append(pl.multiple_of(jnp.maximum(kb, 0) * tk, tk))
            earlier = stack([earliers[i]] * per_pair)
            for p in range(n_pairs):
                kpt = kt_ref[p * LANES:(p + 1) * LANES, pl.ds(offs[i], tk)]
                z = _dot(stack(qm[i][p]), kpt)
                lsm = -_softplus(z)
                if masked:
                    lsm = jnp.where(earlier, lsm, 0.0)
                hi = lsm.astype(BF16)
                lo = (lsm - hi.astype(F32)).astype(BF16)
                zs.append(z)
                lsms.append(lsm)
                hls.append(jnp.concatenate([hi, lo], axis=1))
        sums = _dot(jnp.concatenate(hls, axis=0), sums_rhs)
        tops, rests = [], []
        for i in range(n_q):
            earlier = stack([earliers[i]] * per_pair)
            keep = jnp.where(valids[i], 1.0, 0.0)
            worst_top, worst_rest = None, None
            for p in range(n_pairs):
                n = i * n_pairs + p
                rows = slice(n * per_pair * nrows, (n + 1) * per_pair * nrows)
                acc = stack([acc_ref[i, p, r] for r in range(per_pair)])
                lw = zs[n] + lsms[n] + sums[rows, :tk] + acc
                if masked:
                    lw = jnp.where(earlier, lw, NEG_BIG)
                wgt = jnp.exp(lw)
                if not masked:
                    wgt = wgt * keep
                wgt = wgt.astype(BF16)
                acc = acc + sums[rows, tk:]
                for r in range(per_pair):
                    acc_r = acc[r * nrows:(r + 1) * nrows]
                    acc_ref[i, p, r, :nrows] = acc_r
                    top = acc_r[:half]
                    worst_top = top if worst_top is None else jnp.maximum(worst_top, top)
                    if nrows > half:
                        rest = acc_r[half:]
                        worst_rest = rest if worst_rest is None else jnp.maximum(worst_rest, rest)
                vpt = vt_ref[p * LANES:(p + 1) * LANES, pl.ds(offs[i], tk)]
                v2t = jnp.concatenate([vpt * m for m in channel_masks], axis=1)
                wcat = jnp.concatenate([wgt[r * nrows:(r + 1) * nrows] for r in range(per_pair)], axis=1)
                out_ref[i, p, :nrows] = out_ref[i, p, :nrows] + _dot_nt(wcat, v2t)
            tops.append(jnp.where(valids[i], jnp.max(worst_top), -jnp.inf))
            if nrows > half:
                rests.append(jnp.where(valids[i], jnp.max(worst_rest), -jnp.inf))
        return tuple(tops), tuple(rests)

    def any_live(step, worsts):
        live = [jnp.logical_and(kds[i] - step >= 0, worsts[i] >= F32_EXP_ZERO_BELOW) for i in range(n_q)]
        return functools.reduce(jnp.logical_or, live)

    tops0, rests0 = block(0, True, tq)

    def full_body(c):
        tops, rests = block(c[0], False, tq)
        return c[0] + 1, tops, rests

    step1, tops1, _ = lax.while_loop(lambda c: any_live(c[0], c[2]), full_body, (1, tops0, rests0))

    def half_body(c):
        tops, _ = block(c[0], False, half)
        return c[0] + 1, tops

    lax.while_loop(lambda c: any_live(c[0], c[1]), half_body, (step1, tops1))

    for i in range(n_q):
        o = jnp.concatenate([out_ref[i, p] for p in range(n_pairs)], axis=1)
        gate = g_ref[i * tq:(i + 1) * tq, :].astype(F32)
        u_ref[i * tq:(i + 1) * tq, :] = (o * _silu(gate)).astype(u_ref.dtype)


def _sb_branch(p3, kt_all, vt_all, q_pos0, tq):
    b, t, _ = p3.shape
    w, keys = kt_all.shape[1:]
    tk = SB_KEY_BLOCK
    assert tk % tq == 0 and q_pos0 % tq == 0 and keys % tk == 0 and vt_all.shape == kt_all.shape
    later = jnp.tril(jnp.ones((tk, tk), BF16), k=-1)
    top = jnp.concatenate([later, jnp.ones((tk, tk), BF16)], axis=1)
    sums_rhs = jnp.concatenate([top, top], axis=0)
    n_q = min(SB_QBLOCKS_PER_STEP, t // tq)
    rows = n_q * tq
    return pl.pallas_call(
        functools.partial(_sb_kernel, q_pos0=q_pos0, tk=tk, tq=tq),
        grid=(b, t // rows),
        in_specs=[
            pl.BlockSpec((None, rows, w), lambda bi, qi: (bi, qi, COL_Q // w)),
            pl.BlockSpec((None, w, keys), lambda bi, qi: (bi, 0, 0)),
            pl.BlockSpec((None, w, keys), lambda bi, qi: (bi, 0, 0)),
            pl.BlockSpec((None, rows, w), lambda bi, qi: (bi, qi, COL_SB_G // w)),
            pl.BlockSpec((2 * tk, 2 * tk), lambda bi, qi: (0, 0)),
        ],
        out_specs=pl.BlockSpec((None, rows, w), lambda bi, qi: (bi, qi, 0)),
        out_shape=jax.ShapeDtypeStruct((b, t, w), BF16),
        scratch_shapes=[
            pltpu.VMEM((n_q, w // LANES, LANES // SB_DH, tq, tk), F32),
            pltpu.VMEM((n_q, w // LANES, tq, LANES), F32),
        ],
        compiler_params=_cparams("parallel", "arbitrary"),
        name="sb_branch",
    )(p3, kt_all, vt_all, p3, sums_rhs)


def _mix_kernel(ua_ref, ub_ref, uc_ref, ud_ref, m0_ref, m1_ref, m2_ref, m3_ref, x_ref, wd_ref, wo_ref, g_ref, o_ref):
    us = (ua_ref, ub_ref, uc_ref, ud_ref)
    ms = (m0_ref, m1_ref, m2_ref, m3_ref)
    mixed = None
    for i in range(N_BRANCH):
        y = _dot(us[i][...], wd_ref[i])
        term = _sigmoid(ms[i][...].astype(F32)) * y
        mixed = term if mixed is None else mixed + term
    o = _dot(mixed.astype(BF16), wo_ref[...])
    ms2 = jnp.mean(o * o, axis=-1, keepdims=True)
    o_ref[...] = x_ref[...] + o * lax.rsqrt(ms2 + EPS) * g_ref[...]


def _mix(us, p2, x2, pw):
    n, d = x2.shape
    w = us[0].shape[-1]
    tm = min(n, 512)
    ublk = pl.BlockSpec((tm, w), lambda i: (i, 0))
    mblk = lambda j: pl.BlockSpec((tm, d), lambda i: (i, COL_MERGE // d + j))
    return pl.pallas_call(
        _mix_kernel,
        grid=(n // tm,),
        in_specs=[ublk] * 4 + [mblk(j) for j in range(N_BRANCH)] + [
            pl.BlockSpec((tm, d), lambda i: (i, 0)),
            pl.BlockSpec((N_BRANCH, w, d), lambda i: (0, 0, 0)),
            pl.BlockSpec((d, d), lambda i: (0, 0)),
            pl.BlockSpec((1, d), lambda i: (0, 0)),
        ],
        out_specs=pl.BlockSpec((tm, d), lambda i: (i, 0)),
        out_shape=jax.ShapeDtypeStruct((n, d), F32),
        input_output_aliases={len(us) + N_BRANCH: 0},
        compiler_params=_cparams("parallel"),
        name="mix",
    )(*us, p2, p2, p2, p2, x2, pw["w_down"], pw["w_out"], pw["norm_post"])


def _block_diag(wb):
    n, c, d = wb.shape
    eye = jnp.eye(n, dtype=wb.dtype)
    return (eye[:, None, :, None] * wb[:, :, None, :]).reshape(n * c, n * d)


def _prep_layer(p):
    d_model = p["w_in"].shape[0]
    bw = p["lru_lambda"].shape[-1]
    cdim = p["ssd_conv_b"].shape[-1]
    n_heads = p["ssd_dt_bias"].shape[-1]
    sizes = (bw, bw, bw, cdim, n_heads, 2 * bw, bw, 3 * bw, bw, N_BRANCH * d_model)
    offs = [0]
    for s in sizes:
        offs.append(offs[-1] + s)
    seg = lambda i: p["w_in"][:, offs[i]:offs[i + 1]]
    lru_x, lru_g, ssd_z, xbc, dt, cf_in, cf_g, qkv, sb_g, merge = [seg(i) for i in range(len(sizes))]
    q, k, v = qkv[:, :bw], qkv[:, bw:2 * bw], qkv[:, 2 * bw:]
    zeros = lambda c: jnp.zeros((d_model, c), p["w_in"].dtype)
    main = jnp.concatenate(
        [xbc, cf_in, lru_x, lru_g, ssd_z, cf_g, q, sb_g, merge, dt, zeros(LANES - n_heads),
         zeros(P_COLS - COL_DT - LANES)], axis=1)
    assert main.shape[1] == P_COLS
    pad_h = lambda a, fill: jnp.concatenate([a, jnp.full((LANES - n_heads,), fill, a.dtype)])[None, :]
    row = lambda a: a[None, :].astype(F32)
    return dict(
        w_main=main.astype(BF16), w_kt=k.T.astype(BF16), w_vt=v.T.astype(BF16),
        norm_pre=row(p["norm_pre"]), norm_post=row(p["norm_post"]),
        lru_conv_w=p["lru_conv_w"], lru_conv_b=row(p["lru_conv_b"]),
        lru_wa=_block_diag(p["lru_wa"]).astype(BF16), lru_ba=row(p["lru_ba"]),
        lru_wx=_block_diag(p["lru_wx"]).astype(BF16), lru_bx=row(p["lru_bx"]),
        lru_lambda=row(p["lru_lambda"]),
        ssd_conv_w=p["ssd_conv_w"], ssd_conv_b=row(p["ssd_conv_b"]),
        ssd_dt_bias=pad_h(p["ssd_dt_bias"], 0.0), ssd_a_log=pad_h(p["ssd_a_log"], 0.0),
        ssd_d=jnp.repeat(p["ssd_d"], SSD_P)[None, :], ssd_norm=row(p["ssd_norm"]),
        cf_conv_w=p["cf_conv_w"], cf_conv_b=row(p["cf_conv_b"]),
        cf_ln_g=row(p["cf_ln_g"]), cf_ln_b=row(p["cf_ln_b"]),
        w_down=p["w_down"].astype(BF16), w_out=p["w_out"].astype(BF16),
    )


def _ssd_state_in(s):
    b = s.shape[0]
    s = s.reshape(b, SSD_G, SSD_R, SSD_P, SSD_N)
    return jnp.transpose(s, (0, 1, 4, 2, 3)).reshape(b, SSD_G, SSD_N, SSD_R * SSD_P)


def _ssd_state_out(s):
    b = s.shape[0]
    s = s.reshape(b, SSD_G, SSD_N, SSD_R, SSD_P)
    return jnp.transpose(s, (0, 1, 3, 4, 2)).reshape(b, SSD_G * SSD_R, SSD_P, SSD_N)


def _layer(x, pw, st, past_k, past_v, k_stack, v_stack, layer, *, seq_tile, ssd_chunk, sb_block):
    b, t, d = x.shape
    x2 = x.reshape(b * t, d)
    p2 = _inproj(x2, pw["norm_pre"], pw["w_main"], BF16, P_TILE_N)
    k_stack, v_stack, kt_new, vt_new = _kvproj(x, pw["norm_pre"], pw["w_kt"], pw["w_vt"], k_stack, v_stack, layer)
    p3 = p2.reshape(b, t, P_COLS)

    u_a, lru_h, lru_conv, u_c, cf_conv = _lru_cf_branch(p3, st["lru_conv"], st["lru_h"][:, None, :],
                                                        st["cf_conv"], pw, seq_tile)
    u_b, ssd_new, ssd_conv = _ssd_branch(p3, st["ssd_conv"], _ssd_state_in(st["ssd"]), pw, ssd_chunk)
    if past_k is None:
        kt_all, vt_all, q_pos0 = kt_new, vt_new, 0
    else:
        q_pos0 = past_k.shape[1]
        pad = -(q_pos0 + t) % SB_KEY_BLOCK

        def with_cache(past, new):
            past_t = jnp.transpose(past, (0, 2, 3, 1)).reshape(b, -1, q_pos0).astype(BF16)
            return jnp.pad(jnp.concatenate([past_t, new], axis=2), ((0, 0), (0, 0), (0, pad)))

        kt_all, vt_all = with_cache(past_k, kt_new), with_cache(past_v, vt_new)
    u_d = _sb_branch(p3, kt_all, vt_all, q_pos0, sb_block)

    w = u_a.shape[-1]
    us = [u.reshape(b * t, w) for u in (u_a, u_b, u_c, u_d)]
    y = _mix(us, p2, x2, pw).reshape(b, t, d)
    new = dict(lru_h=lru_h[:, 0, :], lru_conv=lru_conv, ssd=_ssd_state_out(ssd_new), ssd_conv=ssd_conv,
               cf_conv=cf_conv)
    return y, new, k_stack, v_stack


_STATE_KEYS = ("lru_h", "lru_conv", "ssd", "ssd_conv", "cf_conv")


def kernel(x_prompt, x_sample, state_lru_h, state_lru_conv, state_ssd, state_ssd_conv, state_cf_conv, cache_sb_k, cache_sb_v, norm_pre, norm_post, w_in, lru_conv_w, lru_conv_b, lru_wa, lru_ba, lru_wx, lru_bx, lru_lambda, ssd_conv_w, ssd_conv_b, ssd_dt_bias, ssd_a_log, ssd_d, ssd_norm, cf_conv_w, cf_conv_b, cf_ln_g, cf_ln_b, w_down, w_out):
    raw = dict(norm_pre=norm_pre, norm_post=norm_post, w_in=w_in, lru_conv_w=lru_conv_w, lru_conv_b=lru_conv_b,
               lru_wa=lru_wa, lru_ba=lru_ba, lru_wx=lru_wx, lru_bx=lru_bx, lru_lambda=lru_lambda,
               ssd_conv_w=ssd_conv_w, ssd_conv_b=ssd_conv_b, ssd_dt_bias=ssd_dt_bias, ssd_a_log=ssd_a_log,
               ssd_d=ssd_d, ssd_norm=ssd_norm, cf_conv_w=cf_conv_w, cf_conv_b=cf_conv_b, cf_ln_g=cf_ln_g,
               cf_ln_b=cf_ln_b, w_down=w_down, w_out=w_out)
    pws = jax.vmap(_prep_layer)(raw)

    bp = x_prompt.shape[0]
    dt = x_prompt.dtype
    bw = state_lru_h.shape[-1]
    zero_state = dict(lru_h=jnp.zeros((bp, bw), dt),
                      lru_conv=jnp.zeros((bp,) + state_lru_conv.shape[2:], dt),
                      ssd=jnp.zeros((bp,) + state_ssd.shape[2:], dt),
                      ssd_conv=jnp.zeros((bp,) + state_ssd_conv.shape[2:], dt),
                      cf_conv=jnp.zeros((bp,) + state_cf_conv.shape[2:], dt))

    depth = w_in.shape[0]
    layers = jnp.arange(depth, dtype=jnp.int32)

    def kv_stacks(x):
        shape = (depth, x.shape[0], SB_H * SB_DH, x.shape[1])
        return jnp.zeros(shape, F32), jnp.zeros(shape, F32)

    def kv_out(stack):
        d_, b_, _, t_ = stack.shape
        return jnp.transpose(stack.reshape(d_, b_, SB_H, SB_DH, t_), (0, 1, 4, 2, 3))

    def prompt_step(carry, xs):
        x, k_stack, v_stack = carry
        pw, layer = xs
        y, new, k_stack, v_stack = _layer(x, pw, zero_state, None, None, k_stack, v_stack, layer,
                                          seq_tile=512, ssd_chunk=128, sb_block=128)
        return (y, k_stack, v_stack), tuple(new[k] for k in _STATE_KEYS)

    (y_p, k_p, v_p), p_new = lax.scan(prompt_step, (x_prompt,) + kv_stacks(x_prompt), (pws, layers))

    t_s = x_sample.shape[1]

    def sample_step(carry, xs):
        x, k_stack, v_stack = carry
        pw, st, pk, pv, layer = xs
        y, new, k_stack, v_stack = _layer(x, pw, st, pk, pv, k_stack, v_stack, layer,
                                          seq_tile=t_s, ssd_chunk=t_s, sb_block=t_s)
        return (y, k_stack, v_stack), tuple(new[k] for k in _STATE_KEYS)

    st_s = dict(lru_h=state_lru_h, lru_conv=state_lru_conv, ssd=state_ssd, ssd_conv=state_ssd_conv,
                cf_conv=state_cf_conv)
    (y_s, k_s, v_s), s_new = lax.scan(sample_step, (x_sample,) + kv_stacks(x_sample),
                                      (pws, st_s, cache_sb_k, cache_sb_v, layers))

    outs = [y_p, y_s]
    for i in range(len(_STATE_KEYS)):
        outs += [p_new[i], s_new[i]]
    outs += [kv_out(k_p), kv_out(k_s), kv_out(v_p), kv_out(v_s)]
    return tuple(outs)
```
